```python
import math
import jax, jax.numpy as jnp
from jax import lax
import numpy as np

D_MODEL = 1024
BATCH = 4
SEQ = 4096
DEPTH = 1

POOL_WIDTH = 512
POOL_WINDOWS = (2, 4, 8, 16)
POOL_GROUPS = len(POOL_WINDOWS)
POOL_GROUP_WIDTH = POOL_WIDTH // POOL_GROUPS
N_HEADS = 8
HEAD_DIM = 64
ATTN_WIDTH = N_HEADS * HEAD_DIM
N_IDX_HEADS = 4
IDX_DIM = 64
IDX_SCALE = (IDX_DIM ** -0.5) * (N_IDX_HEADS ** -0.5)
ATTN_SCALE = HEAD_DIM ** -0.5
TOPK_MAX = 256
Q_BLOCK = 128
REL_BUCKETS = 32
REL_MAX_DIST = 128
N_BRANCHES = 2
N_EXPERTS = 32
TOP_K_EXPERTS = 4
D_FF = D_MODEL
SWIGLU_LIMIT = 7.0
SWIGLU_ALPHA = 1.702
RMS_EPS = 1e-5
NEG_BIG = -1e30
SPLIT_SIZES = (POOL_WIDTH, ATTN_WIDTH, ATTN_WIDTH, ATTN_WIDTH,
               N_IDX_HEADS * IDX_DIM, IDX_DIM, N_IDX_HEADS, N_BRANCHES * D_MODEL)
D_IN = sum(SPLIT_SIZES)

kernel_name = "hybrid_pool_dsa_gated_moe_block"


def rmsnorm(x, g):
    xf = x.astype(jnp.float32)
    y = xf * lax.rsqrt(jnp.mean(xf * xf, axis=-1, keepdims=True) + RMS_EPS)
    return (y * g.astype(jnp.float32)).astype(x.dtype)


def pool_mixer(u, w_grp, scale):
    b, s, _ = u.shape
    uf = u.astype(jnp.float32).reshape(b, s, POOL_GROUPS, POOL_GROUP_WIDTH)
    csum = lax.cumsum(uf, axis=1)
    t = jnp.arange(s)
    outs = []
    for g, w in enumerate(POOL_WINDOWS):
        cg = csum[:, :, g]
        shifted = jnp.pad(cg, ((0, 0), (w, 0), (0, 0)))[:, :s]
        cnt = jnp.minimum(t + 1, w).astype(jnp.float32)[None, :, None]
        outs.append((cg - shifted) / cnt - uf[:, :, g])
    pooled = jnp.stack(outs, axis=2)
    mixed = jnp.einsum('bsgc,gcd->bsgd', pooled, w_grp.astype(jnp.float32))
    mixed = mixed * scale.astype(jnp.float32).reshape(POOL_GROUPS, POOL_GROUP_WIDTH)
    return mixed.reshape(b, s, POOL_WIDTH).astype(u.dtype)


def rel_bucket(dist):
    n = jnp.maximum(dist, 0)
    max_exact = REL_BUCKETS // 2
    nf = jnp.maximum(n, 1).astype(jnp.float32)
    large = max_exact + (jnp.log(nf / max_exact) / math.log(REL_MAX_DIST / max_exact)
                         * (REL_BUCKETS - max_exact)).astype(jnp.int32)
    large = jnp.minimum(large, REL_BUCKETS - 1)
    return jnp.where(n < max_exact, n, large)


def dsa_attention(q, k, v, qi, ki, wi, rel_table):
    b, s = q.shape[0], q.shape[1]
    topk = min(TOPK_MAX, s // 4)
    nb = s // Q_BLOCK
    key_pos = jnp.arange(s)
    ki_f = ki.astype(jnp.float32)
    table = rel_table.astype(jnp.float32)
    gather = jax.vmap(lambda a, i: a[i])

    def to_blocks(a):
        return a.reshape((b, nb, Q_BLOCK) + a.shape[2:]).swapaxes(0, 1)

    def block_fn(args):
        qb, qib, wib, start = args
        t = start + jnp.arange(Q_BLOCK)
        sc = jnp.einsum('bqhd,bsd->bqhs', qib.astype(jnp.float32), ki_f)
        score = jnp.einsum('bqhs,bqh->bqs', jax.nn.relu(sc), wib.astype(jnp.float32)) * IDX_SCALE
        causal = key_pos[None, :] <= t[:, None]
        score = jnp.where(causal[None], score, -jnp.inf)
        _, idx = lax.top_k(score, topk)
        valid = idx <= t[None, :, None]
        k_sel = gather(k, idx).astype(jnp.float32)
        v_sel = gather(v, idx).astype(jnp.float32)
        logits = jnp.einsum('bqhd,bqkhd->bqhk', qb.astype(jnp.float32), k_sel) * ATTN_SCALE
        bias = table[rel_bucket(t[None, :, None] - idx)]
        logits = logits + bias.transpose(0, 1, 3, 2)
        logits = jnp.where(valid[:, :, None, :], logits, NEG_BIG)
        p = jax.nn.softmax(logits, axis=-1)
        out = jnp.einsum('bqhk,bqkhd->bqhd', p, v_sel)
        return out.astype(q.dtype)

    starts = jnp.arange(nb, dtype=jnp.int32) * Q_BLOCK
    outs = lax.map(block_fn, (to_blocks(q), to_blocks(qi), to_blocks(wi), starts))
    return outs.swapaxes(0, 1).reshape(b, s, ATTN_WIDTH)


def moe(h, w_router, b_router, w1, b1, w2, b2):
    b, s, d = h.shape
    hf = h.reshape(b * s, d)
    logits = hf.astype(jnp.float32) @ w_router.astype(jnp.float32) + b_router.astype(jnp.float32)
    top_vals, top_idx = lax.top_k(logits, TOP_K_EXPERTS)
    gates = jax.nn.softmax(top_vals, axis=-1)
    combine = jnp.sum(jax.nn.one_hot(top_idx, N_EXPERTS, dtype=jnp.float32) * gates[..., None], axis=1)
    out = jnp.zeros((b * s, d), jnp.float32)
    for e in range(N_EXPERTS):
        gu = hf @ w1[e] + b1[e]
        g_ = jnp.minimum(gu[:, :D_FF].astype(jnp.float32), SWIGLU_LIMIT)
        u_ = jnp.clip(gu[:, D_FF:].astype(jnp.float32), -SWIGLU_LIMIT, SWIGLU_LIMIT)
        act = (g_ * jax.nn.sigmoid(SWIGLU_ALPHA * g_) * (u_ + 1.0)).astype(h.dtype)
        out = out + combine[:, e:e + 1] * (act @ w2[e] + b2[e]).astype(jnp.float32)
    return out.reshape(b, s, d).astype(h.dtype)


def setup_inputs(seed: int = 0) -> dict:
    key = jax.random.key(seed)
    ks = jax.random.split(key, 20)
    f32 = jnp.float32
    nrm = lambda k, shape, sc: jax.random.normal(k, shape, f32) * sc
    return {
        "x": nrm(ks[0], (BATCH, SEQ, D_MODEL), 1.0),
        "mix_norm": 1.0 + nrm(ks[1], (DEPTH, D_MODEL), 0.02),
        "w_in": nrm(ks[2], (DEPTH, D_MODEL, D_IN), D_MODEL ** -0.5),
        "pool_w": nrm(ks[3], (DEPTH, POOL_GROUPS, POOL_GROUP_WIDTH, POOL_GROUP_WIDTH), POOL_GROUP_WIDTH ** -0.5),
        "pool_scale": 1.0 + nrm(ks[4], (DEPTH, POOL_WIDTH), 0.1),
        "w_branch_pool": nrm(ks[5], (DEPTH, POOL_WIDTH, D_MODEL), POOL_WIDTH ** -0.5),
        "w_branch_attn": nrm(ks[6], (DEPTH, ATTN_WIDTH, D_MODEL), ATTN_WIDTH ** -0.5),
        "rel_bias": nrm(ks[7], (REL_BUCKETS, N_HEADS), 0.5),
        "w_out": nrm(ks[8], (DEPTH, D_MODEL, D_MODEL), D_MODEL ** -0.5),
        "ffn_norm": 1.0 + nrm(ks[9], (DEPTH, D_MODEL), 0.02),
        "w_router": nrm(ks[10], (DEPTH, D_MODEL, N_EXPERTS), D_MODEL ** -0.5),
        "b_router": nrm(ks[11], (DEPTH, N_EXPERTS), 0.01),
        "w1": nrm(ks[12], (DEPTH, N_EXPERTS, D_MODEL, 2 * D_FF), D_MODEL ** -0.5),
        "b1": nrm(ks[13], (DEPTH, N_EXPERTS, 2 * D_FF), 0.01),
        "w2": nrm(ks[14], (DEPTH, N_EXPERTS, D_FF, D_MODEL), D_FF ** -0.5),
        "b2": nrm(ks[15], (DEPTH, N_EXPERTS, D_MODEL), 0.01),
        "final_norm": 1.0 + nrm(ks[16], (D_MODEL,), 0.02),
    }


def reference(x, mix_norm, w_in, pool_w, pool_scale, w_branch_pool, w_branch_attn, rel_bias,
              w_out, ffn_norm, w_router, b_router, w1, b1, w2, b2, final_norm):
    b, s, _ = x.shape
    offsets = [int(o) for o in np.cumsum(SPLIT_SIZES)[:-1]]
    for l in range(DEPTH):
        h = rmsnorm(x, mix_norm[l])
        z = h @ w_in[l]
        z_pool, z_q, z_k, z_v, z_qi, z_ki, z_wi, z_gate = jnp.split(z, offsets, axis=-1)
        y_pool = pool_mixer(z_pool, pool_w[l], pool_scale[l]) @ w_branch_pool[l]
        q = z_q.reshape(b, s, N_HEADS, HEAD_DIM)
        k = z_k.reshape(b, s, N_HEADS, HEAD_DIM)
        v = z_v.reshape(b, s, N_HEADS, HEAD_DIM)
        qi = z_qi.reshape(b, s, N_IDX_HEADS, IDX_DIM)
        y_attn = dsa_attention(q, k, v, qi, z_ki, z_wi, rel_bias) @ w_branch_attn[l]
        gates = jax.nn.sigmoid(z_gate.astype(jnp.float32)).reshape(b, s, N_BRANCHES, D_MODEL)
        merged = gates[:, :, 0] * y_pool.astype(jnp.float32) + gates[:, :, 1] * y_attn.astype(jnp.float32)
        x = x + merged.astype(x.dtype) @ w_out[l]
        h2 = rmsnorm(x, ffn_norm[l])
        x = x + moe(h2, w_router[l], b_router[l], w1[l], b1[l], w2[l], b2[l])
    return rmsnorm(x, final_norm)
```

```python
import functools
import math

import jax
import jax.numpy as jnp
import numpy as np
from jax import lax
from jax.experimental import pallas as pl
from jax.experimental.pallas import tpu as pltpu

D_MODEL = 1024
POOL_WIDTH = 512
POOL_WINDOWS = (2, 4, 8, 16)
POOL_GROUPS = len(POOL_WINDOWS)
POOL_GROUP_WIDTH = POOL_WIDTH // POOL_GROUPS
N_HEADS = 8
HEAD_DIM = 64
ATTN_WIDTH = N_HEADS * HEAD_DIM
N_IDX_HEADS = 4
IDX_DIM = 64
IDX_SCALE = (IDX_DIM ** -0.5) * (N_IDX_HEADS ** -0.5)
ATTN_SCALE = HEAD_DIM ** -0.5
TOPK_MAX = 256
REL_BUCKETS = 32
REL_MAX_DIST = 128
N_BRANCHES = 2
N_EXPERTS = 32
TOP_K_EXPERTS = 4
D_FF = D_MODEL
SWIGLU_LIMIT = 7.0
SWIGLU_ALPHA = 1.702
RMS_EPS = 1e-5
SPLIT_SIZES = (POOL_WIDTH, ATTN_WIDTH, ATTN_WIDTH, ATTN_WIDTH,
               N_IDX_HEADS * IDX_DIM, IDX_DIM, N_IDX_HEADS, N_BRANCHES * D_MODEL)

LANES = 128
SUBLANES = 8
VMEM_LIMIT_BYTES = 56 * 1024 * 1024

TM_IN = 512
TQ = 256
TK = 256
TM_POST = 512
TM_ROW = 256
TM_EXP = 256
POOL_HALO = 16
N_BISECT = 20
PAIR = 2 * HEAD_DIM
IDX_PAD = LANES
ROUTER_PAD = LANES
DMA_UNROLL = 8

F32 = jnp.float32
BF16 = jnp.bfloat16
NEG_INF = float("-inf")
M_INIT = -1e30


def _dot(a, b):
    return jnp.dot(a, b, preferred_element_type=F32)


def _dot_nt(a, b):
    return lax.dot_general(a, b, (((1,), (1,)), ((), ())), preferred_element_type=F32)


def _rmsnorm(x, g):
    ms = jnp.mean(x * x, axis=-1, keepdims=True)
    return x * lax.rsqrt(ms + RMS_EPS) * g


def _const_spec(shape):
    nd = len(shape)
    return pl.BlockSpec(shape, lambda *_: (0,) * nd)


def _params(n_axes):
    return pltpu.CompilerParams(
        dimension_semantics=("arbitrary",) * n_axes,
        vmem_limit_bytes=VMEM_LIMIT_BYTES)


def _inproj_body(x_ref, g_ref, wpool_ref, wq_ref, wkT_ref, wv_ref, wqi_ref, wkiT_ref,
                 wwi_ref, wg0_ref, wg1_ref, poolw_ref, pscale_ref, wbp_ref,
                 q_ref, kT_ref, v_ref, qi_ref, kiT_ref, wi_ref, pp_ref, g1_ref,
                 halo_ref, *, tiles_per_seq):
    i = pl.program_id(0)
    tm = x_ref.shape[0]
    h = _rmsnorm(x_ref[...], g_ref[...]).astype(BF16)

    q_ref[...] = (_dot(h, wq_ref[...]) * ATTN_SCALE).astype(BF16)
    kT = _dot_nt(wkT_ref[...], h).astype(BF16)
    kiT = _dot_nt(wkiT_ref[...], h).astype(BF16)
    for j in range(tm // TK):
        kT_ref[j] = kT[:, j * TK:(j + 1) * TK]
        kiT_ref[j] = kiT[:, j * TK:(j + 1) * TK]
    v_ref[...] = _dot(h, wv_ref[...]).astype(BF16)
    qi_ref[...] = _dot(h, wqi_ref[...]).astype(BF16)
    wi_ref[...] = _dot(h, wwi_ref[...]) * IDX_SCALE
    g1_ref[...] = jax.nn.sigmoid(_dot(h, wg1_ref[...])).astype(BF16)

    zp = _dot(h, wpool_ref[...])
    seq_tile = lax.rem(i, tiles_per_seq)

    @pl.when(seq_tile == 0)
    def _():
        halo_ref[...] = jnp.zeros_like(halo_ref)

    zext = jnp.concatenate([halo_ref[...], zp], axis=0)
    halo_ref[...] = zp[tm - POOL_HALO:, :]
    gw = POOL_GROUP_WIDTH
    s2 = zext + pltpu.roll(zext, 1, 0)
    s4 = s2[:, gw:] + pltpu.roll(s2[:, gw:], 2, 0)
    s8 = s4[:, gw:] + pltpu.roll(s4[:, gw:], 4, 0)
    s16 = s8[:, gw:] + pltpu.roll(s8[:, gw:], 8, 0)
    wsum = (s2[POOL_HALO:, :gw], s4[POOL_HALO:, :gw], s8[POOL_HALO:, :gw], s16[POOL_HALO:, :])
    t = seq_tile * tm + lax.broadcasted_iota(jnp.int32, (tm, 1), 0)
    mixed = []
    for g, w in enumerate(POOL_WINDOWS):
        cnt = jnp.minimum(t + 1, w).astype(F32)
        pooled = wsum[g] / cnt - zp[:, g * gw:(g + 1) * gw]
        mixed.append(_dot(pooled.astype(BF16), poolw_ref[g]) * pscale_ref[:, g * gw:(g + 1) * gw])
    mixed = jnp.concatenate(mixed, axis=1).astype(BF16)
    y_pool = _dot(mixed, wbp_ref[...])
    gate0 = jax.nn.sigmoid(_dot(h, wg0_ref[...]))
    pp_ref[...] = (gate0 * y_pool).astype(BF16)


def _inproj(xf, mix_norm, w_in, pool_w, pool_scale, w_branch_pool, seq):
    n, d = xf.shape
    tm = min(TM_IN, seq)
    assert seq % tm == 0 and tm % TK == 0 and n % tm == 0
    offs = [int(o) for o in np.cumsum(SPLIT_SIZES)]
    w = w_in.astype(BF16)
    w_pool = w[:, :offs[0]]
    w_q = w[:, offs[0]:offs[1]]
    w_kT = w[:, offs[1]:offs[2]].T
    w_v = w[:, offs[2]:offs[3]]
    w_qi = w[:, offs[3]:offs[4]].reshape(d, N_IDX_HEADS, IDX_DIM)
    w_qi = jnp.pad(w_qi, ((0, 0), (0, 0), (0, IDX_PAD - IDX_DIM))).reshape(d, N_IDX_HEADS * IDX_PAD)
    w_kiT = jnp.pad(w[:, offs[4]:offs[5]].T, ((0, IDX_PAD - IDX_DIM), (0, 0)))
    w_wi = jnp.pad(w[:, offs[5]:offs[6]], ((0, 0), (0, LANES - N_IDX_HEADS)))
    w_g0 = w[:, offs[6]:offs[6] + D_MODEL]
    w_g1 = w[:, offs[6] + D_MODEL:]
    consts = [mix_norm.reshape(1, d).astype(F32), w_pool, w_q, w_kT, w_v, w_qi, w_kiT, w_wi,
              w_g0, w_g1, pool_w.astype(BF16), pool_scale.reshape(1, POOL_WIDTH).astype(F32),
              w_branch_pool.astype(BF16)]
    grid = (n // tm,)
    row = lambda width: pl.BlockSpec((tm, width), lambda i: (i, 0))
    out_shape = [
        jax.ShapeDtypeStruct((n, ATTN_WIDTH), BF16),
        jax.ShapeDtypeStruct((n // TK, ATTN_WIDTH, TK), BF16),
        jax.ShapeDtypeStruct((n, ATTN_WIDTH), BF16),
        jax.ShapeDtypeStruct((n, N_IDX_HEADS * IDX_PAD), BF16),
        jax.ShapeDtypeStruct((n // TK, IDX_PAD, TK), BF16),
        jax.ShapeDtypeStruct((n, LANES), F32),
        jax.ShapeDtypeStruct((n, D_MODEL), BF16),
        jax.ShapeDtypeStruct((n, D_MODEL), BF16),
    ]
    out_specs = [
        row(ATTN_WIDTH),
        pl.BlockSpec((tm // TK, ATTN_WIDTH, TK), lambda i: (i, 0, 0)),
        row(ATTN_WIDTH),
        row(N_IDX_HEADS * IDX_PAD),
        pl.BlockSpec((tm // TK, IDX_PAD, TK), lambda i: (i, 0, 0)),
        row(LANES),
        row(D_MODEL),
        row(D_MODEL),
    ]
    return pl.pallas_call(
        functools.partial(_inproj_body, tiles_per_seq=seq // tm),
        grid=grid,
        in_specs=[row(d)] + [_const_spec(c.shape) for c in consts],
        out_specs=out_specs,
        out_shape=out_shape,
        scratch_shapes=[pltpu.VMEM((POOL_HALO, POOL_WIDTH), F32)],
        compiler_params=_params(1),
        name="inproj",
    )(xf, *consts)


def _rel_thresholds():
    n = np.arange(0, 4 * REL_MAX_DIST)
    max_exact = REL_BUCKETS // 2
    nf = np.maximum(n, 1).astype(np.float32)
    large = max_exact + (np.log(nf / np.float32(max_exact))
                         / np.float32(math.log(REL_MAX_DIST / max_exact))
                         * np.float32(REL_BUCKETS - max_exact)).astype(np.int32)
    bucket = np.where(n < max_exact, n, np.minimum(large, REL_BUCKETS - 1))
    assert np.all(np.diff(bucket) >= 0) and np.all(np.diff(bucket) <= 1)
    assert bucket[-1] == REL_BUCKETS - 1
    return [int(np.argmax(bucket >= b)) for b in range(1, REL_BUCKETS)]


def _attn_body(table_ref, q_ref, qi_ref, wi_ref, kT_ref, kiT_ref, v_ref, o_ref,
               score_ref, band_ref, qm_ref, m_ref, l_ref, acc_ref, thr_ref, cut_ref,
               *, topk):
    b = pl.program_id(0)
    qi = pl.program_id(1)
    tq = q_ref.shape[0]
    nk = qi + 1
    n_cols = score_ref.shape[0] * TK
    lane_k = lax.broadcasted_iota(jnp.int32, (tq, TK), 1)
    row_q = lax.broadcasted_iota(jnp.int32, (tq, TK), 0)

    @pl.when((b == 0) & (qi == 0))
    def _():
        thresholds = _rel_thresholds()
        for part in range(3):
            dist = row_q - lane_k + (2 - part) * TK
            for h in range(N_HEADS):
                bias = jnp.full((tq, TK), table_ref[h], F32)
                for bkt, thr in enumerate(thresholds, start=1):
                    bias = jnp.where(dist >= thr, table_ref[bkt * N_HEADS + h], bias)
                band_ref[h, part] = jnp.where(dist < 0, NEG_INF, bias)

    lane_p = lax.broadcasted_iota(jnp.int32, (tq, PAIR), 1)
    first_half = lane_p < HEAD_DIM
    for hp in range(N_HEADS // 2):
        qp = q_ref[:, hp * PAIR:(hp + 1) * PAIR]
        zero = jnp.zeros_like(qp)
        qm_ref[2 * hp] = jnp.where(first_half, qp, zero)
        qm_ref[2 * hp + 1] = jnp.where(first_half, zero, qp)

    wi = wi_ref[...]

    def score_tile(kj, carry):
        mx, mn = carry
        ki_t = kiT_ref[kj]
        sc = jnp.zeros((tq, TK), F32)
        for h in range(N_IDX_HEADS):
            s_h = _dot(qi_ref[:, h * IDX_PAD:(h + 1) * IDX_PAD], ki_t)
            sc = sc + jnp.maximum(s_h, 0.0) * wi[:, h:h + 1]
        causal = (kj * TK + lane_k) <= (qi * tq + row_q)
        score_ref[kj] = jnp.where(causal, sc, NEG_INF)
        mx = jnp.maximum(mx, jnp.max(jnp.where(causal, sc, NEG_INF), axis=1, keepdims=True))
        mn = jnp.minimum(mn, jnp.min(jnp.where(causal, sc, -NEG_INF), axis=1, keepdims=True))
        return mx, mn

    row_max, row_min = lax.fori_loop(
        0, nk, score_tile,
        (jnp.full((tq, 1), NEG_INF, F32), jnp.full((tq, 1), -NEG_INF, F32)))

    def count_where(pred):
        def body(kj, acc):
            m = jnp.where(pred(score_ref[kj], kj), 1.0, 0.0)
            return acc + m[:, :LANES] + m[:, LANES:]
        acc = lax.fori_loop(0, nk, body, jnp.zeros((tq, LANES), F32))
        return jnp.sum(acc, axis=1, keepdims=True)

    def max_where(pred):
        def body(kj, acc):
            s = score_ref[kj]
            m = jnp.where(pred(s, kj), s, NEG_INF)
            return jnp.maximum(acc, jnp.maximum(m[:, :LANES], m[:, LANES:]))
        acc = lax.fori_loop(0, nk, body, jnp.full((tq, LANES), NEG_INF, F32))
        return jnp.max(acc, axis=1, keepdims=True)

    kf = float(topk)
    thr_ref[...] = jnp.full((tq, 1), NEG_INF, F32)
    cut_ref[...] = jnp.full((tq, 1), n_cols, jnp.int32)

    @pl.when(qi * tq + 1 > topk)
    def _():
        def bisect(_, carry):
            lo, hi = carry
            mid = 0.5 * (lo + hi)
            above = count_where(lambda s, kj: s > mid) >= kf
            return jnp.where(above, mid, lo), jnp.where(above, hi, mid)

        _, hi = lax.fori_loop(0, N_BISECT, bisect, (row_min, row_max))
        cand = max_where(lambda s, kj: s <= hi)
        n_ge = count_where(lambda s, kj: s >= cand)

        def unresolved(state):
            it, _, n_ge = state
            return (jnp.min(n_ge) < kf) & (it < n_cols)

        def step(state):
            it, cand, n_ge = state
            nxt = max_where(lambda s, kj: s < cand)
            n_nxt = count_where(lambda s, kj: s >= nxt)
            open_ = n_ge < kf
            return it + 1, jnp.where(open_, nxt, cand), jnp.where(open_, n_nxt, n_ge)

        _, thr, n_ge = lax.while_loop(unresolved, step, (jnp.int32(0), cand, n_ge))
        thr_ref[...] = thr

        @pl.when(jnp.max(n_ge) > kf)
        def _():
            need = kf - count_where(lambda s, kj: s > thr)

            def bisect_idx(_, carry):
                lo_i, hi_i = carry
                mid_i = (lo_i + hi_i) // 2
                enough = count_where(
                    lambda s, kj: (s == thr) & ((kj * TK + lane_k) <= mid_i)) >= need
                return jnp.where(enough, lo_i, mid_i), jnp.where(enough, mid_i, hi_i)

            n_steps = max(1, int(math.ceil(math.log2(n_cols))) + 1)
            _, hi_i = lax.fori_loop(
                0, n_steps, bisect_idx,
                (jnp.full((tq, 1), -1, jnp.int32), jnp.full((tq, 1), n_cols - 1, jnp.int32)))
            cut_ref[...] = hi_i

    m_ref[...] = jnp.full(m_ref.shape, M_INIT, F32)
    l_ref[...] = jnp.zeros(l_ref.shape, F32)
    acc_ref[...] = jnp.zeros(acc_ref.shape, F32)
    thr = thr_ref[...]
    cut = cut_ref[...]

    def attend(kj, carry):
        part = jnp.clip(kj - qi + 2, 0, 2)
        sc = score_ref[kj]
        col = kj * TK + lane_k
        sel_bias = jnp.where(
            sc > thr, 0.0,
            jnp.where(sc == thr, jnp.where(col <= cut, 0.0, NEG_INF), NEG_INF))
        kT_t = kT_ref[kj]
        v_t = v_ref[pl.ds(pl.multiple_of(kj * TK, TK), TK), :]
        for hp in range(N_HEADS // 2):
            kp = kT_t[hp * PAIR:(hp + 1) * PAIR, :]
            vp = v_t[:, hp * PAIR:(hp + 1) * PAIR]
            outs, alphas = [], []
            for e in range(2):
                h = 2 * hp + e
                s = _dot(qm_ref[h], kp) + band_ref[h, part] + sel_bias
                m_old = m_ref[h]
                m_new = jnp.maximum(m_old, jnp.max(s, axis=1, keepdims=True))
                p = jnp.exp(s - m_new)
                alpha = jnp.exp(m_old - m_new)
                l_ref[h] = alpha * l_ref[h] + jnp.sum(p, axis=1, keepdims=True)
                m_ref[h] = m_new
                outs.append(_dot(p.astype(BF16), vp))
                alphas.append(alpha)
            acc_ref[hp] = (acc_ref[hp] * jnp.where(first_half, alphas[0], alphas[1])
                           + jnp.where(first_half, outs[0], outs[1]))
        return carry

    lax.fori_loop(0, nk, attend, 0)
    for hp in range(N_HEADS // 2):
        denom = jnp.where(first_half, l_ref[2 * hp], l_ref[2 * hp + 1])
        o_ref[:, hp * PAIR:(hp + 1) * PAIR] = (acc_ref[hp] / denom).astype(o_ref.dtype)


def _attention(q, kT, v, qi4, kiT, wi, rel_bias, batch, seq):
    n = q.shape[0]
    tq = min(TQ, seq)
    assert tq == TK and seq % tq == 0
    topk = min(TOPK_MAX, seq // 4)
    assert topk == tq or seq == tq
    nq = seq // tq
    nkt = seq // TK
    table = rel_bias.astype(F32).reshape(REL_BUCKETS * N_HEADS)
    qrow = lambda width: pl.BlockSpec((tq, width), lambda b, i: (b * nq + i, 0))
    return pl.pallas_call(
        functools.partial(_attn_body, topk=topk),
        grid=(batch, nq),
        in_specs=[
            pl.BlockSpec(memory_space=pltpu.SMEM),
            qrow(ATTN_WIDTH),
            qrow(N_IDX_HEADS * IDX_PAD),
            qrow(LANES),
            pl.BlockSpec((nkt, ATTN_WIDTH, TK), lambda b, i: (b, 0, 0)),
            pl.BlockSpec((nkt, IDX_PAD, TK), lambda b, i: (b, 0, 0)),
            pl.BlockSpec((seq, ATTN_WIDTH), lambda b, i: (b, 0)),
        ],
        out_specs=qrow(ATTN_WIDTH),
        out_shape=jax.ShapeDtypeStruct((n, ATTN_WIDTH), BF16),
        scratch_shapes=[
            pltpu.VMEM((nkt, tq, TK), F32),
            pltpu.VMEM((N_HEADS, 3, tq, TK), F32),
            pltpu.VMEM((N_HEADS, tq, PAIR), BF16),
            pltpu.VMEM((N_HEADS, tq, 1), F32),
            pltpu.VMEM((N_HEADS, tq, 1), F32),
            pltpu.VMEM((N_HEADS // 2, tq, PAIR), F32),
            pltpu.VMEM((tq, 1), F32),
            pltpu.VMEM((tq, 1), jnp.int32),
        ],
        compiler_params=_params(2),
        name="attention",
    )(table, q, qi4, wi, kT, kiT, v)


def _post_body(attn_ref, pp_ref, g1_ref, x_ref, wba_ref, wout_ref, fg_ref, wr_hi_ref, wr_lo_ref,
               br_ref, x1_ref, h2_ref, eidx_ref, gates_ref, counts_ref, carry_ref):
    i = pl.program_id(0)
    tm = x_ref.shape[0]
    y_attn = _dot(attn_ref[...], wba_ref[...])
    merged = pp_ref[...].astype(F32) + g1_ref[...].astype(F32) * y_attn
    x1 = x_ref[...] + _dot(merged.astype(BF16), wout_ref[...])
    x1_ref[...] = x1
    h2 = _rmsnorm(x1, fg_ref[...])
    h2_ref[...] = h2

    h_hi = h2.astype(BF16)
    h_lo = (h2 - h_hi.astype(F32)).astype(BF16)
    logits = (_dot(h_hi, wr_hi_ref[...]) + _dot(h_hi, wr_lo_ref[...])
              + _dot(h_lo, wr_hi_ref[...]) + br_ref[...])

    lane = lax.broadcasted_iota(jnp.int32, (tm, ROUTER_PAD), 1)
    work = logits
    vals, idxs = [], []
    for _ in range(TOP_K_EXPERTS):
        mx = jnp.max(work, axis=1, keepdims=True)
        ix = jnp.min(jnp.where(work == mx, lane, ROUTER_PAD), axis=1, keepdims=True)
        vals.append(mx)
        idxs.append(ix)
        work = jnp.where(lane == ix, NEG_INF, work)
    exps = [jnp.exp(v - vals[0]) for v in vals]
    denom = exps[0] + exps[1] + exps[2] + exps[3]

    member = jnp.zeros((tm, ROUTER_PAD), F32)
    for ix in idxs:
        member = member + jnp.where(lane == ix, 1.0, 0.0)

    @pl.when(i == 0)
    def _():
        carry_ref[...] = jnp.zeros_like(carry_ref)

    r_i = lax.broadcasted_iota(jnp.int32, (tm, tm), 0)
    c_i = lax.broadcasted_iota(jnp.int32, (tm, tm), 1)
    strict_lower = jnp.where(c_i < r_i, 1.0, 0.0).astype(BF16)
    before = _dot(strict_lower, member.astype(BF16)) + carry_ref[...]
    carry_new = carry_ref[...] + jnp.sum(member, axis=0, keepdims=True)
    carry_ref[...] = carry_new
    counts_ref[...] = jnp.broadcast_to(carry_new, counts_ref.shape)

    eidx = jnp.zeros((tm, ROUTER_PAD), jnp.int32)
    gates = jnp.zeros((tm, ROUTER_PAD), F32)
    for k in range(TOP_K_EXPERTS):
        rank = jnp.sum(jnp.where(lane == idxs[k], before, 0.0), axis=1, keepdims=True)
        eidx = jnp.where(lane == k, idxs[k], eidx)
        eidx = jnp.where(lane == TOP_K_EXPERTS + k, rank.astype(jnp.int32), eidx)
        gates = jnp.where(lane == k, exps[k] / denom, gates)
    eidx_ref[...] = eidx
    gates_ref[...] = gates


def _post_attn(attn, pp, g1, xf, w_branch_attn, w_out, ffn_norm, w_router, b_router):
    n, d = xf.shape
    tm = min(TM_POST, n)
    assert n % tm == 0
    wr = jnp.pad(w_router.astype(F32), ((0, 0), (0, ROUTER_PAD - N_EXPERTS)))
    wr_hi = wr.astype(BF16)
    wr_lo = (wr - wr_hi.astype(F32)).astype(BF16)
    br = jnp.pad(b_router.astype(F32), (0, ROUTER_PAD - N_EXPERTS),
                 constant_values=NEG_INF).reshape(1, ROUTER_PAD)
    consts = [w_branch_attn.astype(BF16), w_out.astype(BF16),
              ffn_norm.reshape(1, d).astype(F32), wr_hi, wr_lo, br]
    row = lambda width: pl.BlockSpec((tm, width), lambda i: (i, 0))
    return pl.pallas_call(
        _post_body,
        grid=(n // tm,),
        in_specs=[row(ATTN_WIDTH), row(d), row(d), row(d)] + [_const_spec(c.shape) for c in consts],
        out_specs=[row(d), row(d), row(ROUTER_PAD), row(ROUTER_PAD),
                   _const_spec((SUBLANES, ROUTER_PAD))],
        out_shape=[
            jax.ShapeDtypeStruct((n, d), F32),
            jax.ShapeDtypeStruct((n, d), F32),
            jax.ShapeDtypeStruct((n, ROUTER_PAD), jnp.int32),
            jax.ShapeDtypeStruct((n, ROUTER_PAD), F32),
            jax.ShapeDtypeStruct((SUBLANES, ROUTER_PAD), F32),
        ],
        scratch_shapes=[pltpu.VMEM((1, ROUTER_PAD), F32)],
        compiler_params=_params(1),
        name="post_attn",
    )(attn, pp, g1, xf, *consts)


def _dispatch_body(pad_start_ref, pad_count_ref, last_tile_ref, dest_ref, h2_ref, xs_ref,
                   zero_ref, sem, zsem, *, first_tail_tile):
    i = pl.program_id(0)
    tm = h2_ref.shape[0]
    n_tiles = xs_ref.shape[0] // TM_EXP

    def row_copy(r, k):
        return pltpu.make_async_copy(
            h2_ref.at[pl.ds(r, 1)], xs_ref.at[pl.ds(dest_ref[0, 0, r * TOP_K_EXPERTS + k], 1)], sem)

    def issue(c, carry):
        for u in range(DMA_UNROLL):
            for k in range(TOP_K_EXPERTS):
                row_copy(c * DMA_UNROLL + u, k).start()
        return carry

    lax.fori_loop(0, tm // DMA_UNROLL, issue, 0)

    @pl.when(i == 0)
    def _():
        zero_ref[...] = jnp.zeros_like(zero_ref)

        def pad_copy(e, r):
            return pltpu.make_async_copy(
                zero_ref.at[pl.ds(0, 1)], xs_ref.at[pl.ds(pad_start_ref[e] + r, 1)], zsem)

        def per_expert(e, carry):
            lax.fori_loop(0, pad_count_ref[e], lambda r, c: (pad_copy(e, r).start(), c)[1], 0)
            lax.fori_loop(0, pad_count_ref[e], lambda r, c: (pad_copy(e, r).wait(), c)[1], 0)
            return carry

        lax.fori_loop(0, N_EXPERTS, per_expert, 0)

        def tail_tile(t, carry):
            @pl.when(t > last_tile_ref[0])
            def _():
                cp = pltpu.make_async_copy(
                    zero_ref, xs_ref.at[pl.ds(pl.multiple_of(t * TM_EXP, TM_EXP), TM_EXP)], zsem)
                cp.start()
                cp.wait()
            return carry

        lax.fori_loop(first_tail_tile, n_tiles, tail_tile, 0)

    def drain(c, carry):
        for u in range(DMA_UNROLL):
            for k in range(TOP_K_EXPERTS):
                row_copy(c * DMA_UNROLL + u, k).wait()
        return carry

    lax.fori_loop(0, tm // DMA_UNROLL, drain, 0)


def _dispatch(h2, dest, pad_start, pad_count, last_tile, n_rows):
    n, d = h2.shape
    tm = min(TM_ROW, n)
    assert n % tm == 0 and tm % DMA_UNROLL == 0
    dest_blocks = dest.reshape(n // tm, 1, tm * TOP_K_EXPERTS)
    return pl.pallas_call(
        functools.partial(_dispatch_body, first_tail_tile=(n * TOP_K_EXPERTS) // TM_EXP),
        grid_spec=pltpu.PrefetchScalarGridSpec(
            num_scalar_prefetch=3,
            grid=(n // tm,),
            in_specs=[
                pl.BlockSpec((1, 1, tm * TOP_K_EXPERTS), lambda i, *_: (i, 0, 0),
                             memory_space=pltpu.SMEM),
                pl.BlockSpec((tm, d), lambda i, *_: (i, 0)),
            ],
            out_specs=pl.BlockSpec(memory_space=pl.ANY),
            scratch_shapes=[pltpu.VMEM((TM_EXP, d), F32), pltpu.SemaphoreType.DMA,
                            pltpu.SemaphoreType.DMA],
        ),
        out_shape=jax.ShapeDtypeStruct((n_rows, d), F32),
        compiler_params=_params(1),
        name="dispatch",
    )(pad_start, pad_count, last_tile, dest_blocks, h2)


def _experts_body(tile_expert_ref, tile_rows_ref, xs_ref, w1_ref, b1_ref, w2_ref,
                  b2_ref, y_ref, w1b_ref, w2b_ref):
    i = pl.program_id(0)
    expert = tile_expert_ref[i]
    prev = tile_expert_ref[jnp.maximum(i - 1, 0)]

    @pl.when((i == 0) | (expert != prev))
    def _():
        w1b_ref[...] = w1_ref[0].astype(BF16)
        w2b_ref[...] = w2_ref[0].astype(BF16)

    @pl.when(tile_rows_ref[i] > 0)
    def _():
        x = xs_ref[...].astype(BF16)
        gu = _dot(x, w1b_ref[...]) + b1_ref[0]
        g = jnp.minimum(gu[:, :D_FF], SWIGLU_LIMIT)
        u = jnp.clip(gu[:, D_FF:], -SWIGLU_LIMIT, SWIGLU_LIMIT)
        act = g * jax.nn.sigmoid(SWIGLU_ALPHA * g) * (u + 1.0)
        y_ref[...] = _dot(act.astype(BF16), w2b_ref[...]) + b2_ref[0]

    @pl.when(tile_rows_ref[i] == 0)
    def _():
        y_ref[...] = jnp.zeros_like(y_ref)


def _experts(xs, tile_expert, tile_rows, w1, b1, w2, b2):
    n_rows, d = xs.shape
    n_tiles = n_rows // TM_EXP
    tile = lambda i, te, tr: (i, 0)
    per_expert = lambda i, te, tr: (te[i], 0, 0)
    return pl.pallas_call(
        _experts_body,
        grid_spec=pltpu.PrefetchScalarGridSpec(
            num_scalar_prefetch=2,
            grid=(n_tiles,),
            in_specs=[
                pl.BlockSpec((TM_EXP, d), tile),
                pl.BlockSpec((1, d, 2 * D_FF), per_expert),
                pl.BlockSpec((1, 1, 2 * D_FF), per_expert),
                pl.BlockSpec((1, D_FF, d), per_expert),
                pl.BlockSpec((1, 1, d), per_expert),
            ],
            out_specs=pl.BlockSpec((TM_EXP, d), tile),
            scratch_shapes=[pltpu.VMEM((d, 2 * D_FF), BF16), pltpu.VMEM((D_FF, d), BF16)],
        ),
        out_shape=jax.ShapeDtypeStruct((n_rows, d), F32),
        compiler_params=_params(1),
        name="experts",
    )(tile_expert, tile_rows, xs, w1, b1.reshape(N_EXPERTS, 1, 2 * D_FF), w2,
      b2.reshape(N_EXPERTS, 1, d))


def _combine_body(dest_ref, gates_ref, x1_ref, fn_ref, y_ref, o_ref, buf_ref, sem):
    tm = x1_ref.shape[0]

    def row_copy(r, k):
        return pltpu.make_async_copy(
            y_ref.at[pl.ds(dest_ref[0, 0, r * TOP_K_EXPERTS + k], 1)],
            buf_ref.at[k, pl.ds(r, 1)], sem)

    def issue(c, carry):
        for u in range(DMA_UNROLL):
            for k in range(TOP_K_EXPERTS):
                row_copy(c * DMA_UNROLL + u, k).start()
        return carry

    def drain(c, carry):
        for u in range(DMA_UNROLL):
            for k in range(TOP_K_EXPERTS):
                row_copy(c * DMA_UNROLL + u, k).wait()
        return carry

    lax.fori_loop(0, tm // DMA_UNROLL, issue, 0)
    lax.fori_loop(0, tm // DMA_UNROLL, drain, 0)
    gates = gates_ref[...]
    out = x1_ref[...]
    for k in range(TOP_K_EXPERTS):
        out = out + gates[:, k:k + 1] * buf_ref[k]
    o_ref[...] = _rmsnorm(out, fn_ref[...])


def _combine(y, dest, gates, x1, final_norm):
    n, d = x1.shape
    tm = min(TM_ROW, n)
    assert n % tm == 0 and tm % DMA_UNROLL == 0
    dest_blocks = dest.reshape(n // tm, 1, tm * TOP_K_EXPERTS)
    row = lambda width: pl.BlockSpec((tm, width), lambda i: (i, 0))
    return pl.pallas_call(
        _combine_body,
        grid=(n // tm,),
        in_specs=[
            pl.BlockSpec((1, 1, tm * TOP_K_EXPERTS), lambda i: (i, 0, 0), memory_space=pltpu.SMEM),
            row(ROUTER_PAD), row(d), _const_spec((1, d)),
            pl.BlockSpec(memory_space=pl.ANY),
        ],
        out_specs=row(d),
        out_shape=jax.ShapeDtypeStruct((n, d), F32),
        scratch_shapes=[pltpu.VMEM((TOP_K_EXPERTS, tm, d), F32), pltpu.SemaphoreType.DMA],
        compiler_params=_params(1),
        name="combine",
    )(dest_blocks, gates, x1, final_norm.reshape(1, d).astype(F32), y)


def _routing_plan(eidx, counts, n_tiles):
    counts = counts[0, :N_EXPERTS].astype(jnp.int32)
    padded = ((counts + TM_EXP - 1) // TM_EXP) * TM_EXP
    ends = jnp.cumsum(padded)
    starts = ends - padded
    experts = eidx[:, :TOP_K_EXPERTS]
    ranks = eidx[:, TOP_K_EXPERTS:2 * TOP_K_EXPERTS]
    dest = (starts[experts] + ranks).astype(jnp.int32)
    tile_row0 = jnp.arange(n_tiles, dtype=jnp.int32) * TM_EXP
    tile_expert = jnp.minimum(jnp.searchsorted(ends, tile_row0, side="right"),
                              N_EXPERTS - 1).astype(jnp.int32)
    tile_rows = jnp.clip(counts[tile_expert] - (tile_row0 - starts[tile_expert]), 0, TM_EXP)
    tile_rows = jnp.where(tile_row0 < ends[-1], tile_rows, 0).astype(jnp.int32)
    last_tile = jnp.maximum(ends[-1] // TM_EXP - 1, 0).astype(jnp.int32).reshape(1)
    pad_start = (starts + counts).astype(jnp.int32)
    pad_count = (padded - counts).astype(jnp.int32)
    return dest, tile_expert, tile_rows, last_tile, pad_start, pad_count


def kernel(x, mix_norm, w_in, pool_w, pool_scale, w_branch_pool, w_branch_attn, rel_bias, w_out,
           ffn_norm, w_router, b_router, w1, b1, w2, b2, final_norm):
    batch, seq, d = x.shape
    n = batch * seq
    depth = mix_norm.shape[0]
    assert depth == 1, "the combine kernel fuses the final norm, so only one layer is supported"
    n_tiles = (n * TOP_K_EXPERTS + N_EXPERTS * (TM_EXP - 1) + TM_EXP - 1) // TM_EXP
    xf = x.reshape(n, d)
    for l in range(depth):
        q, kT, v, qi4, kiT, wi, pp, g1 = _inproj(
            xf, mix_norm[l], w_in[l], pool_w[l], pool_scale[l], w_branch_pool[l], seq)
        attn = _attention(q, kT, v, qi4, kiT, wi, rel_bias, batch, seq)
        x1, h2, eidx, gates, counts = _post_attn(
            attn, pp, g1, xf, w_branch_attn[l], w_out[l], ffn_norm[l], w_router[l], b_router[l])
        dest, tile_expert, tile_rows, last_tile, pad_start, pad_count = _routing_plan(
            eidx, counts, n_tiles)
        xs = _dispatch(h2, dest, pad_start, pad_count, last_tile, n_tiles * TM_EXP)
        y = _experts(xs, tile_expert, tile_rows, w1[l], b1[l], w2[l], b2[l])
        xf = _combine(y, dest, gates, x1, final_norm)
    return xf.reshape(batch, seq, d)
```

```python
import functools
import math

import jax
import jax.numpy as jnp
import numpy as np
from jax import lax
from jax.experimental import pallas as pl
from jax.experimental.pallas import tpu as pltpu

D_MODEL = 1024
POOL_WIDTH = 512
POOL_WINDOWS = (2, 4, 8, 16)
POOL_GROUPS = len(POOL_WINDOWS)
POOL_GROUP_WIDTH = POOL_WIDTH // POOL_GROUPS
N_HEADS = 8
HEAD_DIM = 64
ATTN_WIDTH = N_HEADS * HEAD_DIM
N_IDX_HEADS = 4
IDX_DIM = 64
IDX_SCALE = (IDX_DIM ** -0.5) * (N_IDX_HEADS ** -0.5)
ATTN_SCALE = HEAD_DIM ** -0.5
TOPK_MAX = 256
REL_BUCKETS = 32
REL_MAX_DIST = 128
N_BRANCHES = 2
N_EXPERTS = 32
TOP_K_EXPERTS = 4
D_FF = D_MODEL
SWIGLU_LIMIT = 7.0
SWIGLU_ALPHA = 1.702
RMS_EPS = 1e-5
SPLIT_SIZES = (POOL_WIDTH, ATTN_WIDTH, ATTN_WIDTH, ATTN_WIDTH,
               N_IDX_HEADS * IDX_DIM, IDX_DIM, N_IDX_HEADS, N_BRANCHES * D_MODEL)

LANES = 128
SUBLANES = 8
VMEM_LIMIT_BYTES = 56 * 1024 * 1024

TM_IN = 512
TQ = 256
TK = 256
TM_POST = 512
TM_ROW = 256
TM_EXP = 256
POOL_HALO = 16
N_BISECT = 20
PAIR = 2 * HEAD_DIM
IDX_PAD = LANES
ROUTER_PAD = LANES
DMA_UNROLL = 8

F32 = jnp.float32
BF16 = jnp.bfloat16
NEG_INF = float("-inf")
M_INIT = -1e30
LOG2E = math.log2(math.e)


def _dot(a, b):
    return jnp.dot(a, b, preferred_element_type=F32)


def _dot_nt(a, b):
    return lax.dot_general(a, b, (((1,), (1,)), ((), ())), preferred_element_type=F32)


def _rmsnorm(x, g):
    ms = jnp.mean(x * x, axis=-1, keepdims=True)
    return x * lax.rsqrt(ms + RMS_EPS) * g


def _const_spec(shape):
    nd = len(shape)
    return pl.BlockSpec(shape, lambda *_: (0,) * nd)


def _params(n_axes):
    return pltpu.CompilerParams(
        dimension_semantics=("arbitrary",) * n_axes,
        vmem_limit_bytes=VMEM_LIMIT_BYTES)


_ROW_SECTIONS = (("pool", POOL_WIDTH), ("k", ATTN_WIDTH), ("ki", IDX_PAD),
                 ("g0", D_MODEL), ("g1", D_MODEL))
_COL_SECTIONS = (("q", ATTN_WIDTH), ("v", ATTN_WIDTH), ("qi", N_IDX_HEADS * IDX_PAD),
                 ("wi", 2 * SUBLANES))


def _section(sections, name):
    start = 0
    for key, width in sections:
        if key == name:
            return slice(start, start + width)
        start += width
    raise KeyError(name)


def _inproj_body(x_ref, g_ref, wrow_ref, wcol_ref, poolw_ref, pscale_ref, wbp_ref,
                 qT_ref, k_ref, vT_ref, qiT_ref, ki_ref, wiT_ref, pp_ref, g1_ref,
                 halo_ref, *, tiles_per_seq):
    i = pl.program_id(0)
    tm = x_ref.shape[0]
    h = _rmsnorm(x_ref[...], g_ref[...]).astype(BF16)
    row_w = lambda name: wrow_ref[:, _section(_ROW_SECTIONS, name)]
    col_w = lambda name: wcol_ref[_section(_COL_SECTIONS, name), :]

    qT = (_dot_nt(col_w("q"), h) * (ATTN_SCALE * LOG2E)).astype(BF16)
    vT = _dot_nt(col_w("v"), h).astype(BF16)
    qiT = _dot_nt(col_w("qi"), h).astype(BF16)
    wiT = _dot_nt(col_w("wi"), h) * IDX_SCALE
    for j in range(tm // TQ):
        qT_ref[j] = qT[:, j * TQ:(j + 1) * TQ]
        qiT_ref[j] = qiT[:, j * TQ:(j + 1) * TQ]
        wiT_ref[j] = wiT[:SUBLANES, j * TQ:(j + 1) * TQ]
    for j in range(tm // TK):
        vT_ref[j] = vT[:, j * TK:(j + 1) * TK]
    k_ref[...] = _dot(h, row_w("k")).astype(BF16)
    ki_ref[...] = _dot(h, row_w("ki")).astype(BF16)
    g1_ref[...] = jax.nn.sigmoid(_dot(h, row_w("g1"))).astype(BF16)

    zp = _dot(h, row_w("pool"))
    seq_tile = lax.rem(i, tiles_per_seq)

    @pl.when(seq_tile == 0)
    def _():
        halo_ref[...] = jnp.zeros_like(halo_ref)

    zext = jnp.concatenate([halo_ref[...], zp], axis=0)
    halo_ref[...] = zp[tm - POOL_HALO:, :]
    gw = POOL_GROUP_WIDTH
    s2 = zext + pltpu.roll(zext, 1, 0)
    s4 = s2[:, gw:] + pltpu.roll(s2[:, gw:], 2, 0)
    s8 = s4[:, gw:] + pltpu.roll(s4[:, gw:], 4, 0)
    s16 = s8[:, gw:] + pltpu.roll(s8[:, gw:], 8, 0)
    wsum = (s2[POOL_HALO:, :gw], s4[POOL_HALO:, :gw], s8[POOL_HALO:, :gw], s16[POOL_HALO:, :])
    t = seq_tile * tm + lax.broadcasted_iota(jnp.int32, (tm, 1), 0)
    mixed = []
    for g, w in enumerate(POOL_WINDOWS):
        cnt = jnp.minimum(t + 1, w).astype(F32)
        pooled = wsum[g] / cnt - zp[:, g * gw:(g + 1) * gw]
        mixed.append(_dot(pooled.astype(BF16), poolw_ref[g]) * pscale_ref[:, g * gw:(g + 1) * gw])
    mixed = jnp.concatenate(mixed, axis=1).astype(BF16)
    y_pool = _dot(mixed, wbp_ref[...])
    gate0 = jax.nn.sigmoid(_dot(h, row_w("g0")))
    pp_ref[...] = (gate0 * y_pool).astype(BF16)


def _inproj(xf, mix_norm, w_in, pool_w, pool_scale, w_branch_pool, seq):
    n, d = xf.shape
    tm = min(TM_IN, seq)
    assert seq % tm == 0 and tm % TK == 0 and tm % TQ == 0 and n % tm == 0
    offs = [0] + [int(o) for o in np.cumsum(SPLIT_SIZES)]
    z_pool, z_q, z_k, z_v, z_qi, z_ki, z_wi, z_gate = (
        w_in[:, offs[j]:offs[j + 1]] for j in range(len(SPLIT_SIZES)))
    pad_cols = lambda a, width: jnp.pad(a, ((0, 0), (0, width - a.shape[1])))
    qi_heads = jnp.pad(z_qi.reshape(d, N_IDX_HEADS, IDX_DIM),
                       ((0, 0), (0, 0), (0, IDX_PAD - IDX_DIM))).reshape(d, N_IDX_HEADS * IDX_PAD)
    parts = {"pool": z_pool, "k": z_k, "ki": pad_cols(z_ki, IDX_PAD),
             "g0": z_gate[:, :D_MODEL], "g1": z_gate[:, D_MODEL:],
             "q": z_q, "v": z_v, "qi": qi_heads, "wi": pad_cols(z_wi, 2 * SUBLANES)}
    w_row = jnp.concatenate([parts[k] for k, _ in _ROW_SECTIONS], axis=1).astype(BF16)
    w_col = jnp.concatenate([parts[k] for k, _ in _COL_SECTIONS], axis=1).astype(BF16).T
    consts = [mix_norm.reshape(1, d).astype(F32), w_row, w_col, pool_w.astype(BF16),
              pool_scale.reshape(1, POOL_WIDTH).astype(F32), w_branch_pool.astype(BF16)]
    grid = (n // tm,)
    row = lambda width: pl.BlockSpec((tm, width), lambda i: (i, 0))
    tiles = lambda t, height: pl.BlockSpec((tm // t, height, t), lambda i: (i, 0, 0))
    out_shape = [
        jax.ShapeDtypeStruct((n // TQ, ATTN_WIDTH, TQ), BF16),
        jax.ShapeDtypeStruct((n, ATTN_WIDTH), BF16),
        jax.ShapeDtypeStruct((n // TK, ATTN_WIDTH, TK), BF16),
        jax.ShapeDtypeStruct((n // TQ, N_IDX_HEADS * IDX_PAD, TQ), BF16),
        jax.ShapeDtypeStruct((n, IDX_PAD), BF16),
        jax.ShapeDtypeStruct((n // TQ, SUBLANES, TQ), F32),
        jax.ShapeDtypeStruct((n, D_MODEL), BF16),
        jax.ShapeDtypeStruct((n, D_MODEL), BF16),
    ]
    out_specs = [
        tiles(TQ, ATTN_WIDTH),
        row(ATTN_WIDTH),
        tiles(TK, ATTN_WIDTH),
        tiles(TQ, N_IDX_HEADS * IDX_PAD),
        row(IDX_PAD),
        tiles(TQ, SUBLANES),
        row(D_MODEL),
        row(D_MODEL),
    ]
    return pl.pallas_call(
        functools.partial(_inproj_body, tiles_per_seq=seq // tm),
        grid=grid,
        in_specs=[row(d)] + [_const_spec(c.shape) for c in consts],
        out_specs=out_specs,
        out_shape=out_shape,
        scratch_shapes=[pltpu.VMEM((POOL_HALO, POOL_WIDTH), F32)],
        compiler_params=_params(1),
        name="inproj",
    )(xf, *consts)


def _rel_thresholds():
    n = np.arange(0, 4 * REL_MAX_DIST)
    max_exact = REL_BUCKETS // 2
    nf = np.maximum(n, 1).astype(np.float32)
    large = max_exact + (np.log(nf / np.float32(max_exact))
                         / np.float32(math.log(REL_MAX_DIST / max_exact))
                         * np.float32(REL_BUCKETS - max_exact)).astype(np.int32)
    bucket = np.where(n < max_exact, n, np.minimum(large, REL_BUCKETS - 1))
    assert np.all(np.diff(bucket) >= 0) and np.all(np.diff(bucket) <= 1)
    assert bucket[-1] == REL_BUCKETS - 1
    return [int(np.argmax(bucket >= b)) for b in range(1, REL_BUCKETS)]


def _attn_body(table_ref, qT_ref, qiT_ref, wiT_ref, k_ref, ki_ref, vT_ref, o_ref,
               score_ref, band_ref, qm_ref, m_ref, l_ref, acc_ref, thr_ref, cut_ref,
               *, topk):
    b = pl.program_id(0)
    qi = pl.program_id(1)
    tq = o_ref.shape[0]
    nk = qi + 1
    n_keys = score_ref.shape[0] * TK
    key_i = lax.broadcasted_iota(jnp.int32, (TK, tq), 0)
    qry_i = lax.broadcasted_iota(jnp.int32, (TK, tq), 1)

    @pl.when((b == 0) & (qi == 0))
    def _():
        thresholds = _rel_thresholds()
        for part in range(3):
            dist = qry_i - key_i + (2 - part) * TK
            for h in range(N_HEADS):
                bias = jnp.full((TK, tq), table_ref[h], F32)
                for bkt, thr in enumerate(thresholds, start=1):
                    bias = jnp.where(dist >= thr, table_ref[bkt * N_HEADS + h], bias)
                band_ref[h, part] = jnp.where(dist < 0, NEG_INF, bias * LOG2E)

    first_half = lax.broadcasted_iota(jnp.int32, (PAIR, tq), 0) < HEAD_DIM
    for hp in range(N_HEADS // 2):
        qp = qT_ref[0, hp * PAIR:(hp + 1) * PAIR, :]
        zero = jnp.zeros_like(qp)
        qm_ref[2 * hp] = jnp.where(first_half, qp, zero)
        qm_ref[2 * hp + 1] = jnp.where(first_half, zero, qp)

    wiT = wiT_ref[0]

    def score_tile(kj, carry):
        mx, mn = carry
        ki_t = ki_ref[pl.ds(pl.multiple_of(kj * TK, TK), TK), :]
        sc = jnp.zeros((TK, tq), F32)
        for h in range(N_IDX_HEADS):
            s_h = _dot(ki_t, qiT_ref[0, h * IDX_PAD:(h + 1) * IDX_PAD, :])
            sc = sc + jnp.maximum(s_h, 0.0) * wiT[h:h + 1, :]
        causal = (kj * TK + key_i) <= (qi * tq + qry_i)
        score_ref[kj] = jnp.where(causal, sc, NEG_INF)
        mx = jnp.maximum(mx, jnp.max(jnp.where(causal, sc, NEG_INF), axis=0, keepdims=True))
        mn = jnp.minimum(mn, jnp.min(jnp.where(causal, sc, -NEG_INF), axis=0, keepdims=True))
        return mx, mn

    row_max, row_min = lax.fori_loop(
        0, nk, score_tile,
        (jnp.full((1, tq), NEG_INF, F32), jnp.full((1, tq), -NEG_INF, F32)))

    def fold(x, op):
        return op(x.reshape(TK // SUBLANES, SUBLANES, tq), axis=0)

    def count_where(pred):
        def body(kj, acc):
            return acc + fold(jnp.where(pred(score_ref[kj], kj), 1.0, 0.0), jnp.sum)
        acc = lax.fori_loop(0, nk, body, jnp.zeros((SUBLANES, tq), F32))
        return jnp.sum(acc, axis=0, keepdims=True)

    def max_where(pred):
        def body(kj, acc):
            s = score_ref[kj]
            return jnp.maximum(acc, fold(jnp.where(pred(s, kj), s, NEG_INF), jnp.max))
        acc = lax.fori_loop(0, nk, body, jnp.full((SUBLANES, tq), NEG_INF, F32))
        return jnp.max(acc, axis=0, keepdims=True)

    kf = float(topk)
    thr_ref[...] = jnp.full((1, tq), NEG_INF, F32)
    cut_ref[...] = jnp.full((1, tq), n_keys, jnp.int32)

    @pl.when(qi * tq + 1 > topk)
    def _():
        def bisect(_, carry):
            lo, hi = carry
            mid = 0.5 * (lo + hi)
            above = count_where(lambda s, kj: s > mid) >= kf
            return jnp.where(above, mid, lo), jnp.where(above, hi, mid)

        _, hi = lax.fori_loop(0, N_BISECT, bisect, (row_min, row_max))
        cand = max_where(lambda s, kj: s <= hi)
        n_ge = count_where(lambda s, kj: s >= cand)

        def unresolved(state):
            it, _, n_ge = state
            return (jnp.min(n_ge) < kf) & (it < n_keys)

        def step(state):
            it, cand, n_ge = state
            nxt = max_where(lambda s, kj: s < cand)
            n_nxt = count_where(lambda s, kj: s >= nxt)
            open_ = n_ge < kf
            return it + 1, jnp.where(open_, nxt, cand), jnp.where(open_, n_nxt, n_ge)

        _, thr, n_ge = lax.while_loop(unresolved, step, (jnp.int32(0), cand, n_ge))
        thr_ref[...] = thr

        @pl.when(jnp.max(n_ge) > kf)
        def _():
            need = kf - count_where(lambda s, kj: s > thr)

            def bisect_idx(_, carry):
                lo_i, hi_i = carry
                mid_i = (lo_i + hi_i) // 2
                enough = count_where(
                    lambda s, kj: (s == thr) & ((kj * TK + key_i) <= mid_i)) >= need
                return jnp.where(enough, lo_i, mid_i), jnp.where(enough, mid_i, hi_i)

            n_steps = max(1, int(math.ceil(math.log2(n_keys))) + 1)
            _, hi_i = lax.fori_loop(
                0, n_steps, bisect_idx,
                (jnp.full((1, tq), -1, jnp.int32), jnp.full((1, tq), n_keys - 1, jnp.int32)))
            cut_ref[...] = hi_i

    m_ref[...] = jnp.full(m_ref.shape, M_INIT, F32)
    l_ref[...] = jnp.zeros(l_ref.shape, F32)
    acc_ref[...] = jnp.zeros(acc_ref.shape, F32)
    thr = thr_ref[...]
    cut = cut_ref[...]

    def attend(kj, carry):
        part = jnp.clip(kj - qi + 2, 0, 2)
        sc = score_ref[kj]
        key = kj * TK + key_i
        sel_bias = jnp.where(
            sc > thr, 0.0,
            jnp.where(sc == thr, jnp.where(key <= cut, 0.0, NEG_INF), NEG_INF))
        k_t = k_ref[pl.ds(pl.multiple_of(kj * TK, TK), TK), :]
        vT_t = vT_ref[kj]
        logits = [_dot(k_t[:, (h // 2) * PAIR:(h // 2 + 1) * PAIR], qm_ref[h])
                  for h in range(N_HEADS)]
        probs, alphas = [], []
        for h in range(N_HEADS):
            s = logits[h] + band_ref[h, part] + sel_bias
            m_old = m_ref[h]
            m_new = jnp.maximum(m_old, jnp.max(s, axis=0, keepdims=True))
            p = jnp.exp2(s - m_new)
            alpha = jnp.exp2(m_old - m_new)
            l_ref[h] = alpha * l_ref[h] + jnp.sum(p, axis=0, keepdims=True)
            m_ref[h] = m_new
            probs.append(p.astype(BF16))
            alphas.append(alpha)
        for hp in range(N_HEADS // 2):
            vTp = vT_t[hp * PAIR:(hp + 1) * PAIR, :]
            outs = [_dot(vTp, probs[2 * hp + e]) for e in range(2)]
            acc_ref[hp] = (acc_ref[hp] * jnp.where(first_half, alphas[2 * hp], alphas[2 * hp + 1])
                           + jnp.where(first_half, outs[0], outs[1]))
        return carry

    lax.fori_loop(0, nk, attend, 0)
    for hp in range(N_HEADS // 2):
        denom = jnp.where(first_half, l_ref[2 * hp], l_ref[2 * hp + 1])
        o_ref[:, hp * PAIR:(hp + 1) * PAIR] = (acc_ref[hp] / denom).T.astype(o_ref.dtype)


def _attention(qT, k, vT, qiT, ki, wiT, rel_bias, batch, seq):
    n = k.shape[0]
    tq = min(TQ, seq)
    assert tq == TQ == TK and seq % tq == 0
    topk = min(TOPK_MAX, seq // 4)
    assert topk == tq or seq == tq
    nq = seq // tq
    nkt = seq // TK
    table = rel_bias.astype(F32).reshape(REL_BUCKETS * N_HEADS)
    qtile = lambda height: pl.BlockSpec((1, height, tq), lambda b, i: (b * nq + i, 0, 0))
    return pl.pallas_call(
        functools.partial(_attn_body, topk=topk),
        grid=(batch, nq),
        in_specs=[
            pl.BlockSpec(memory_space=pltpu.SMEM),
            qtile(ATTN_WIDTH),
            qtile(N_IDX_HEADS * IDX_PAD),
            qtile(SUBLANES),
            pl.BlockSpec((seq, ATTN_WIDTH), lambda b, i: (b, 0)),
            pl.BlockSpec((seq, IDX_PAD), lambda b, i: (b, 0)),
            pl.BlockSpec((nkt, ATTN_WIDTH, TK), lambda b, i: (b, 0, 0)),
        ],
        out_specs=pl.BlockSpec((tq, ATTN_WIDTH), lambda b, i: (b * nq + i, 0)),
        out_shape=jax.ShapeDtypeStruct((n, ATTN_WIDTH), BF16),
        scratch_shapes=[
            pltpu.VMEM((nkt, TK, tq), F32),
            pltpu.VMEM((N_HEADS, 3, TK, tq), F32),
            pltpu.VMEM((N_HEADS, PAIR, tq), BF16),
            pltpu.VMEM((N_HEADS, 1, tq), F32),
            pltpu.VMEM((N_HEADS, 1, tq), F32),
            pltpu.VMEM((N_HEADS // 2, PAIR, tq), F32),
            pltpu.VMEM((1, tq), F32),
            pltpu.VMEM((1, tq), jnp.int32),
        ],
        compiler_params=_params(2),
        name="attention",
    )(table, qT, qiT, wiT, k, ki, vT)


def _post_body(attn_ref, pp_ref, g1_ref, x_ref, wba_ref, wout_ref, fg_ref, wr_hi_ref, wr_lo_ref,
               br_ref, x1_ref, h2_ref, eidx_ref, gates_ref, counts_ref, carry_ref):
    i = pl.program_id(0)
    tm = x_ref.shape[0]
    y_attn = _dot(attn_ref[...], wba_ref[...])
    merged = pp_ref[...].astype(F32) + g1_ref[...].astype(F32) * y_attn
    x1 = x_ref[...] + _dot(merged.astype(BF16), wout_ref[...])
    x1_ref[...] = x1
    h2 = _rmsnorm(x1, fg_ref[...])
    h2_ref[...] = h2

    h_hi = h2.astype(BF16)
    h_lo = (h2 - h_hi.astype(F32)).astype(BF16)
    logits = (_dot(h_hi, wr_hi_ref[...]) + _dot(h_hi, wr_lo_ref[...])
              + _dot(h_lo, wr_hi_ref[...]) + br_ref[...])

    lane = lax.broadcasted_iota(jnp.int32, (tm, ROUTER_PAD), 1)
    work = logits
    vals, idxs = [], []
    for _ in range(TOP_K_EXPERTS):
        mx = jnp.max(work, axis=1, keepdims=True)
        ix = jnp.min(jnp.where(work == mx, lane, ROUTER_PAD), axis=1, keepdims=True)
        vals.append(mx)
        idxs.append(ix)
        work = jnp.where(lane == ix, NEG_INF, work)
    exps = [jnp.exp(v - vals[0]) for v in vals]
    denom = exps[0] + exps[1] + exps[2] + exps[3]

    member = jnp.zeros((tm, ROUTER_PAD), F32)
    for ix in idxs:
        member = member + jnp.where(lane == ix, 1.0, 0.0)

    @pl.when(i == 0)
    def _():
        carry_ref[...] = jnp.zeros_like(carry_ref)

    r_i = lax.broadcasted_iota(jnp.int32, (tm, tm), 0)
    c_i = lax.broadcasted_iota(jnp.int32, (tm, tm), 1)
    strict_lower = jnp.where(c_i < r_i, 1.0, 0.0).astype(BF16)
    before = _dot(strict_lower, member.astype(BF16)) + carry_ref[...]
    carry_new = carry_ref[...] + jnp.sum(member, axis=0, keepdims=True)
    carry_ref[...] = carry_new
    counts_ref[...] = jnp.broadcast_to(carry_new, counts_ref.shape)

    eidx = jnp.zeros((tm, ROUTER_PAD), jnp.int32)
    gates = jnp.zeros((tm, ROUTER_PAD), F32)
    for k in range(TOP_K_EXPERTS):
        rank = jnp.sum(jnp.where(lane == idxs[k], before, 0.0), axis=1, keepdims=True)
        eidx = jnp.where(lane == k, idxs[k], eidx)
        eidx = jnp.where(lane == TOP_K_EXPERTS + k, rank.astype(jnp.int32), eidx)
        gates = jnp.where(lane == k, exps[k] / denom, gates)
    eidx_ref[...] = eidx
    gates_ref[...] = gates


def _post_attn(attn, pp, g1, xf, w_branch_attn, w_out, ffn_norm, w_router, b_router):
    n, d = xf.shape
    tm = min(TM_POST, n)
    assert n % tm == 0
    wr = jnp.pad(w_router.astype(F32), ((0, 0), (0, ROUTER_PAD - N_EXPERTS)))
    wr_hi = wr.astype(BF16)
    wr_lo = (wr - wr_hi.astype(F32)).astype(BF16)
    br = jnp.pad(b_router.astype(F32), (0, ROUTER_PAD - N_EXPERTS),
                 constant_values=NEG_INF).reshape(1, ROUTER_PAD)
    consts = [w_branch_attn.astype(BF16), w_out.astype(BF16),
              ffn_norm.reshape(1, d).astype(F32), wr_hi, wr_lo, br]
    row = lambda width: pl.BlockSpec((tm, width), lambda i: (i, 0))
    return pl.pallas_call(
        _post_body,
        grid=(n // tm,),
        in_specs=[row(ATTN_WIDTH), row(d), row(d), row(d)] + [_const_spec(c.shape) for c in consts],
        out_specs=[row(d), row(d), row(ROUTER_PAD), row(ROUTER_PAD),
                   _const_spec((SUBLANES, ROUTER_PAD))],
        out_shape=[
            jax.ShapeDtypeStruct((n, d), F32),
            jax.ShapeDtypeStruct((n, d), F32),
            jax.ShapeDtypeStruct((n, ROUTER_PAD), jnp.int32),
            jax.ShapeDtypeStruct((n, ROUTER_PAD), F32),
            jax.ShapeDtypeStruct((SUBLANES, ROUTER_PAD), F32),
        ],
        scratch_shapes=[pltpu.VMEM((1, ROUTER_PAD), F32)],
        compiler_params=_params(1),
        name="post_attn",
    )(attn, pp, g1, xf, *consts)


def _dispatch_body(pad_start_ref, pad_count_ref, last_tile_ref, dest_ref, h2_ref, xs_ref,
                   zero_ref, sem, zsem, *, first_tail_tile):
    i = pl.program_id(0)
    tm = h2_ref.shape[0]
    n_tiles = xs_ref.shape[0] // TM_EXP

    def row_copy(r, k):
        return pltpu.make_async_copy(
            h2_ref.at[pl.ds(r, 1)], xs_ref.at[pl.ds(dest_ref[0, 0, r * TOP_K_EXPERTS + k], 1)], sem)

    def issue(c, carry):
        for u in range(DMA_UNROLL):
            for k in range(TOP_K_EXPERTS):
                row_copy(c * DMA_UNROLL + u, k).start()
        return carry

    lax.fori_loop(0, tm // DMA_UNROLL, issue, 0)

    @pl.when(i == 0)
    def _():
        zero_ref[...] = jnp.zeros_like(zero_ref)

        def pad_copy(e, r):
            return pltpu.make_async_copy(
                zero_ref.at[pl.ds(0, 1)], xs_ref.at[pl.ds(pad_start_ref[e] + r, 1)], zsem)

        def per_expert(e, carry):
            lax.fori_loop(0, pad_count_ref[e], lambda r, c: (pad_copy(e, r).start(), c)[1], 0)
            lax.fori_loop(0, pad_count_ref[e], lambda r, c: (pad_copy(e, r).wait(), c)[1], 0)
            return carry

        lax.fori_loop(0, N_EXPERTS, per_expert, 0)

        def tail_tile(t, carry):
            @pl.when(t > last_tile_ref[0])
            def _():
                cp = pltpu.make_async_copy(
                    zero_ref, xs_ref.at[pl.ds(pl.multiple_of(t * TM_EXP, TM_EXP), TM_EXP)], zsem)
                cp.start()
                cp.wait()
            return carry

        lax.fori_loop(first_tail_tile, n_tiles, tail_tile, 0)

    def drain(c, carry):
        for u in range(DMA_UNROLL):
            for k in range(TOP_K_EXPERTS):
                row_copy(c * DMA_UNROLL + u, k).wait()
        return carry

    lax.fori_loop(0, tm // DMA_UNROLL, drain, 0)


def _dispatch(h2, dest, pad_start, pad_count, last_tile, n_rows):
    n, d = h2.shape
    tm = min(TM_ROW, n)
    assert n % tm == 0 and tm % DMA_UNROLL == 0
    dest_blocks = dest.reshape(n // tm, 1, tm * TOP_K_EXPERTS)
    return pl.pallas_call(
        functools.partial(_dispatch_body, first_tail_tile=(n * TOP_K_EXPERTS) // TM_EXP),
        grid_spec=pltpu.PrefetchScalarGridSpec(
            num_scalar_prefetch=3,
            grid=(n // tm,),
            in_specs=[
                pl.BlockSpec((1, 1, tm * TOP_K_EXPERTS), lambda i, *_: (i, 0, 0),
                             memory_space=pltpu.SMEM),
                pl.BlockSpec((tm, d), lambda i, *_: (i, 0)),
            ],
            out_specs=pl.BlockSpec(memory_space=pl.ANY),
            scratch_shapes=[pltpu.VMEM((TM_EXP, d), F32), pltpu.SemaphoreType.DMA,
                            pltpu.SemaphoreType.DMA],
        ),
        out_shape=jax.ShapeDtypeStruct((n_rows, d), F32),
        compiler_params=_params(1),
        name="dispatch",
    )(pad_start, pad_count, last_tile, dest_blocks, h2)


def _experts_body(tile_expert_ref, tile_rows_ref, xs_ref, w1_ref, b1_ref, w2_ref,
                  b2_ref, y_ref, w1b_ref, w2b_ref):
    i = pl.program_id(0)
    expert = tile_expert_ref[i]
    prev = tile_expert_ref[jnp.maximum(i - 1, 0)]

    @pl.when((i == 0) | (expert != prev))
    def _():
        w1b_ref[...] = w1_ref[0].astype(BF16)
        w2b_ref[...] = w2_ref[0].astype(BF16)

    @pl.when(tile_rows_ref[i] > 0)
    def _():
        x = xs_ref[...].astype(BF16)
        gu = _dot(x, w1b_ref[...]) + b1_ref[0]
        g = jnp.minimum(gu[:, :D_FF], SWIGLU_LIMIT)
        u = jnp.clip(gu[:, D_FF:], -SWIGLU_LIMIT, SWIGLU_LIMIT)
        act = g * jax.nn.sigmoid(SWIGLU_ALPHA * g) * (u + 1.0)
        y_ref[...] = _dot(act.astype(BF16), w2b_ref[...]) + b2_ref[0]

    @pl.when(tile_rows_ref[i] == 0)
    def _():
        y_ref[...] = jnp.zeros_like(y_ref)


def _experts(xs, tile_expert, tile_rows, w1, b1, w2, b2):
    n_rows, d = xs.shape
    n_tiles = n_rows // TM_EXP
    tile = lambda i, te, tr: (i, 0)
    per_expert = lambda i, te, tr: (te[i], 0, 0)
    return pl.pallas_call(
        _experts_body,
        grid_spec=pltpu.PrefetchScalarGridSpec(
            num_scalar_prefetch=2,
            grid=(n_tiles,),
            in_specs=[
                pl.BlockSpec((TM_EXP, d), tile),
                pl.BlockSpec((1, d, 2 * D_FF), per_expert),
                pl.BlockSpec((1, 1, 2 * D_FF), per_expert),
                pl.BlockSpec((1, D_FF, d), per_expert),
                pl.BlockSpec((1, 1, d), per_expert),
            ],
            out_specs=pl.BlockSpec((TM_EXP, d), tile),
            scratch_shapes=[pltpu.VMEM((d, 2 * D_FF), BF16), pltpu.VMEM((D_FF, d), BF16)],
        ),
        out_shape=jax.ShapeDtypeStruct((n_rows, d), F32),
        compiler_params=_params(1),
        name="experts",
    )(tile_expert, tile_rows, xs, w1, b1.reshape(N_EXPERTS, 1, 2 * D_FF), w2,
      b2.reshape(N_EXPERTS, 1, d))


def _combine_body(dest_ref, gates_ref, x1_ref, fn_ref, y_ref, o_ref, buf_ref, sem):
    tm = x1_ref.shape[0]

    def row_copy(r, k):
        return pltpu.make_async_copy(
            y_ref.at[pl.ds(dest_ref[0, 0, r * TOP_K_EXPERTS + k], 1)],
            buf_ref.at[k, pl.ds(r, 1)], sem)

    def issue(c, carry):
        for u in range(DMA_UNROLL):
            for k in range(TOP_K_EXPERTS):
                row_copy(c * DMA_UNROLL + u, k).start()
        return carry

    def drain(c, carry):
        for u in range(DMA_UNROLL):
            for k in range(TOP_K_EXPERTS):
                row_copy(c * DMA_UNROLL + u, k).wait()
        return carry

    lax.fori_loop(0, tm // DMA_UNROLL, issue, 0)
    lax.fori_loop(0, tm // DMA_UNROLL, drain, 0)
    gates = gates_ref[...]
    out = x1_ref[...]
    for k in range(TOP_K_EXPERTS):
        out = out + gates[:, k:k + 1] * buf_ref[k]
    o_ref[...] = _rmsnorm(out, fn_ref[...])


def _combine(y, dest, gates, x1, final_norm):
    n, d = x1.shape
    tm = min(TM_ROW, n)
    assert n % tm == 0 and tm % DMA_UNROLL == 0
    dest_blocks = dest.reshape(n // tm, 1, tm * TOP_K_EXPERTS)
    row = lambda width: pl.BlockSpec((tm, width), lambda i: (i, 0))
    return pl.pallas_call(
        _combine_body,
        grid=(n // tm,),
        in_specs=[
            pl.BlockSpec((1, 1, tm * TOP_K_EXPERTS), lambda i: (i, 0, 0), memory_space=pltpu.SMEM),
            row(ROUTER_PAD), row(d), _const_spec((1, d)),
            pl.BlockSpec(memory_space=pl.ANY),
        ],
        out_specs=row(d),
        out_shape=jax.ShapeDtypeStruct((n, d), F32),
        scratch_shapes=[pltpu.VMEM((TOP_K_EXPERTS, tm, d), F32), pltpu.SemaphoreType.DMA],
        compiler_params=_params(1),
        name="combine",
    )(dest_blocks, gates, x1, final_norm.reshape(1, d).astype(F32), y)


def _routing_plan(eidx, counts, n_tiles):
    counts = counts[0, :N_EXPERTS].astype(jnp.int32)
    padded = ((counts + TM_EXP - 1) // TM_EXP) * TM_EXP
    ends = jnp.cumsum(padded)
    starts = ends - padded
    experts = eidx[:, :TOP_K_EXPERTS]
    ranks = eidx[:, TOP_K_EXPERTS:2 * TOP_K_EXPERTS]
    dest = (starts[experts] + ranks).astype(jnp.int32)
    tile_row0 = jnp.arange(n_tiles, dtype=jnp.int32) * TM_EXP
    tile_expert = jnp.minimum(jnp.sum(tile_row0[:, None] >= ends[None, :], axis=1),
                              N_EXPERTS - 1).astype(jnp.int32)
    tile_rows = jnp.clip(counts[tile_expert] - (tile_row0 - starts[tile_expert]), 0, TM_EXP)
    tile_rows = jnp.where(tile_row0 < ends[-1], tile_rows, 0).astype(jnp.int32)
    last_tile = jnp.maximum(ends[-1] // TM_EXP - 1, 0).astype(jnp.int32).reshape(1)
    pad_start = (starts + counts).astype(jnp.int32)
    pad_count = (padded - counts).astype(jnp.int32)
    return dest, tile_expert, tile_rows, last_tile, pad_start, pad_count


def kernel(x, mix_norm, w_in, pool_w, pool_scale, w_branch_pool, w_branch_attn, rel_bias, w_out,
           ffn_norm, w_router, b_router, w1, b1, w2, b2, final_norm):
    batch, seq, d = x.shape
    n = batch * seq
    depth = mix_norm.shape[0]
    assert depth == 1, "the combine kernel fuses the final norm, so only one layer is supported"
    n_tiles = (n * TOP_K_EXPERTS + N_EXPERTS * (TM_EXP - 1) + TM_EXP - 1) // TM_EXP
    xf = x.reshape(n, d)
    for l in range(depth):
        qT, k, vT, qiT, ki, wiT, pp, g1 = _inproj(
            xf, mix_norm[l], w_in[l], pool_w[l], pool_scale[l], w_branch_pool[l], seq)
        attn = _attention(qT, k, vT, qiT, ki, wiT, rel_bias, batch, seq)
        x1, h2, eidx, gates, counts = _post_attn(
            attn, pp, g1, xf, w_branch_attn[l], w_out[l], ffn_norm[l], w_router[l], b_router[l])
        dest, tile_expert, tile_rows, last_tile, pad_start, pad_count = _routing_plan(
            eidx, counts, n_tiles)
        xs = _dispatch(h2, dest, pad_start, pad_count, last_tile, n_tiles * TM_EXP)
        y = _experts(xs, tile_expert, tile_rows, w1[l], b1[l], w2[l], b2[l])
        xf = _combine(y, dest, gates, x1, final_norm)
    return xf.reshape(batch, seq, d)
```

```python
import functools
import math

import jax
import jax.numpy as jnp
import numpy as np
from jax import lax
from jax.experimental import pallas as pl
from jax.experimental.pallas import tpu as pltpu

D_MODEL = 1024
POOL_WIDTH = 512
POOL_WINDOWS = (2, 4, 8, 16)
POOL_GROUPS = len(POOL_WINDOWS)
POOL_GROUP_WIDTH = POOL_WIDTH // POOL_GROUPS
N_HEADS = 8
HEAD_DIM = 64
ATTN_WIDTH = N_HEADS * HEAD_DIM
N_IDX_HEADS = 4
IDX_DIM = 64
IDX_SCALE = (IDX_DIM ** -0.5) * (N_IDX_HEADS ** -0.5)
ATTN_SCALE = HEAD_DIM ** -0.5
TOPK_MAX = 256
REL_BUCKETS = 32
REL_MAX_DIST = 128
N_BRANCHES = 2
N_EXPERTS = 32
TOP_K_EXPERTS = 4
D_FF = D_MODEL
SWIGLU_LIMIT = 7.0
SWIGLU_ALPHA = 1.702
RMS_EPS = 1e-5
SPLIT_SIZES = (POOL_WIDTH, ATTN_WIDTH, ATTN_WIDTH, ATTN_WIDTH,
               N_IDX_HEADS * IDX_DIM, IDX_DIM, N_IDX_HEADS, N_BRANCHES * D_MODEL)

LANES = 128
SUBLANES = 8
VMEM_LIMIT_BYTES = 56 * 1024 * 1024

TM_IN = 512
TQ = 256
TK = 256
TM_POST = 512
TM_ROW = 256
TM_EXP = 256
POOL_HALO = 16
N_BISECT_BF16 = 10
N_BISECT_F32 = 8
BF16_ROWS = 2 * SUBLANES
BF16_STEP = 2.0 ** -7
TINY = 1e-30
PAIR = 2 * HEAD_DIM
IDX_PAD = LANES
ROUTER_PAD = LANES
DMA_UNROLL = 8
N_DMA_PRIORITIES = 2

F32 = jnp.float32
BF16 = jnp.bfloat16
NEG_INF = float("-inf")
M_INIT = -1e30
LOG2E = math.log2(math.e)


def _dot(a, b):
    return jnp.dot(a, b, preferred_element_type=F32)


def _dot_nt(a, b):
    return lax.dot_general(a, b, (((1,), (1,)), ((), ())), preferred_element_type=F32)


def _rmsnorm(x, g):
    ms = jnp.mean(x * x, axis=-1, keepdims=True)
    return x * lax.rsqrt(ms + RMS_EPS) * g


def _const_spec(shape):
    nd = len(shape)
    return pl.BlockSpec(shape, lambda *_: (0,) * nd)


def _params(n_axes):
    return pltpu.CompilerParams(
        dimension_semantics=("arbitrary",) * n_axes,
        vmem_limit_bytes=VMEM_LIMIT_BYTES)


_ROW_SECTIONS = (("pool", POOL_WIDTH), ("k", ATTN_WIDTH), ("ki", IDX_PAD),
                 ("g0", D_MODEL), ("g1", D_MODEL))
_COL_SECTIONS = (("q", ATTN_WIDTH), ("v", ATTN_WIDTH), ("qi", N_IDX_HEADS * IDX_PAD),
                 ("wi", 2 * SUBLANES))


def _section(sections, name):
    start = 0
    for key, width in sections:
        if key == name:
            return slice(start, start + width)
        start += width
    raise KeyError(name)


def _inproj_body(x_ref, g_ref, wrow_ref, wcol_ref, poolw_ref, pscale_ref, wbp_ref,
                 qT_ref, k_ref, vT_ref, qiT_ref, ki_ref, wiT_ref, pp_ref, g1_ref,
                 halo_ref, *, tiles_per_seq):
    i = pl.program_id(0)
    tm = x_ref.shape[0]
    h = _rmsnorm(x_ref[...], g_ref[...]).astype(BF16)
    row_w = lambda name: wrow_ref[:, _section(_ROW_SECTIONS, name)]
    col_w = lambda name: wcol_ref[_section(_COL_SECTIONS, name), :]

    qT = (_dot_nt(col_w("q"), h) * (ATTN_SCALE * LOG2E)).astype(BF16)
    vT = _dot_nt(col_w("v"), h).astype(BF16)
    qiT = _dot_nt(col_w("qi"), h).astype(BF16)
    wiT = _dot_nt(col_w("wi"), h) * IDX_SCALE
    for j in range(tm // TQ):
        qT_ref[j] = qT[:, j * TQ:(j + 1) * TQ]
        qiT_ref[j] = qiT[:, j * TQ:(j + 1) * TQ]
        wiT_ref[j] = wiT[:SUBLANES, j * TQ:(j + 1) * TQ]
    for j in range(tm // TK):
        vT_ref[j] = vT[:, j * TK:(j + 1) * TK]
    k_ref[...] = _dot(h, row_w("k")).astype(BF16)
    ki_ref[...] = _dot(h, row_w("ki")).astype(BF16)
    g1_ref[...] = jax.nn.sigmoid(_dot(h, row_w("g1"))).astype(BF16)

    zp = _dot(h, row_w("pool"))
    seq_tile = lax.rem(i, tiles_per_seq)

    @pl.when(seq_tile == 0)
    def _():
        halo_ref[...] = jnp.zeros_like(halo_ref)

    zext = jnp.concatenate([halo_ref[...], zp], axis=0)
    halo_ref[...] = zp[tm - POOL_HALO:, :]
    gw = POOL_GROUP_WIDTH
    s2 = zext + pltpu.roll(zext, 1, 0)
    s4 = s2[:, gw:] + pltpu.roll(s2[:, gw:], 2, 0)
    s8 = s4[:, gw:] + pltpu.roll(s4[:, gw:], 4, 0)
    s16 = s8[:, gw:] + pltpu.roll(s8[:, gw:], 8, 0)
    wsum = (s2[POOL_HALO:, :gw], s4[POOL_HALO:, :gw], s8[POOL_HALO:, :gw], s16[POOL_HALO:, :])
    t = seq_tile * tm + lax.broadcasted_iota(jnp.int32, (tm, 1), 0)
    mixed = []
    for g, w in enumerate(POOL_WINDOWS):
        cnt = jnp.minimum(t + 1, w).astype(F32)
        pooled = wsum[g] / cnt - zp[:, g * gw:(g + 1) * gw]
        mixed.append(_dot(pooled.astype(BF16), poolw_ref[g]) * pscale_ref[:, g * gw:(g + 1) * gw])
    mixed = jnp.concatenate(mixed, axis=1).astype(BF16)
    y_pool = _dot(mixed, wbp_ref[...])
    gate0 = jax.nn.sigmoid(_dot(h, row_w("g0")))
    pp_ref[...] = (gate0 * y_pool).astype(BF16)


def _inproj(xf, mix_norm, w_in, pool_w, pool_scale, w_branch_pool, seq):
    n, d = xf.shape
    tm = min(TM_IN, seq)
    assert seq % tm == 0 and tm % TK == 0 and tm % TQ == 0 and n % tm == 0
    offs = [0] + [int(o) for o in np.cumsum(SPLIT_SIZES)]
    z_pool, z_q, z_k, z_v, z_qi, z_ki, z_wi, z_gate = (
        w_in[:, offs[j]:offs[j + 1]] for j in range(len(SPLIT_SIZES)))
    pad_cols = lambda a, width: jnp.pad(a, ((0, 0), (0, width - a.shape[1])))
    qi_heads = jnp.pad(z_qi.reshape(d, N_IDX_HEADS, IDX_DIM),
                       ((0, 0), (0, 0), (0, IDX_PAD - IDX_DIM))).reshape(d, N_IDX_HEADS * IDX_PAD)
    parts = {"pool": z_pool, "k": z_k, "ki": pad_cols(z_ki, IDX_PAD),
             "g0": z_gate[:, :D_MODEL], "g1": z_gate[:, D_MODEL:],
             "q": z_q, "v": z_v, "qi": qi_heads, "wi": pad_cols(z_wi, 2 * SUBLANES)}
    w_row = jnp.concatenate([parts[k] for k, _ in _ROW_SECTIONS], axis=1).astype(BF16)
    w_col = jnp.concatenate([parts[k] for k, _ in _COL_SECTIONS], axis=1).astype(BF16).T
    consts = [mix_norm.reshape(1, d).astype(F32), w_row, w_col, pool_w.astype(BF16),
              pool_scale.reshape(1, POOL_WIDTH).astype(F32), w_branch_pool.astype(BF16)]
    grid = (n // tm,)
    row = lambda width: pl.BlockSpec((tm, width), lambda i: (i, 0))
    tiles = lambda t, height: pl.BlockSpec((tm // t, height, t), lambda i: (i, 0, 0))
    out_shape = [
        jax.ShapeDtypeStruct((n // TQ, ATTN_WIDTH, TQ), BF16),
        jax.ShapeDtypeStruct((n, ATTN_WIDTH), BF16),
        jax.ShapeDtypeStruct((n // TK, ATTN_WIDTH, TK), BF16),
        jax.ShapeDtypeStruct((n // TQ, N_IDX_HEADS * IDX_PAD, TQ), BF16),
        jax.ShapeDtypeStruct((n, IDX_PAD), BF16),
        jax.ShapeDtypeStruct((n // TQ, SUBLANES, TQ), F32),
        jax.ShapeDtypeStruct((n, D_MODEL), BF16),
        jax.ShapeDtypeStruct((n, D_MODEL), BF16),
    ]
    out_specs = [
        tiles(TQ, ATTN_WIDTH),
        row(ATTN_WIDTH),
        tiles(TK, ATTN_WIDTH),
        tiles(TQ, N_IDX_HEADS * IDX_PAD),
        row(IDX_PAD),
        tiles(TQ, SUBLANES),
        row(D_MODEL),
        row(D_MODEL),
    ]
    return pl.pallas_call(
        functools.partial(_inproj_body, tiles_per_seq=seq // tm),
        grid=grid,
        in_specs=[row(d)] + [_const_spec(c.shape) for c in consts],
        out_specs=out_specs,
        out_shape=out_shape,
        scratch_shapes=[pltpu.VMEM((POOL_HALO, POOL_WIDTH), F32)],
        compiler_params=_params(1),
        name="inproj",
    )(xf, *consts)


def _rel_thresholds():
    n = np.arange(0, 4 * REL_MAX_DIST)
    max_exact = REL_BUCKETS // 2
    nf = np.maximum(n, 1).astype(np.float32)
    large = max_exact + (np.log(nf / np.float32(max_exact))
                         / np.float32(math.log(REL_MAX_DIST / max_exact))
                         * np.float32(REL_BUCKETS - max_exact)).astype(np.int32)
    bucket = np.where(n < max_exact, n, np.minimum(large, REL_BUCKETS - 1))
    assert np.all(np.diff(bucket) >= 0) and np.all(np.diff(bucket) <= 1)
    assert bucket[-1] == REL_BUCKETS - 1
    return [int(np.argmax(bucket >= b)) for b in range(1, REL_BUCKETS)]


def _attn_body(table_ref, qT_ref, qiT_ref, wiT_ref, k_ref, ki_ref, vT_ref, o_ref,
               score_ref, sb_ref, band_ref, qm_ref, m_ref, l_ref, acc_ref, thr_ref, cut_ref,
               *, topk):
    b = pl.program_id(0)
    qi = pl.program_id(1)
    tq = o_ref.shape[0]
    nk = qi + 1
    n_keys = score_ref.shape[0] * TK
    key_i = lax.broadcasted_iota(jnp.int32, (TK, tq), 0)
    qry_i = lax.broadcasted_iota(jnp.int32, (TK, tq), 1)

    @pl.when((b == 0) & (qi == 0))
    def _():
        thresholds = _rel_thresholds()
        for part in range(3):
            dist = qry_i - key_i + (2 - part) * TK
            for h in range(N_HEADS):
                bias = jnp.full((TK, tq), table_ref[h], F32)
                for bkt, thr in enumerate(thresholds, start=1):
                    bias = jnp.where(dist >= thr, table_ref[bkt * N_HEADS + h], bias)
                band_ref[h, part] = jnp.where(dist < 0, NEG_INF, bias * LOG2E)

    first_half = lax.broadcasted_iota(jnp.int32, (PAIR, tq), 0) < HEAD_DIM
    for hp in range(N_HEADS // 2):
        qp = qT_ref[0, hp * PAIR:(hp + 1) * PAIR, :]
        zero = jnp.zeros_like(qp)
        qm_ref[2 * hp] = jnp.where(first_half, qp, zero)
        qm_ref[2 * hp + 1] = jnp.where(first_half, zero, qp)

    wiT = wiT_ref[0]

    def score_tile(kj, carry):
        mx, mn = carry
        ki_t = ki_ref[pl.ds(pl.multiple_of(kj * TK, TK), TK), :]
        sc = jnp.zeros((TK, tq), F32)
        for h in range(N_IDX_HEADS):
            s_h = _dot(ki_t, qiT_ref[0, h * IDX_PAD:(h + 1) * IDX_PAD, :])
            sc = sc + jnp.maximum(s_h, 0.0) * wiT[h:h + 1, :]
        causal = (kj * TK + key_i) <= (qi * tq + qry_i)
        masked = jnp.where(causal, sc, NEG_INF)
        score_ref[kj] = masked
        sb_ref[kj] = masked.astype(BF16)
        mx = jnp.maximum(mx, jnp.max(jnp.where(causal, sc, NEG_INF), axis=0, keepdims=True))
        mn = jnp.minimum(mn, jnp.min(jnp.where(causal, sc, -NEG_INF), axis=0, keepdims=True))
        return mx, mn

    row_max, row_min = lax.fori_loop(
        0, nk, score_tile,
        (jnp.full((1, tq), NEG_INF, F32), jnp.full((1, tq), -NEG_INF, F32)))

    def fold(x, op):
        return op(x.reshape(TK // SUBLANES, SUBLANES, tq), axis=0)

    def count_where(pred):
        def body(kj, acc):
            return acc + fold(jnp.where(pred(score_ref[kj], kj), 1.0, 0.0), jnp.sum)
        acc = lax.fori_loop(0, nk, body, jnp.zeros((SUBLANES, tq), F32))
        return jnp.sum(acc, axis=0, keepdims=True)

    def max_where(pred):
        def body(kj, acc):
            s = score_ref[kj]
            return jnp.maximum(acc, fold(jnp.where(pred(s, kj), s, NEG_INF), jnp.max))
        acc = lax.fori_loop(0, nk, body, jnp.full((SUBLANES, tq), NEG_INF, F32))
        return jnp.max(acc, axis=0, keepdims=True)

    kf = float(topk)
    thr_ref[...] = jnp.full((1, tq), NEG_INF, F32)
    cut_ref[...] = jnp.full((1, tq), n_keys, jnp.int32)

    @pl.when(qi * tq + 1 > topk)
    def _():
        def count_above_bf16(mid_b):
            mid_t = jnp.broadcast_to(mid_b, (TK, tq))
            one, zero = jnp.ones((), BF16), jnp.zeros((), BF16)

            def body(kj, acc):
                m = jnp.where(sb_ref[kj] > mid_t, one, zero)
                parts = [m[r * BF16_ROWS:(r + 1) * BF16_ROWS] for r in range(TK // BF16_ROWS)]
                while len(parts) > 1:
                    parts = [a + b for a, b in zip(parts[::2], parts[1::2])]
                return acc + parts[0].astype(F32)
            acc = lax.fori_loop(0, nk, body, jnp.zeros((BF16_ROWS, tq), F32))
            return jnp.sum(acc, axis=0, keepdims=True)

        def widen(v, sign):
            return v + sign * (jnp.abs(v) * BF16_STEP + TINY)

        def bisect_bf16(_, carry):
            lo, hi = carry
            mid_b = (0.5 * (lo + hi)).astype(BF16)
            above = count_above_bf16(mid_b) >= kf
            mid = mid_b.astype(F32)
            return jnp.where(above, mid, lo), jnp.where(above, hi, mid)

        lo, hi = lax.fori_loop(0, N_BISECT_BF16, bisect_bf16,
                               (widen(row_min, -1.0), widen(row_max, 1.0)))

        def bisect(_, carry):
            lo, hi = carry
            mid = 0.5 * (lo + hi)
            above = count_where(lambda s, kj: s > mid) >= kf
            return jnp.where(above, mid, lo), jnp.where(above, hi, mid)

        _, hi = lax.fori_loop(0, N_BISECT_F32, bisect, (widen(lo, -1.0), widen(hi, 1.0)))
        cand = max_where(lambda s, kj: s <= hi)
        n_ge = count_where(lambda s, kj: s >= cand)

        def unresolved(state):
            it, _, n_ge = state
            return (jnp.min(n_ge) < kf) & (it < n_keys)

        def step(state):
            it, cand, n_ge = state
            nxt = max_where(lambda s, kj: s < cand)
            n_nxt = count_where(lambda s, kj: s >= nxt)
            open_ = n_ge < kf
            return it + 1, jnp.where(open_, nxt, cand), jnp.where(open_, n_nxt, n_ge)

        _, thr, n_ge = lax.while_loop(unresolved, step, (jnp.int32(0), cand, n_ge))
        thr_ref[...] = thr

        @pl.when(jnp.max(n_ge) > kf)
        def _():
            need = kf - count_where(lambda s, kj: s > thr)

            def bisect_idx(_, carry):
                lo_i, hi_i = carry
                mid_i = (lo_i + hi_i) // 2
                enough = count_where(
                    lambda s, kj: (s == thr) & ((kj * TK + key_i) <= mid_i)) >= need
                return jnp.where(enough, lo_i, mid_i), jnp.where(enough, mid_i, hi_i)

            n_steps = max(1, int(math.ceil(math.log2(n_keys))) + 1)
            _, hi_i = lax.fori_loop(
                0, n_steps, bisect_idx,
                (jnp.full((1, tq), -1, jnp.int32), jnp.full((1, tq), n_keys - 1, jnp.int32)))
            cut_ref[...] = hi_i

    m_ref[...] = jnp.full(m_ref.shape, M_INIT, F32)
    l_ref[...] = jnp.zeros(l_ref.shape, F32)
    acc_ref[...] = jnp.zeros(acc_ref.shape, F32)
    thr = thr_ref[...]
    cut = cut_ref[...]

    def attend(kj, carry):
        part = jnp.clip(kj - qi + 2, 0, 2)
        sc = score_ref[kj]
        key = kj * TK + key_i
        sel_bias = jnp.where(
            sc > thr, 0.0,
            jnp.where(sc == thr, jnp.where(key <= cut, 0.0, NEG_INF), NEG_INF))
        k_t = k_ref[pl.ds(pl.multiple_of(kj * TK, TK), TK), :]
        vT_t = vT_ref[kj]
        logits = [_dot(k_t[:, (h // 2) * PAIR:(h // 2 + 1) * PAIR], qm_ref[h])
                  for h in range(N_HEADS)]
        probs, alphas = [], []
        for h in range(N_HEADS):
            s = logits[h] + band_ref[h, part] + sel_bias
            m_old = m_ref[h]
            m_new = jnp.maximum(m_old, jnp.max(s, axis=0, keepdims=True))
            p = jnp.exp2(s - m_new)
            alpha = jnp.exp2(m_old - m_new)
            l_ref[h] = alpha * l_ref[h] + jnp.sum(p, axis=0, keepdims=True)
            m_ref[h] = m_new
            probs.append(p.astype(BF16))
            alphas.append(alpha)
        for hp in range(N_HEADS // 2):
            vTp = vT_t[hp * PAIR:(hp + 1) * PAIR, :]
            outs = [_dot(vTp, probs[2 * hp + e]) for e in range(2)]
            acc_ref[hp] = (acc_ref[hp] * jnp.where(first_half, alphas[2 * hp], alphas[2 * hp + 1])
                           + jnp.where(first_half, outs[0], outs[1]))
        return carry

    lax.fori_loop(0, nk, attend, 0)
    for hp in range(N_HEADS // 2):
        denom = jnp.where(first_half, l_ref[2 * hp], l_ref[2 * hp + 1])
        o_ref[:, hp * PAIR:(hp + 1) * PAIR] = (acc_ref[hp] / denom).T.astype(o_ref.dtype)


def _attention(qT, k, vT, qiT, ki, wiT, rel_bias, batch, seq):
    n = k.shape[0]
    tq = min(TQ, seq)
    assert tq == TQ == TK and seq % tq == 0
    topk = min(TOPK_MAX, seq // 4)
    assert topk == tq or seq == tq
    nq = seq // tq
    nkt = seq // TK
    table = rel_bias.astype(F32).reshape(REL_BUCKETS * N_HEADS)
    qtile = lambda height: pl.BlockSpec((1, height, tq), lambda b, i: (b * nq + i, 0, 0))
    return pl.pallas_call(
        functools.partial(_attn_body, topk=topk),
        grid=(batch, nq),
        in_specs=[
            pl.BlockSpec(memory_space=pltpu.SMEM),
            qtile(ATTN_WIDTH),
            qtile(N_IDX_HEADS * IDX_PAD),
            qtile(SUBLANES),
            pl.BlockSpec((seq, ATTN_WIDTH), lambda b, i: (b, 0)),
            pl.BlockSpec((seq, IDX_PAD), lambda b, i: (b, 0)),
            pl.BlockSpec((nkt, ATTN_WIDTH, TK), lambda b, i: (b, 0, 0)),
        ],
        out_specs=pl.BlockSpec((tq, ATTN_WIDTH), lambda b, i: (b * nq + i, 0)),
        out_shape=jax.ShapeDtypeStruct((n, ATTN_WIDTH), BF16),
        scratch_shapes=[
            pltpu.VMEM((nkt, TK, tq), F32),
            pltpu.VMEM((nkt, TK, tq), BF16),
            pltpu.VMEM((N_HEADS, 3, TK, tq), F32),
            pltpu.VMEM((N_HEADS, PAIR, tq), BF16),
            pltpu.VMEM((N_HEADS, 1, tq), F32),
            pltpu.VMEM((N_HEADS, 1, tq), F32),
            pltpu.VMEM((N_HEADS // 2, PAIR, tq), F32),
            pltpu.VMEM((1, tq), F32),
            pltpu.VMEM((1, tq), jnp.int32),
        ],
        compiler_params=_params(2),
        name="attention",
    )(table, qT, qiT, wiT, k, ki, vT)


def _post_body(attn_ref, pp_ref, g1_ref, x_ref, wba_ref, wout_ref, fg_ref, wr_hi_ref, wr_lo_ref,
               br_ref, x1_ref, h2_ref, eidx_ref, gates_ref, counts_ref, carry_ref):
    i = pl.program_id(0)
    tm = x_ref.shape[0]
    y_attn = _dot(attn_ref[...], wba_ref[...])
    merged = pp_ref[...].astype(F32) + g1_ref[...].astype(F32) * y_attn
    x1 = x_ref[...] + _dot(merged.astype(BF16), wout_ref[...])
    x1_ref[...] = x1
    h2 = _rmsnorm(x1, fg_ref[...])
    h2_ref[...] = h2

    h_hi = h2.astype(BF16)
    h_lo = (h2 - h_hi.astype(F32)).astype(BF16)
    logits = (_dot(h_hi, wr_hi_ref[...]) + _dot(h_hi, wr_lo_ref[...])
              + _dot(h_lo, wr_hi_ref[...]) + br_ref[...])

    lane = lax.broadcasted_iota(jnp.int32, (tm, ROUTER_PAD), 1)
    work = logits
    vals, idxs = [], []
    for _ in range(TOP_K_EXPERTS):
        mx = jnp.max(work, axis=1, keepdims=True)
        ix = jnp.min(jnp.where(work == mx, lane, ROUTER_PAD), axis=1, keepdims=True)
        vals.append(mx)
        idxs.append(ix)
        work = jnp.where(lane == ix, NEG_INF, work)
    exps = [jnp.exp(v - vals[0]) for v in vals]
    denom = exps[0] + exps[1] + exps[2] + exps[3]

    member = jnp.zeros((tm, ROUTER_PAD), F32)
    for ix in idxs:
        member = member + jnp.where(lane == ix, 1.0, 0.0)

    @pl.when(i == 0)
    def _():
        carry_ref[...] = jnp.zeros_like(carry_ref)

    r_i = lax.broadcasted_iota(jnp.int32, (tm, tm), 0)
    c_i = lax.broadcasted_iota(jnp.int32, (tm, tm), 1)
    strict_lower = jnp.where(c_i < r_i, 1.0, 0.0).astype(BF16)
    before = _dot(strict_lower, member.astype(BF16)) + carry_ref[...]
    carry_new = carry_ref[...] + jnp.sum(member, axis=0, keepdims=True)
    carry_ref[...] = carry_new
    counts_ref[...] = jnp.broadcast_to(carry_new, counts_ref.shape)

    eidx = jnp.zeros((tm, ROUTER_PAD), jnp.int32)
    gates = jnp.zeros((tm, ROUTER_PAD), F32)
    for k in range(TOP_K_EXPERTS):
        rank = jnp.sum(jnp.where(lane == idxs[k], before, 0.0), axis=1, keepdims=True)
        eidx = jnp.where(lane == k, idxs[k], eidx)
        eidx = jnp.where(lane == TOP_K_EXPERTS + k, rank.astype(jnp.int32), eidx)
        gates = jnp.where(lane == k, exps[k] / denom, gates)
    eidx_ref[...] = eidx
    gates_ref[...] = gates


def _post_attn(attn, pp, g1, xf, w_branch_attn, w_out, ffn_norm, w_router, b_router):
    n, d = xf.shape
    tm = min(TM_POST, n)
    assert n % tm == 0
    wr = jnp.pad(w_router.astype(F32), ((0, 0), (0, ROUTER_PAD - N_EXPERTS)))
    wr_hi = wr.astype(BF16)
    wr_lo = (wr - wr_hi.astype(F32)).astype(BF16)
    br = jnp.pad(b_router.astype(F32), (0, ROUTER_PAD - N_EXPERTS),
                 constant_values=NEG_INF).reshape(1, ROUTER_PAD)
    consts = [w_branch_attn.astype(BF16), w_out.astype(BF16),
              ffn_norm.reshape(1, d).astype(F32), wr_hi, wr_lo, br]
    row = lambda width: pl.BlockSpec((tm, width), lambda i: (i, 0))
    return pl.pallas_call(
        _post_body,
        grid=(n // tm,),
        in_specs=[row(ATTN_WIDTH), row(d), row(d), row(d)] + [_const_spec(c.shape) for c in consts],
        out_specs=[row(d), row(d), row(ROUTER_PAD), row(ROUTER_PAD),
                   _const_spec((SUBLANES, ROUTER_PAD))],
        out_shape=[
            jax.ShapeDtypeStruct((n, d), F32),
            jax.ShapeDtypeStruct((n, d), F32),
            jax.ShapeDtypeStruct((n, ROUTER_PAD), jnp.int32),
            jax.ShapeDtypeStruct((n, ROUTER_PAD), F32),
            jax.ShapeDtypeStruct((SUBLANES, ROUTER_PAD), F32),
        ],
        scratch_shapes=[pltpu.VMEM((1, ROUTER_PAD), F32)],
        compiler_params=_params(1),
        name="post_attn",
    )(attn, pp, g1, xf, *consts)


def _dispatch_body(pad_start_ref, pad_count_ref, last_tile_ref, dest_ref, h2_ref, xs_ref,
                   zero_ref, sem, zsem, *, first_tail_tile):
    i = pl.program_id(0)
    tm = h2_ref.shape[0]
    n_tiles = xs_ref.shape[0] // TM_EXP

    def row_copy(r, k):
        return pltpu.make_async_copy(
            h2_ref.at[pl.ds(r, 1)], xs_ref.at[pl.ds(dest_ref[0, 0, r * TOP_K_EXPERTS + k], 1)], sem)

    def issue(c, carry):
        for u in range(DMA_UNROLL):
            for k in range(TOP_K_EXPERTS):
                row_copy(c * DMA_UNROLL + u, k).start(priority=k % N_DMA_PRIORITIES)
        return carry

    lax.fori_loop(0, tm // DMA_UNROLL, issue, 0)

    @pl.when(i == 0)
    def _():
        zero_ref[...] = jnp.zeros_like(zero_ref)

        def pad_copy(e, r):
            return pltpu.make_async_copy(
                zero_ref.at[pl.ds(0, 1)], xs_ref.at[pl.ds(pad_start_ref[e] + r, 1)], zsem)

        def per_expert(e, carry):
            lax.fori_loop(0, pad_count_ref[e], lambda r, c: (pad_copy(e, r).start(), c)[1], 0)
            lax.fori_loop(0, pad_count_ref[e], lambda r, c: (pad_copy(e, r).wait(), c)[1], 0)
            return carry

        lax.fori_loop(0, N_EXPERTS, per_expert, 0)

        def tail_tile(t, carry):
            @pl.when(t > last_tile_ref[0])
            def _():
                cp = pltpu.make_async_copy(
                    zero_ref, xs_ref.at[pl.ds(pl.multiple_of(t * TM_EXP, TM_EXP), TM_EXP)], zsem)
                cp.start()
                cp.wait()
            return carry

        lax.fori_loop(first_tail_tile, n_tiles, tail_tile, 0)

    def drain(c, carry):
        for u in range(DMA_UNROLL):
            for k in range(TOP_K_EXPERTS):
                row_copy(c * DMA_UNROLL + u, k).wait()
        return carry

    lax.fori_loop(0, tm // DMA_UNROLL, drain, 0)


def _dispatch(h2, dest, pad_start, pad_count, last_tile, n_rows):
    n, d = h2.shape
    tm = min(TM_ROW, n)
    assert n % tm == 0 and tm % DMA_UNROLL == 0
    dest_blocks = dest.reshape(n // tm, 1, tm * TOP_K_EXPERTS)
    return pl.pallas_call(
        functools.partial(_dispatch_body, first_tail_tile=(n * TOP_K_EXPERTS) // TM_EXP),
        grid_spec=pltpu.PrefetchScalarGridSpec(
            num_scalar_prefetch=3,
            grid=(n // tm,),
            in_specs=[
                pl.BlockSpec((1, 1, tm * TOP_K_EXPERTS), lambda i, *_: (i, 0, 0),
                             memory_space=pltpu.SMEM),
                pl.BlockSpec((tm, d), lambda i, *_: (i, 0)),
            ],
            out_specs=pl.BlockSpec(memory_space=pl.ANY),
            scratch_shapes=[pltpu.VMEM((TM_EXP, d), F32), pltpu.SemaphoreType.DMA,
                            pltpu.SemaphoreType.DMA],
        ),
        out_shape=jax.ShapeDtypeStruct((n_rows, d), F32),
        compiler_params=_params(1),
        name="dispatch",
    )(pad_start, pad_count, last_tile, dest_blocks, h2)


def _experts_body(tile_expert_ref, tile_rows_ref, run_start_ref, next_expert_ref,
                  xs_ref, w1_hbm, b1_ref, w2_hbm, b2_ref, y_ref,
                  w1f_ref, w2f_ref, w1b_ref, w2b_ref, wsem):
    i = pl.program_id(0)

    def fetch(expert):
        return (pltpu.make_async_copy(w1_hbm.at[expert], w1f_ref, wsem.at[0]),
                pltpu.make_async_copy(w2_hbm.at[expert], w2f_ref, wsem.at[1]))

    @pl.when(i == 0)
    def _():
        for cp in fetch(tile_expert_ref[0]):
            cp.start()

    @pl.when(run_start_ref[i] == 1)
    def _():
        for cp in fetch(tile_expert_ref[i]):
            cp.wait()
        w1b_ref[...] = w1f_ref[...].astype(BF16)
        w2b_ref[...] = w2f_ref[...].astype(BF16)

        @pl.when(next_expert_ref[i] >= 0)
        def _():
            for cp in fetch(next_expert_ref[i]):
                cp.start()

    @pl.when(tile_rows_ref[i] > 0)
    def _():
        x = xs_ref[...].astype(BF16)
        gu = _dot(x, w1b_ref[...]) + b1_ref[0]
        g = jnp.minimum(gu[:, :D_FF], SWIGLU_LIMIT)
        u = jnp.clip(gu[:, D_FF:], -SWIGLU_LIMIT, SWIGLU_LIMIT)
        act = g * jax.nn.sigmoid(SWIGLU_ALPHA * g) * (u + 1.0)
        y_ref[...] = _dot(act.astype(BF16), w2b_ref[...]) + b2_ref[0]

    @pl.when(tile_rows_ref[i] == 0)
    def _():
        y_ref[...] = jnp.zeros_like(y_ref)


def _experts(xs, tile_expert, tile_rows, run_start, next_expert, w1, b1, w2, b2):
    n_rows, d = xs.shape
    n_tiles = n_rows // TM_EXP
    tile = lambda i, *_: (i, 0)
    per_expert = lambda i, te, *_: (te[i], 0, 0)
    return pl.pallas_call(
        _experts_body,
        grid_spec=pltpu.PrefetchScalarGridSpec(
            num_scalar_prefetch=4,
            grid=(n_tiles,),
            in_specs=[
                pl.BlockSpec((TM_EXP, d), tile),
                pl.BlockSpec(memory_space=pl.ANY),
                pl.BlockSpec((1, 1, 2 * D_FF), per_expert),
                pl.BlockSpec(memory_space=pl.ANY),
                pl.BlockSpec((1, 1, d), per_expert),
            ],
            out_specs=pl.BlockSpec((TM_EXP, d), tile),
            scratch_shapes=[
                pltpu.VMEM((d, 2 * D_FF), F32), pltpu.VMEM((D_FF, d), F32),
                pltpu.VMEM((d, 2 * D_FF), BF16), pltpu.VMEM((D_FF, d), BF16),
                pltpu.SemaphoreType.DMA((2,)),
            ],
        ),
        out_shape=jax.ShapeDtypeStruct((n_rows, d), F32),
        compiler_params=_params(1),
        name="experts",
    )(tile_expert, tile_rows, run_start, next_expert, xs, w1,
      b1.reshape(N_EXPERTS, 1, 2 * D_FF), w2, b2.reshape(N_EXPERTS, 1, d))


def _combine_body(dest_ref, gates_ref, x1_ref, fn_ref, y_ref, o_ref, buf_ref, sem):
    tm = x1_ref.shape[0]

    def row_copy(r, k):
        return pltpu.make_async_copy(
            y_ref.at[pl.ds(dest_ref[0, 0, r * TOP_K_EXPERTS + k], 1)],
            buf_ref.at[k, pl.ds(r, 1)], sem)

    def issue(c, carry):
        for u in range(DMA_UNROLL):
            for k in range(TOP_K_EXPERTS):
                row_copy(c * DMA_UNROLL + u, k).start(priority=k % N_DMA_PRIORITIES)
        return carry

    def drain(c, carry):
        for u in range(DMA_UNROLL):
            for k in range(TOP_K_EXPERTS):
                row_copy(c * DMA_UNROLL + u, k).wait()
        return carry

    lax.fori_loop(0, tm // DMA_UNROLL, issue, 0)
    lax.fori_loop(0, tm // DMA_UNROLL, drain, 0)
    gates = gates_ref[...]
    out = x1_ref[...]
    for k in range(TOP_K_EXPERTS):
        out = out + gates[:, k:k + 1] * buf_ref[k]
    o_ref[...] = _rmsnorm(out, fn_ref[...])


def _combine(y, dest, gates, x1, final_norm):
    n, d = x1.shape
    tm = min(TM_ROW, n)
    assert n % tm == 0 and tm % DMA_UNROLL == 0
    dest_blocks = dest.reshape(n // tm, 1, tm * TOP_K_EXPERTS)
    row = lambda width: pl.BlockSpec((tm, width), lambda i: (i, 0))
    return pl.pallas_call(
        _combine_body,
        grid=(n // tm,),
        in_specs=[
            pl.BlockSpec((1, 1, tm * TOP_K_EXPERTS), lambda i: (i, 0, 0), memory_space=pltpu.SMEM),
            row(ROUTER_PAD), row(d), _const_spec((1, d)),
            pl.BlockSpec(memory_space=pl.ANY),
        ],
        out_specs=row(d),
        out_shape=jax.ShapeDtypeStruct((n, d), F32),
        scratch_shapes=[pltpu.VMEM((TOP_K_EXPERTS, tm, d), F32), pltpu.SemaphoreType.DMA],
        compiler_params=_params(1),
        name="combine",
    )(dest_blocks, gates, x1, final_norm.reshape(1, d).astype(F32), y)


def _routing_plan(eidx, counts, n_tiles):
    counts = counts[0, :N_EXPERTS].astype(jnp.int32)
    padded = ((counts + TM_EXP - 1) // TM_EXP) * TM_EXP
    ends = jnp.cumsum(padded)
    starts = ends - padded
    experts = eidx[:, :TOP_K_EXPERTS]
    ranks = eidx[:, TOP_K_EXPERTS:2 * TOP_K_EXPERTS]
    dest = (starts[experts] + ranks).astype(jnp.int32)
    tile_row0 = jnp.arange(n_tiles, dtype=jnp.int32) * TM_EXP
    tile_expert = jnp.minimum(jnp.sum(tile_row0[:, None] >= ends[None, :], axis=1),
                              N_EXPERTS - 1).astype(jnp.int32)
    tile_rows = jnp.clip(counts[tile_expert] - (tile_row0 - starts[tile_expert]), 0, TM_EXP)
    used = tile_row0 < ends[-1]
    tile_rows = jnp.where(used, tile_rows, 0).astype(jnp.int32)
    last_tile = jnp.maximum(ends[-1] // TM_EXP - 1, 0).astype(jnp.int32).reshape(1)
    pad_start = (starts + counts).astype(jnp.int32)
    pad_count = (padded - counts).astype(jnp.int32)
    changed = jnp.concatenate([jnp.ones((1,), bool), tile_expert[1:] != tile_expert[:-1]])
    run_start = (used & changed).astype(jnp.int32)
    ids = jnp.where(counts > 0, jnp.arange(N_EXPERTS, dtype=jnp.int32), N_EXPERTS)
    later = jnp.concatenate([lax.cummin(ids, reverse=True)[1:],
                             jnp.full((1,), N_EXPERTS, jnp.int32)])
    next_expert = jnp.where(later < N_EXPERTS, later, -1)[tile_expert].astype(jnp.int32)
    return dest, tile_expert, tile_rows, run_start, next_expert, last_tile, pad_start, pad_count


def kernel(x, mix_norm, w_in, pool_w, pool_scale, w_branch_pool, w_branch_attn, rel_bias, w_out,
           ffn_norm, w_router, b_router, w1, b1, w2, b2, final_norm):
    batch, seq, d = x.shape
    n = batch * seq
    depth = mix_norm.shape[0]
    assert depth == 1, "the combine kernel fuses the final norm, so only one layer is supported"
    n_tiles = (n * TOP_K_EXPERTS + N_EXPERTS * (TM_EXP - 1) + TM_EXP - 1) // TM_EXP
    xf = x.reshape(n, d)
    for l in range(depth):
        qT, k, vT, qiT, ki, wiT, pp, g1 = _inproj(
            xf, mix_norm[l], w_in[l], pool_w[l], pool_scale[l], w_branch_pool[l], seq)
        attn = _attention(qT, k, vT, qiT, ki, wiT, rel_bias, batch, seq)
        x1, h2, eidx, gates, counts = _post_attn(
            attn, pp, g1, xf, w_branch_attn[l], w_out[l], ffn_norm[l], w_router[l], b_router[l])
        (dest, tile_expert, tile_rows, run_start, next_expert, last_tile, pad_start,
         pad_count) = _routing_plan(eidx, counts, n_tiles)
        xs = _dispatch(h2, dest, pad_start, pad_count, last_tile, n_tiles * TM_EXP)
        y = _experts(xs, tile_expert, tile_rows, run_start, next_expert,
                     w1[l], b1[l], w2[l], b2[l])
        xf = _combine(y, dest, gates, x1, final_norm)
    return xf.reshape(batch, seq, d)
```

```python
import functools
import math

import jax
import jax.numpy as jnp
import numpy as np
from jax import lax
from jax.experimental import pallas as pl
from jax.experimental.pallas import tpu as pltpu

D_MODEL = 1024
POOL_WIDTH = 512
POOL_WINDOWS = (2, 4, 8, 16)
POOL_GROUPS = len(POOL_WINDOWS)
POOL_GROUP_WIDTH = POOL_WIDTH // POOL_GROUPS
N_HEADS = 8
HEAD_DIM = 64
ATTN_WIDTH = N_HEADS * HEAD_DIM
N_IDX_HEADS = 4
IDX_DIM = 64
IDX_SCALE = (IDX_DIM ** -0.5) * (N_IDX_HEADS ** -0.5)
ATTN_SCALE = HEAD_DIM ** -0.5
TOPK_MAX = 256
REL_BUCKETS = 32
REL_MAX_DIST = 128
N_BRANCHES = 2
N_EXPERTS = 32
TOP_K_EXPERTS = 4
D_FF = D_MODEL
SWIGLU_LIMIT = 7.0
SWIGLU_ALPHA = 1.702
RMS_EPS = 1e-5
SPLIT_SIZES = (POOL_WIDTH, ATTN_WIDTH, ATTN_WIDTH, ATTN_WIDTH,
               N_IDX_HEADS * IDX_DIM, IDX_DIM, N_IDX_HEADS, N_BRANCHES * D_MODEL)

LANES = 128
SUBLANES = 8
VMEM_LIMIT_BYTES = 56 * 1024 * 1024

TM_IN = 512
TQ = 256
TK = 256
TM_POST = 512
TM_ROW = 256
TM_EXP = 256
POOL_HALO = 16
N_BISECT_BF16 = 10
N_BISECT_F32 = 8
BF16_ROWS = 2 * SUBLANES
BF16_STEP = 2.0 ** -7
TINY = 1e-30
PAIR = 2 * HEAD_DIM
IDX_PAD = LANES
ROUTER_PAD = LANES
DMA_UNROLL = 8
N_DMA_PRIORITIES = 2

F32 = jnp.float32
BF16 = jnp.bfloat16
NEG_INF = float("-inf")
M_INIT = -1e30
LOG2E = math.log2(math.e)


def _dot(a, b):
    return jnp.dot(a, b, preferred_element_type=F32)


def _dot_nt(a, b):
    return lax.dot_general(a, b, (((1,), (1,)), ((), ())), preferred_element_type=F32)


def _rmsnorm(x, g):
    ms = jnp.mean(x * x, axis=-1, keepdims=True)
    return x * lax.rsqrt(ms + RMS_EPS) * g


def _const_spec(shape):
    nd = len(shape)
    return pl.BlockSpec(shape, lambda *_: (0,) * nd)


def _params(n_axes):
    return pltpu.CompilerParams(
        dimension_semantics=("arbitrary",) * n_axes,
        vmem_limit_bytes=VMEM_LIMIT_BYTES)


_ROW_SECTIONS = (("pool", POOL_WIDTH), ("k", ATTN_WIDTH), ("ki", IDX_PAD),
                 ("g0", D_MODEL), ("g1", D_MODEL))
_COL_SECTIONS = (("q", ATTN_WIDTH), ("v", ATTN_WIDTH), ("qi", N_IDX_HEADS * IDX_PAD),
                 ("wi", 2 * SUBLANES))


def _section(sections, name):
    start = 0
    for key, width in sections:
        if key == name:
            return slice(start, start + width)
        start += width
    raise KeyError(name)


def _inproj_body(x_ref, g_ref, wrow_ref, wcol_ref, poolw_ref, pscale_ref, wbp_ref,
                 qT_ref, k_ref, vT_ref, qiT_ref, ki_ref, wiT_ref, pp_ref, g1_ref,
                 halo_ref, *, tiles_per_seq):
    i = pl.program_id(0)
    tm = x_ref.shape[0]
    h = _rmsnorm(x_ref[...], g_ref[...]).astype(BF16)
    row_w = lambda name: wrow_ref[:, _section(_ROW_SECTIONS, name)]
    col_w = lambda name: wcol_ref[_section(_COL_SECTIONS, name), :]

    qT = (_dot_nt(col_w("q"), h) * (ATTN_SCALE * LOG2E)).astype(BF16)
    vT = _dot_nt(col_w("v"), h).astype(BF16)
    qiT = _dot_nt(col_w("qi"), h).astype(BF16)
    wiT = _dot_nt(col_w("wi"), h) * IDX_SCALE
    for j in range(tm // TQ):
        qT_ref[j] = qT[:, j * TQ:(j + 1) * TQ]
        qiT_ref[j] = qiT[:, j * TQ:(j + 1) * TQ]
        wiT_ref[j] = wiT[:SUBLANES, j * TQ:(j + 1) * TQ]
    for j in range(tm // TK):
        vT_ref[j] = vT[:, j * TK:(j + 1) * TK]
    k_ref[...] = _dot(h, row_w("k")).astype(BF16)
    ki_ref[...] = _dot(h, row_w("ki")).astype(BF16)
    g1_ref[...] = jax.nn.sigmoid(_dot(h, row_w("g1"))).astype(BF16)

    zp = _dot(h, row_w("pool"))
    seq_tile = lax.rem(i, tiles_per_seq)

    @pl.when(seq_tile == 0)
    def _():
        halo_ref[...] = jnp.zeros_like(halo_ref)

    zext = jnp.concatenate([halo_ref[...], zp], axis=0)
    halo_ref[...] = zp[tm - POOL_HALO:, :]
    gw = POOL_GROUP_WIDTH
    s2 = zext + pltpu.roll(zext, 1, 0)
    s4 = s2[:, gw:] + pltpu.roll(s2[:, gw:], 2, 0)
    s8 = s4[:, gw:] + pltpu.roll(s4[:, gw:], 4, 0)
    s16 = s8[:, gw:] + pltpu.roll(s8[:, gw:], 8, 0)
    wsum = (s2[POOL_HALO:, :gw], s4[POOL_HALO:, :gw], s8[POOL_HALO:, :gw], s16[POOL_HALO:, :])
    t = seq_tile * tm + lax.broadcasted_iota(jnp.int32, (tm, 1), 0)
    mixed = []
    for g, w in enumerate(POOL_WINDOWS):
        cnt = jnp.minimum(t + 1, w).astype(F32)
        pooled = wsum[g] / cnt - zp[:, g * gw:(g + 1) * gw]
        mixed.append(_dot(pooled.astype(BF16), poolw_ref[g]) * pscale_ref[:, g * gw:(g + 1) * gw])
    mixed = jnp.concatenate(mixed, axis=1).astype(BF16)
    y_pool = _dot(mixed, wbp_ref[...])
    gate0 = jax.nn.sigmoid(_dot(h, row_w("g0")))
    pp_ref[...] = (gate0 * y_pool).astype(BF16)


def _inproj(xf, mix_norm, w_in, pool_w, pool_scale, w_branch_pool, seq):
    n, d = xf.shape
    tm = min(TM_IN, seq)
    assert seq % tm == 0 and tm % TK == 0 and tm % TQ == 0 and n % tm == 0
    offs = [0] + [int(o) for o in np.cumsum(SPLIT_SIZES)]
    z_pool, z_q, z_k, z_v, z_qi, z_ki, z_wi, z_gate = (
        w_in[:, offs[j]:offs[j + 1]] for j in range(len(SPLIT_SIZES)))
    pad_cols = lambda a, width: jnp.pad(a, ((0, 0), (0, width - a.shape[1])))
    qi_heads = jnp.pad(z_qi.reshape(d, N_IDX_HEADS, IDX_DIM),
                       ((0, 0), (0, 0), (0, IDX_PAD - IDX_DIM))).reshape(d, N_IDX_HEADS * IDX_PAD)
    parts = {"pool": z_pool, "k": z_k, "ki": pad_cols(z_ki, IDX_PAD),
             "g0": z_gate[:, :D_MODEL], "g1": z_gate[:, D_MODEL:],
             "q": z_q, "v": z_v, "qi": qi_heads, "wi": pad_cols(z_wi, 2 * SUBLANES)}
    w_row = jnp.concatenate([parts[k] for k, _ in _ROW_SECTIONS], axis=1).astype(BF16)
    w_col = jnp.concatenate([parts[k] for k, _ in _COL_SECTIONS], axis=1).astype(BF16).T
    consts = [mix_norm.reshape(1, d).astype(F32), w_row, w_col, pool_w.astype(BF16),
              pool_scale.reshape(1, POOL_WIDTH).astype(F32), w_branch_pool.astype(BF16)]
    grid = (n // tm,)
    row = lambda width: pl.BlockSpec((tm, width), lambda i: (i, 0))
    tiles = lambda t, height: pl.BlockSpec((tm // t, height, t), lambda i: (i, 0, 0))
    out_shape = [
        jax.ShapeDtypeStruct((n // TQ, ATTN_WIDTH, TQ), BF16),
        jax.ShapeDtypeStruct((n, ATTN_WIDTH), BF16),
        jax.ShapeDtypeStruct((n // TK, ATTN_WIDTH, TK), BF16),
        jax.ShapeDtypeStruct((n // TQ, N_IDX_HEADS * IDX_PAD, TQ), BF16),
        jax.ShapeDtypeStruct((n, IDX_PAD), BF16),
        jax.ShapeDtypeStruct((n // TQ, SUBLANES, TQ), F32),
        jax.ShapeDtypeStruct((n, D_MODEL), BF16),
        jax.ShapeDtypeStruct((n, D_MODEL), BF16),
    ]
    out_specs = [
        tiles(TQ, ATTN_WIDTH),
        row(ATTN_WIDTH),
        tiles(TK, ATTN_WIDTH),
        tiles(TQ, N_IDX_HEADS * IDX_PAD),
        row(IDX_PAD),
        tiles(TQ, SUBLANES),
        row(D_MODEL),
        row(D_MODEL),
    ]
    return pl.pallas_call(
        functools.partial(_inproj_body, tiles_per_seq=seq // tm),
        grid=grid,
        in_specs=[row(d)] + [_const_spec(c.shape) for c in consts],
        out_specs=out_specs,
        out_shape=out_shape,
        scratch_shapes=[pltpu.VMEM((POOL_HALO, POOL_WIDTH), F32)],
        compiler_params=_params(1),
        name="inproj",
    )(xf, *consts)


def _rel_thresholds():
    n = np.arange(0, 4 * REL_MAX_DIST)
    max_exact = REL_BUCKETS // 2
    nf = np.maximum(n, 1).astype(np.float32)
    large = max_exact + (np.log(nf / np.float32(max_exact))
                         / np.float32(math.log(REL_MAX_DIST / max_exact))
                         * np.float32(REL_BUCKETS - max_exact)).astype(np.int32)
    bucket = np.where(n < max_exact, n, np.minimum(large, REL_BUCKETS - 1))
    assert np.all(np.diff(bucket) >= 0) and np.all(np.diff(bucket) <= 1)
    assert bucket[-1] == REL_BUCKETS - 1
    return [int(np.argmax(bucket >= b)) for b in range(1, REL_BUCKETS)]


def _attn_body(table_ref, qT_ref, qiT_ref, wiT_ref, k_ref, ki_ref, vT_ref, o_ref,
               score_ref, sb_ref, band_ref, tri_ref, qm_ref, m_ref, l_ref, acc_ref, thr_ref,
               need_ref,
               *, topk):
    b = pl.program_id(0)
    qi = pl.program_id(1)
    tq = o_ref.shape[0]
    nk = qi + 1
    n_keys = score_ref.shape[0] * TK
    key_i = lax.broadcasted_iota(jnp.int32, (TK, tq), 0)
    qry_i = lax.broadcasted_iota(jnp.int32, (TK, tq), 1)

    @pl.when((b == 0) & (qi == 0))
    def _():
        r_i = lax.broadcasted_iota(jnp.int32, (TK, TK), 0)
        c_i = lax.broadcasted_iota(jnp.int32, (TK, TK), 1)
        tri_ref[...] = jnp.where(c_i < r_i, 1.0, 0.0).astype(BF16)
        thresholds = _rel_thresholds()
        for part in range(3):
            dist = qry_i - key_i + (2 - part) * TK
            for h in range(N_HEADS):
                bias = jnp.full((TK, tq), table_ref[h], F32)
                for bkt, thr in enumerate(thresholds, start=1):
                    bias = jnp.where(dist >= thr, table_ref[bkt * N_HEADS + h], bias)
                band_ref[h, part] = jnp.where(dist < 0, NEG_INF, bias * LOG2E)

    first_half = lax.broadcasted_iota(jnp.int32, (PAIR, tq), 0) < HEAD_DIM
    for hp in range(N_HEADS // 2):
        qp = qT_ref[0, hp * PAIR:(hp + 1) * PAIR, :]
        zero = jnp.zeros_like(qp)
        qm_ref[2 * hp] = jnp.where(first_half, qp, zero)
        qm_ref[2 * hp + 1] = jnp.where(first_half, zero, qp)

    wiT = wiT_ref[0]

    def score_tile(kj, carry):
        mx, mn = carry
        ki_t = ki_ref[pl.ds(pl.multiple_of(kj * TK, TK), TK), :]
        sc = jnp.zeros((TK, tq), F32)
        for h in range(N_IDX_HEADS):
            s_h = _dot(ki_t, qiT_ref[0, h * IDX_PAD:(h + 1) * IDX_PAD, :])
            sc = sc + jnp.maximum(s_h, 0.0) * wiT[h:h + 1, :]
        causal = (kj * TK + key_i) <= (qi * tq + qry_i)
        masked = jnp.where(causal, sc, NEG_INF)
        score_ref[kj] = masked
        sb_ref[kj] = masked.astype(BF16)
        mx = jnp.maximum(mx, jnp.max(sc, axis=0, keepdims=True))
        mn = jnp.minimum(mn, jnp.min(sc, axis=0, keepdims=True))
        return mx, mn

    row_max, row_min = lax.fori_loop(
        0, nk, score_tile,
        (jnp.full((1, tq), NEG_INF, F32), jnp.full((1, tq), -NEG_INF, F32)))

    def fold(x, op):
        return op(x.reshape(TK // SUBLANES, SUBLANES, tq), axis=0)

    def count_where(pred):
        def body(kj, acc):
            return acc + fold(jnp.where(pred(score_ref[kj], kj), 1.0, 0.0), jnp.sum)
        acc = lax.fori_loop(0, nk, body, jnp.zeros((SUBLANES, tq), F32))
        return jnp.sum(acc, axis=0, keepdims=True)

    def max_where(pred):
        def body(kj, acc):
            s = score_ref[kj]
            return jnp.maximum(acc, fold(jnp.where(pred(s, kj), s, NEG_INF), jnp.max))
        acc = lax.fori_loop(0, nk, body, jnp.full((SUBLANES, tq), NEG_INF, F32))
        return jnp.max(acc, axis=0, keepdims=True)

    kf = float(topk)
    thr_ref[...] = jnp.full((1, tq), NEG_INF, F32)
    need_ref[...] = jnp.full((1, tq), float(n_keys), F32)

    @pl.when(qi * tq + 1 > topk)
    def _():
        def count_above_bf16(mid_b):
            mid_t = jnp.broadcast_to(mid_b, (TK, tq))
            one, zero = jnp.ones((), BF16), jnp.zeros((), BF16)

            def body(kj, acc):
                m = jnp.where(sb_ref[kj] > mid_t, one, zero)
                parts = [m[r * BF16_ROWS:(r + 1) * BF16_ROWS] for r in range(TK // BF16_ROWS)]
                while len(parts) > 1:
                    parts = [a + b for a, b in zip(parts[::2], parts[1::2])]
                return acc + parts[0].astype(F32)
            acc = lax.fori_loop(0, nk, body, jnp.zeros((BF16_ROWS, tq), F32))
            return jnp.sum(acc, axis=0, keepdims=True)

        def widen(v, sign):
            return v + sign * (jnp.abs(v) * BF16_STEP + TINY)

        def bisect_bf16(_, carry):
            lo, hi = carry
            mid_b = (0.5 * (lo + hi)).astype(BF16)
            above = count_above_bf16(mid_b) >= kf
            mid = mid_b.astype(F32)
            return jnp.where(above, mid, lo), jnp.where(above, hi, mid)

        lo, hi = lax.fori_loop(0, N_BISECT_BF16, bisect_bf16,
                               (widen(row_min, -1.0), widen(row_max, 1.0)))

        def bisect(_, carry):
            lo, hi = carry
            mid = 0.5 * (lo + hi)
            above = count_where(lambda s, kj: s > mid) >= kf
            return jnp.where(above, mid, lo), jnp.where(above, hi, mid)

        _, hi = lax.fori_loop(0, N_BISECT_F32, bisect, (widen(lo, -1.0), widen(hi, 1.0)))
        cand = max_where(lambda s, kj: s <= hi)
        n_ge = count_where(lambda s, kj: s >= cand)

        def unresolved(state):
            it, _, n_ge = state
            return (jnp.min(n_ge) < kf) & (it < n_keys)

        def step(state):
            it, cand, n_ge = state
            nxt = max_where(lambda s, kj: s < cand)
            n_nxt = count_where(lambda s, kj: s >= nxt)
            open_ = n_ge < kf
            return it + 1, jnp.where(open_, nxt, cand), jnp.where(open_, n_nxt, n_ge)

        _, thr, n_ge = lax.while_loop(unresolved, step, (jnp.int32(0), cand, n_ge))
        thr_ref[...] = thr

        @pl.when(jnp.max(n_ge) > kf)
        def _():
            need_ref[...] = kf - count_where(lambda s, kj: s > thr)

    m_ref[...] = jnp.full(m_ref.shape, M_INIT, F32)
    l_ref[...] = jnp.zeros(l_ref.shape, F32)
    acc_ref[...] = jnp.zeros(acc_ref.shape, F32)
    thr = thr_ref[...]
    need = need_ref[...]

    def attend(kj, ties_before):
        part = jnp.clip(kj - qi + 2, 0, 2)
        sc = score_ref[kj]
        tied = jnp.where(sc == thr, 1.0, 0.0)
        rank = _dot(tri_ref[...], tied.astype(BF16)) + ties_before
        ties_before = rank[TK - 1:, :] + tied[TK - 1:, :]
        sel_bias = jnp.where(
            sc > thr, 0.0,
            jnp.where(sc == thr, jnp.where(rank < need, 0.0, NEG_INF), NEG_INF))
        k_t = k_ref[pl.ds(pl.multiple_of(kj * TK, TK), TK), :]
        vT_t = vT_ref[kj]
        logits = [_dot(k_t[:, (h // 2) * PAIR:(h // 2 + 1) * PAIR], qm_ref[h])
                  for h in range(N_HEADS)]
        probs, alphas = [], []
        for h in range(N_HEADS):
            s = logits[h] + band_ref[h, part] + sel_bias
            m_old = m_ref[h]
            m_new = jnp.maximum(m_old, jnp.max(s, axis=0, keepdims=True))
            p = jnp.exp2(s - m_new)
            alpha = jnp.exp2(m_old - m_new)
            l_ref[h] = alpha * l_ref[h] + jnp.sum(p, axis=0, keepdims=True)
            m_ref[h] = m_new
            probs.append(p.astype(BF16))
            alphas.append(alpha)
        for hp in range(N_HEADS // 2):
            vTp = vT_t[hp * PAIR:(hp + 1) * PAIR, :]
            outs = [_dot(vTp, probs[2 * hp + e]) for e in range(2)]
            acc_ref[hp] = (acc_ref[hp] * jnp.where(first_half, alphas[2 * hp], alphas[2 * hp + 1])
                           + jnp.where(first_half, outs[0], outs[1]))
        return ties_before

    lax.fori_loop(0, nk, attend, jnp.zeros((1, tq), F32))
    for hp in range(N_HEADS // 2):
        denom = jnp.where(first_half, l_ref[2 * hp], l_ref[2 * hp + 1])
        o_ref[:, hp * PAIR:(hp + 1) * PAIR] = (acc_ref[hp] / denom).T.astype(o_ref.dtype)


def _attention(qT, k, vT, qiT, ki, wiT, rel_bias, batch, seq):
    n = k.shape[0]
    tq = min(TQ, seq)
    assert tq == TQ == TK and seq % tq == 0
    topk = min(TOPK_MAX, seq // 4)
    assert topk == tq or seq == tq
    nq = seq // tq
    nkt = seq // TK
    table = rel_bias.astype(F32).reshape(REL_BUCKETS * N_HEADS)
    qtile = lambda height: pl.BlockSpec((1, height, tq), lambda b, i: (b * nq + i, 0, 0))
    return pl.pallas_call(
        functools.partial(_attn_body, topk=topk),
        grid=(batch, nq),
        in_specs=[
            pl.BlockSpec(memory_space=pltpu.SMEM),
            qtile(ATTN_WIDTH),
            qtile(N_IDX_HEADS * IDX_PAD),
            qtile(SUBLANES),
            pl.BlockSpec((seq, ATTN_WIDTH), lambda b, i: (b, 0)),
            pl.BlockSpec((seq, IDX_PAD), lambda b, i: (b, 0)),
            pl.BlockSpec((nkt, ATTN_WIDTH, TK), lambda b, i: (b, 0, 0)),
        ],
        out_specs=pl.BlockSpec((tq, ATTN_WIDTH), lambda b, i: (b * nq + i, 0)),
        out_shape=jax.ShapeDtypeStruct((n, ATTN_WIDTH), BF16),
        scratch_shapes=[
            pltpu.VMEM((nkt, TK, tq), F32),
            pltpu.VMEM((nkt, TK, tq), BF16),
            pltpu.VMEM((N_HEADS, 3, TK, tq), F32),
            pltpu.VMEM((TK, TK), BF16),
            pltpu.VMEM((N_HEADS, PAIR, tq), BF16),
            pltpu.VMEM((N_HEADS, 1, tq), F32),
            pltpu.VMEM((N_HEADS, 1, tq), F32),
            pltpu.VMEM((N_HEADS // 2, PAIR, tq), F32),
            pltpu.VMEM((1, tq), F32),
            pltpu.VMEM((1, tq), F32),
        ],
        compiler_params=_params(2),
        name="attention",
    )(table, qT, qiT, wiT, k, ki, vT)


def _post_body(attn_ref, pp_ref, g1_ref, x_ref, wba_ref, wout_ref, fg_ref, wr_hi_ref, wr_lo_ref,
               br_ref, x1_ref, h2_ref, eidx_ref, gates_ref, counts_ref, carry_ref):
    i = pl.program_id(0)
    tm = x_ref.shape[0]
    y_attn = _dot(attn_ref[...], wba_ref[...])
    merged = pp_ref[...].astype(F32) + g1_ref[...].astype(F32) * y_attn
    x1 = x_ref[...] + _dot(merged.astype(BF16), wout_ref[...])
    x1_ref[...] = x1
    h2 = _rmsnorm(x1, fg_ref[...])
    h2_ref[...] = h2

    h_hi = h2.astype(BF16)
    h_lo = (h2 - h_hi.astype(F32)).astype(BF16)
    logits = (_dot(h_hi, wr_hi_ref[...]) + _dot(h_hi, wr_lo_ref[...])
              + _dot(h_lo, wr_hi_ref[...]) + br_ref[...])

    lane = lax.broadcasted_iota(jnp.int32, (tm, ROUTER_PAD), 1)
    work = logits
    vals, idxs = [], []
    for _ in range(TOP_K_EXPERTS):
        mx = jnp.max(work, axis=1, keepdims=True)
        ix = jnp.min(jnp.where(work == mx, lane, ROUTER_PAD), axis=1, keepdims=True)
        vals.append(mx)
        idxs.append(ix)
        work = jnp.where(lane == ix, NEG_INF, work)
    exps = [jnp.exp(v - vals[0]) for v in vals]
    denom = exps[0] + exps[1] + exps[2] + exps[3]

    member = jnp.zeros((tm, ROUTER_PAD), F32)
    for ix in idxs:
        member = member + jnp.where(lane == ix, 1.0, 0.0)

    @pl.when(i == 0)
    def _():
        carry_ref[...] = jnp.zeros_like(carry_ref)

    r_i = lax.broadcasted_iota(jnp.int32, (tm, tm), 0)
    c_i = lax.broadcasted_iota(jnp.int32, (tm, tm), 1)
    strict_lower = jnp.where(c_i < r_i, 1.0, 0.0).astype(BF16)
    before = _dot(strict_lower, member.astype(BF16)) + carry_ref[...]
    carry_new = carry_ref[...] + jnp.sum(member, axis=0, keepdims=True)
    carry_ref[...] = carry_new
    counts_ref[...] = jnp.broadcast_to(carry_new, counts_ref.shape)

    eidx = jnp.zeros((tm, ROUTER_PAD), jnp.int32)
    gates = jnp.zeros((tm, ROUTER_PAD), F32)
    for k in range(TOP_K_EXPERTS):
        rank = jnp.sum(jnp.where(lane == idxs[k], before, 0.0), axis=1, keepdims=True)
        eidx = jnp.where(lane == k, idxs[k], eidx)
        eidx = jnp.where(lane == TOP_K_EXPERTS + k, rank.astype(jnp.int32), eidx)
        gates = jnp.where(lane == k, exps[k] / denom, gates)
    eidx_ref[...] = eidx
    gates_ref[...] = gates


def _post_attn(attn, pp, g1, xf, w_branch_attn, w_out, ffn_norm, w_router, b_router):
    n, d = xf.shape
    tm = min(TM_POST, n)
    assert n % tm == 0
    wr = jnp.pad(w_router.astype(F32), ((0, 0), (0, ROUTER_PAD - N_EXPERTS)))
    wr_hi = wr.astype(BF16)
    wr_lo = (wr - wr_hi.astype(F32)).astype(BF16)
    br = jnp.pad(b_router.astype(F32), (0, ROUTER_PAD - N_EXPERTS),
                 constant_values=NEG_INF).reshape(1, ROUTER_PAD)
    consts = [w_branch_attn.astype(BF16), w_out.astype(BF16),
              ffn_norm.reshape(1, d).astype(F32), wr_hi, wr_lo, br]
    row = lambda width: pl.BlockSpec((tm, width), lambda i: (i, 0))
    return pl.pallas_call(
        _post_body,
        grid=(n // tm,),
        in_specs=[row(ATTN_WIDTH), row(d), row(d), row(d)] + [_const_spec(c.shape) for c in consts],
        out_specs=[row(d), row(d), row(ROUTER_PAD), row(ROUTER_PAD),
                   _const_spec((SUBLANES, ROUTER_PAD))],
        out_shape=[
            jax.ShapeDtypeStruct((n, d), F32),
            jax.ShapeDtypeStruct((n, d), F32),
            jax.ShapeDtypeStruct((n, ROUTER_PAD), jnp.int32),
            jax.ShapeDtypeStruct((n, ROUTER_PAD), F32),
            jax.ShapeDtypeStruct((SUBLANES, ROUTER_PAD), F32),
        ],
        scratch_shapes=[pltpu.VMEM((1, ROUTER_PAD), F32)],
        compiler_params=_params(1),
        name="post_attn",
    )(attn, pp, g1, xf, *consts)


def _dispatch_body(pad_start_ref, pad_count_ref, last_tile_ref, dest_ref, h2_ref, xs_ref,
                   zero_ref, sem, zsem, *, first_tail_tile):
    i = pl.program_id(0)
    tm = h2_ref.shape[0]
    n_tiles = xs_ref.shape[0] // TM_EXP

    def row_copy(r, k):
        return pltpu.make_async_copy(
            h2_ref.at[pl.ds(r, 1)], xs_ref.at[pl.ds(dest_ref[0, 0, r * TOP_K_EXPERTS + k], 1)], sem)

    def issue(c, carry):
        for u in range(DMA_UNROLL):
            for k in range(TOP_K_EXPERTS):
                row_copy(c * DMA_UNROLL + u, k).start(priority=k % N_DMA_PRIORITIES)
        return carry

    lax.fori_loop(0, tm // DMA_UNROLL, issue, 0)

    @pl.when(i == 0)
    def _():
        zero_ref[...] = jnp.zeros_like(zero_ref)

        def pad_copy(e, r):
            return pltpu.make_async_copy(
                zero_ref.at[pl.ds(0, 1)], xs_ref.at[pl.ds(pad_start_ref[e] + r, 1)], zsem)

        def per_expert(e, carry):
            lax.fori_loop(0, pad_count_ref[e], lambda r, c: (pad_copy(e, r).start(), c)[1], 0)
            lax.fori_loop(0, pad_count_ref[e], lambda r, c: (pad_copy(e, r).wait(), c)[1], 0)
            return carry

        lax.fori_loop(0, N_EXPERTS, per_expert, 0)

        def tail_tile(t, carry):
            @pl.when(t > last_tile_ref[0])
            def _():
                cp = pltpu.make_async_copy(
                    zero_ref, xs_ref.at[pl.ds(pl.multiple_of(t * TM_EXP, TM_EXP), TM_EXP)], zsem)
                cp.start()
                cp.wait()
            return carry

        lax.fori_loop(first_tail_tile, n_tiles, tail_tile, 0)

    def drain(c, carry):
        for u in range(DMA_UNROLL):
            for k in range(TOP_K_EXPERTS):
                row_copy(c * DMA_UNROLL + u, k).wait()
        return carry

    lax.fori_loop(0, tm // DMA_UNROLL, drain, 0)


def _dispatch(h2, dest, pad_start, pad_count, last_tile, n_rows):
    n, d = h2.shape
    tm = min(TM_ROW, n)
    assert n % tm == 0 and tm % DMA_UNROLL == 0
    dest_blocks = dest.reshape(n // tm, 1, tm * TOP_K_EXPERTS)
    return pl.pallas_call(
        functools.partial(_dispatch_body, first_tail_tile=(n * TOP_K_EXPERTS) // TM_EXP),
        grid_spec=pltpu.PrefetchScalarGridSpec(
            num_scalar_prefetch=3,
            grid=(n // tm,),
            in_specs=[
                pl.BlockSpec((1, 1, tm * TOP_K_EXPERTS), lambda i, *_: (i, 0, 0),
                             memory_space=pltpu.SMEM),
                pl.BlockSpec((tm, d), lambda i, *_: (i, 0)),
            ],
            out_specs=pl.BlockSpec(memory_space=pl.ANY),
            scratch_shapes=[pltpu.VMEM((TM_EXP, d), F32), pltpu.SemaphoreType.DMA,
                            pltpu.SemaphoreType.DMA],
        ),
        out_shape=jax.ShapeDtypeStruct((n_rows, d), F32),
        compiler_params=_params(1),
        name="dispatch",
    )(pad_start, pad_count, last_tile, dest_blocks, h2)


def _experts_body(tile_expert_ref, tile_rows_ref, run_start_ref, next_expert_ref,
                  xs_ref, w1_hbm, b1_ref, w2_hbm, b2_ref, y_ref,
                  w1f_ref, w2f_ref, w1b_ref, w2b_ref, wsem):
    i = pl.program_id(0)

    def fetch(expert):
        return (pltpu.make_async_copy(w1_hbm.at[expert], w1f_ref, wsem.at[0]),
                pltpu.make_async_copy(w2_hbm.at[expert], w2f_ref, wsem.at[1]))

    @pl.when(i == 0)
    def _():
        for cp in fetch(tile_expert_ref[0]):
            cp.start()

    @pl.when(run_start_ref[i] == 1)
    def _():
        for cp in fetch(tile_expert_ref[i]):
            cp.wait()
        w1b_ref[...] = w1f_ref[...].astype(BF16)
        w2b_ref[...] = w2f_ref[...].astype(BF16)

        @pl.when(next_expert_ref[i] >= 0)
        def _():
            for cp in fetch(next_expert_ref[i]):
                cp.start()

    @pl.when(tile_rows_ref[i] > 0)
    def _():
        x = xs_ref[...].astype(BF16)
        gu = _dot(x, w1b_ref[...]) + b1_ref[0]
        g = jnp.minimum(gu[:, :D_FF], SWIGLU_LIMIT)
        u = jnp.clip(gu[:, D_FF:], -SWIGLU_LIMIT, SWIGLU_LIMIT)
        act = g * jax.nn.sigmoid(SWIGLU_ALPHA * g) * (u + 1.0)
        y_ref[...] = _dot(act.astype(BF16), w2b_ref[...]) + b2_ref[0]

    @pl.when(tile_rows_ref[i] == 0)
    def _():
        y_ref[...] = jnp.zeros_like(y_ref)


def _experts(xs, tile_expert, tile_rows, run_start, next_expert, w1, b1, w2, b2):
    n_rows, d = xs.shape
    n_tiles = n_rows // TM_EXP
    tile = lambda i, *_: (i, 0)
    per_expert = lambda i, te, *_: (te[i], 0, 0)
    return pl.pallas_call(
        _experts_body,
        grid_spec=pltpu.PrefetchScalarGridSpec(
            num_scalar_prefetch=4,
            grid=(n_tiles,),
            in_specs=[
                pl.BlockSpec((TM_EXP, d), tile),
                pl.BlockSpec(memory_space=pl.ANY),
                pl.BlockSpec((1, 1, 2 * D_FF), per_expert),
                pl.BlockSpec(memory_space=pl.ANY),
                pl.BlockSpec((1, 1, d), per_expert),
            ],
            out_specs=pl.BlockSpec((TM_EXP, d), tile),
            scratch_shapes=[
                pltpu.VMEM((d, 2 * D_FF), F32), pltpu.VMEM((D_FF, d), F32),
                pltpu.VMEM((d, 2 * D_FF), BF16), pltpu.VMEM((D_FF, d), BF16),
                pltpu.SemaphoreType.DMA((2,)),
            ],
        ),
        out_shape=jax.ShapeDtypeStruct((n_rows, d), F32),
        compiler_params=_params(1),
        name="experts",
    )(tile_expert, tile_rows, run_start, next_expert, xs, w1,
      b1.reshape(N_EXPERTS, 1, 2 * D_FF), w2, b2.reshape(N_EXPERTS, 1, d))


def _combine_body(dest_ref, gates_ref, x1_ref, fn_ref, y_ref, o_ref, buf_ref, sem):
    tm = x1_ref.shape[0]

    def row_copy(r, k):
        return pltpu.make_async_copy(
            y_ref.at[pl.ds(dest_ref[0, 0, r * TOP_K_EXPERTS + k], 1)],
            buf_ref.at[k, pl.ds(r, 1)], sem)

    def issue(c, carry):
        for u in range(DMA_UNROLL):
            for k in range(TOP_K_EXPERTS):
                row_copy(c * DMA_UNROLL + u, k).start(priority=k % N_DMA_PRIORITIES)
        return carry

    def drain(c, carry):
        for u in range(DMA_UNROLL):
            for k in range(TOP_K_EXPERTS):
                row_copy(c * DMA_UNROLL + u, k).wait()
        return carry

    lax.fori_loop(0, tm // DMA_UNROLL, issue, 0)
    lax.fori_loop(0, tm // DMA_UNROLL, drain, 0)
    gates = gates_ref[...]
    out = x1_ref[...]
    for k in range(TOP_K_EXPERTS):
        out = out + gates[:, k:k + 1] * buf_ref[k]
    o_ref[...] = _rmsnorm(out, fn_ref[...])


def _combine(y, dest, gates, x1, final_norm):
    n, d = x1.shape
    tm = min(TM_ROW, n)
    assert n % tm == 0 and tm % DMA_UNROLL == 0
    dest_blocks = dest.reshape(n // tm, 1, tm * TOP_K_EXPERTS)
    row = lambda width: pl.BlockSpec((tm, width), lambda i: (i, 0))
    return pl.pallas_call(
        _combine_body,
        grid=(n // tm,),
        in_specs=[
            pl.BlockSpec((1, 1, tm * TOP_K_EXPERTS), lambda i: (i, 0, 0), memory_space=pltpu.SMEM),
            row(ROUTER_PAD), row(d), _const_spec((1, d)),
            pl.BlockSpec(memory_space=pl.ANY),
        ],
        out_specs=row(d),
        out_shape=jax.ShapeDtypeStruct((n, d), F32),
        scratch_shapes=[pltpu.VMEM((TOP_K_EXPERTS, tm, d), F32), pltpu.SemaphoreType.DMA],
        compiler_params=_params(1),
        name="combine",
    )(dest_blocks, gates, x1, final_norm.reshape(1, d).astype(F32), y)


def _routing_plan(eidx, counts, n_tiles):
    counts = counts[0, :N_EXPERTS].astype(jnp.int32)
    padded = ((counts + TM_EXP - 1) // TM_EXP) * TM_EXP
    ends = jnp.cumsum(padded)
    starts = ends - padded
    experts = eidx[:, :TOP_K_EXPERTS]
    ranks = eidx[:, TOP_K_EXPERTS:2 * TOP_K_EXPERTS]
    dest = (starts[experts] + ranks).astype(jnp.int32)
    tile_row0 = jnp.arange(n_tiles, dtype=jnp.int32) * TM_EXP
    tile_expert = jnp.minimum(jnp.sum(tile_row0[:, None] >= ends[None, :], axis=1),
                              N_EXPERTS - 1).astype(jnp.int32)
    tile_rows = jnp.clip(counts[tile_expert] - (tile_row0 - starts[tile_expert]), 0, TM_EXP)
    used = tile_row0 < ends[-1]
    tile_rows = jnp.where(used, tile_rows, 0).astype(jnp.int32)
    last_tile = jnp.maximum(ends[-1] // TM_EXP - 1, 0).astype(jnp.int32).reshape(1)
    pad_start = (starts + counts).astype(jnp.int32)
    pad_count = (padded - counts).astype(jnp.int32)
    changed = jnp.concatenate([jnp.ones((1,), bool), tile_expert[1:] != tile_expert[:-1]])
    run_start = (used & changed).astype(jnp.int32)
    ids = jnp.where(counts > 0, jnp.arange(N_EXPERTS, dtype=jnp.int32), N_EXPERTS)
    later = jnp.concatenate([lax.cummin(ids, reverse=True)[1:],
                             jnp.full((1,), N_EXPERTS, jnp.int32)])
    next_expert = jnp.where(later < N_EXPERTS, later, -1)[tile_expert].astype(jnp.int32)
    return dest, tile_expert, tile_rows, run_start, next_expert, last_tile, pad_start, pad_count


def kernel(x, mix_norm, w_in, pool_w, pool_scale, w_branch_pool, w_branch_attn, rel_bias, w_out,
           ffn_norm, w_router, b_router, w1, b1, w2, b2, final_norm):
    batch, seq, d = x.shape
    n = batch * seq
    depth = mix_norm.shape[0]
    assert depth == 1, "the combine kernel fuses the final norm, so only one layer is supported"
    n_tiles = (n * TOP_K_EXPERTS + N_EXPERTS * (TM_EXP - 1) + TM_EXP - 1) // TM_EXP
    xf = x.reshape(n, d)
    for l in range(depth):
        qT, k, vT, qiT, ki, wiT, pp, g1 = _inproj(
            xf, mix_norm[l], w_in[l], pool_w[l], pool_scale[l], w_branch_pool[l], seq)
        attn = _attention(qT, k, vT, qiT, ki, wiT, rel_bias, batch, seq)
        x1, h2, eidx, gates, counts = _post_attn(
            attn, pp, g1, xf, w_branch_attn[l], w_out[l], ffn_norm[l], w_router[l], b_router[l])
        (dest, tile_expert, tile_rows, run_start, next_expert, last_tile, pad_start,
         pad_count) = _routing_plan(eidx, counts, n_tiles)
        xs = _dispatch(h2, dest, pad_start, pad_count, last_tile, n_tiles * TM_EXP)
        y = _experts(xs, tile_expert, tile_rows, run_start, next_expert,
                     w1[l], b1[l], w2[l], b2[l])
        xf = _combine(y, dest, gates, x1, final_norm)
    return xf.reshape(batch, seq, d)
```

```python
import functools
import math

import jax
import jax.numpy as jnp
import numpy as np
from jax import lax
from jax.experimental import pallas as pl
from jax.experimental.pallas import tpu as pltpu

D_MODEL = 1024
POOL_WIDTH = 512
POOL_WINDOWS = (2, 4, 8, 16)
POOL_GROUPS = len(POOL_WINDOWS)
POOL_GROUP_WIDTH = POOL_WIDTH // POOL_GROUPS
N_HEADS = 8
HEAD_DIM = 64
ATTN_WIDTH = N_HEADS * HEAD_DIM
N_IDX_HEADS = 4
IDX_DIM = 64
IDX_SCALE = (IDX_DIM ** -0.5) * (N_IDX_HEADS ** -0.5)
ATTN_SCALE = HEAD_DIM ** -0.5
TOPK_MAX = 256
REL_BUCKETS = 32
REL_MAX_DIST = 128
N_BRANCHES = 2
N_EXPERTS = 32
TOP_K_EXPERTS = 4
D_FF = D_MODEL
SWIGLU_LIMIT = 7.0
SWIGLU_ALPHA = 1.702
RMS_EPS = 1e-5
SPLIT_SIZES = (POOL_WIDTH, ATTN_WIDTH, ATTN_WIDTH, ATTN_WIDTH,
               N_IDX_HEADS * IDX_DIM, IDX_DIM, N_IDX_HEADS, N_BRANCHES * D_MODEL)

LANES = 128
SUBLANES = 8
VMEM_LIMIT_BYTES = 56 * 1024 * 1024

TM_IN = 512
TQ = 256
TK = 256
TM_POST = 512
TM_EXP = 256
ROW_ALIGN = SUBLANES
POOL_HALO = 16
N_BISECT_BF16 = 10
N_BISECT_F32 = 8
BF16_ROWS = 2 * SUBLANES
BF16_STEP = 2.0 ** -7
TINY = 1e-30
PAIR = 2 * HEAD_DIM
IDX_PAD = LANES
ROUTER_PAD = LANES

F32 = jnp.float32
BF16 = jnp.bfloat16
NEG_INF = float("-inf")
M_INIT = -1e30
LOG2E = math.log2(math.e)


def _dot(a, b):
    return jnp.dot(a, b, preferred_element_type=F32)


def _dot_nt(a, b):
    return lax.dot_general(a, b, (((1,), (1,)), ((), ())), preferred_element_type=F32)


def _rmsnorm(x, g):
    ms = jnp.mean(x * x, axis=-1, keepdims=True)
    return x * lax.rsqrt(ms + RMS_EPS) * g


def _const_spec(shape):
    nd = len(shape)
    return pl.BlockSpec(shape, lambda *_: (0,) * nd)


def _params(n_axes):
    return pltpu.CompilerParams(
        dimension_semantics=("arbitrary",) * n_axes,
        vmem_limit_bytes=VMEM_LIMIT_BYTES)


_ROW_SECTIONS = (("pool", POOL_WIDTH), ("k", ATTN_WIDTH), ("ki", IDX_PAD),
                 ("g0", D_MODEL), ("g1", D_MODEL))
_COL_SECTIONS = (("q", ATTN_WIDTH), ("v", ATTN_WIDTH), ("qi", N_IDX_HEADS * IDX_PAD),
                 ("wi", 2 * SUBLANES))


def _section(sections, name):
    start = 0
    for key, width in sections:
        if key == name:
            return slice(start, start + width)
        start += width
    raise KeyError(name)


def _inproj_body(x_ref, g_ref, wrow_ref, wcol_ref, poolw_ref, pscale_ref, wbp_ref,
                 qT_ref, k_ref, vT_ref, qiT_ref, ki_ref, wiT_ref, pp_ref, g1_ref,
                 halo_ref, *, tiles_per_seq):
    i = pl.program_id(0)
    tm = x_ref.shape[0]
    h = _rmsnorm(x_ref[...], g_ref[...]).astype(BF16)
    row_w = lambda name: wrow_ref[:, _section(_ROW_SECTIONS, name)]
    col_w = lambda name: wcol_ref[_section(_COL_SECTIONS, name), :]

    qT = (_dot_nt(col_w("q"), h) * (ATTN_SCALE * LOG2E)).astype(BF16)
    vT = _dot_nt(col_w("v"), h).astype(BF16)
    qiT = _dot_nt(col_w("qi"), h).astype(BF16)
    wiT = _dot_nt(col_w("wi"), h) * IDX_SCALE
    for j in range(tm // TQ):
        qT_ref[j] = qT[:, j * TQ:(j + 1) * TQ]
        qiT_ref[j] = qiT[:, j * TQ:(j + 1) * TQ]
        wiT_ref[j] = wiT[:SUBLANES, j * TQ:(j + 1) * TQ]
    for j in range(tm // TK):
        vT_ref[j] = vT[:, j * TK:(j + 1) * TK]
    k_ref[...] = _dot(h, row_w("k")).astype(BF16)
    ki_ref[...] = _dot(h, row_w("ki")).astype(BF16)
    g1_ref[...] = jax.nn.sigmoid(_dot(h, row_w("g1"))).astype(BF16)

    zp = _dot(h, row_w("pool"))
    seq_tile = lax.rem(i, tiles_per_seq)

    @pl.when(seq_tile == 0)
    def _():
        halo_ref[...] = jnp.zeros_like(halo_ref)

    zext = jnp.concatenate([halo_ref[...], zp], axis=0)
    halo_ref[...] = zp[tm - POOL_HALO:, :]
    gw = POOL_GROUP_WIDTH
    s2 = zext + pltpu.roll(zext, 1, 0)
    s4 = s2[:, gw:] + pltpu.roll(s2[:, gw:], 2, 0)
    s8 = s4[:, gw:] + pltpu.roll(s4[:, gw:], 4, 0)
    s16 = s8[:, gw:] + pltpu.roll(s8[:, gw:], 8, 0)
    wsum = (s2[POOL_HALO:, :gw], s4[POOL_HALO:, :gw], s8[POOL_HALO:, :gw], s16[POOL_HALO:, :])
    t = seq_tile * tm + lax.broadcasted_iota(jnp.int32, (tm, 1), 0)
    mixed = []
    for g, w in enumerate(POOL_WINDOWS):
        cnt = jnp.minimum(t + 1, w).astype(F32)
        pooled = wsum[g] / cnt - zp[:, g * gw:(g + 1) * gw]
        mixed.append(_dot(pooled.astype(BF16), poolw_ref[g]) * pscale_ref[:, g * gw:(g + 1) * gw])
    mixed = jnp.concatenate(mixed, axis=1).astype(BF16)
    y_pool = _dot(mixed, wbp_ref[...])
    gate0 = jax.nn.sigmoid(_dot(h, row_w("g0")))
    pp_ref[...] = (gate0 * y_pool).astype(BF16)


def _inproj(xf, mix_norm, w_in, pool_w, pool_scale, w_branch_pool, seq):
    n, d = xf.shape
    tm = min(TM_IN, seq)
    assert seq % tm == 0 and tm % TK == 0 and tm % TQ == 0 and n % tm == 0
    offs = [0] + [int(o) for o in np.cumsum(SPLIT_SIZES)]
    z_pool, z_q, z_k, z_v, z_qi, z_ki, z_wi, z_gate = (
        w_in[:, offs[j]:offs[j + 1]] for j in range(len(SPLIT_SIZES)))
    pad_cols = lambda a, width: jnp.pad(a, ((0, 0), (0, width - a.shape[1])))
    qi_heads = jnp.pad(z_qi.reshape(d, N_IDX_HEADS, IDX_DIM),
                       ((0, 0), (0, 0), (0, IDX_PAD - IDX_DIM))).reshape(d, N_IDX_HEADS * IDX_PAD)
    parts = {"pool": z_pool, "k": z_k, "ki": pad_cols(z_ki, IDX_PAD),
             "g0": z_gate[:, :D_MODEL], "g1": z_gate[:, D_MODEL:],
             "q": z_q, "v": z_v, "qi": qi_heads, "wi": pad_cols(z_wi, 2 * SUBLANES)}
    w_row = jnp.concatenate([parts[k] for k, _ in _ROW_SECTIONS], axis=1).astype(BF16)
    w_col = jnp.concatenate([parts[k] for k, _ in _COL_SECTIONS], axis=1).astype(BF16).T
    consts = [mix_norm.reshape(1, d).astype(F32), w_row, w_col, pool_w.astype(BF16),
              pool_scale.reshape(1, POOL_WIDTH).astype(F32), w_branch_pool.astype(BF16)]
    grid = (n // tm,)
    row = lambda width: pl.BlockSpec((tm, width), lambda i: (i, 0))
    tiles = lambda t, height: pl.BlockSpec((tm // t, height, t), lambda i: (i, 0, 0))
    out_shape = [
        jax.ShapeDtypeStruct((n // TQ, ATTN_WIDTH, TQ), BF16),
        jax.ShapeDtypeStruct((n, ATTN_WIDTH), BF16),
        jax.ShapeDtypeStruct((n // TK, ATTN_WIDTH, TK), BF16),
        jax.ShapeDtypeStruct((n // TQ, N_IDX_HEADS * IDX_PAD, TQ), BF16),
        jax.ShapeDtypeStruct((n, IDX_PAD), BF16),
        jax.ShapeDtypeStruct((n // TQ, SUBLANES, TQ), F32),
        jax.ShapeDtypeStruct((n, D_MODEL), BF16),
        jax.ShapeDtypeStruct((n, D_MODEL), BF16),
    ]
    out_specs = [
        tiles(TQ, ATTN_WIDTH),
        row(ATTN_WIDTH),
        tiles(TK, ATTN_WIDTH),
        tiles(TQ, N_IDX_HEADS * IDX_PAD),
        row(IDX_PAD),
        tiles(TQ, SUBLANES),
        row(D_MODEL),
        row(D_MODEL),
    ]
    return pl.pallas_call(
        functools.partial(_inproj_body, tiles_per_seq=seq // tm),
        grid=grid,
        in_specs=[row(d)] + [_const_spec(c.shape) for c in consts],
        out_specs=out_specs,
        out_shape=out_shape,
        scratch_shapes=[pltpu.VMEM((POOL_HALO, POOL_WIDTH), F32)],
        compiler_params=_params(1),
        name="inproj",
    )(xf, *consts)


def _rel_thresholds():
    n = np.arange(0, 4 * REL_MAX_DIST)
    max_exact = REL_BUCKETS // 2
    nf = np.maximum(n, 1).astype(np.float32)
    large = max_exact + (np.log(nf / np.float32(max_exact))
                         / np.float32(math.log(REL_MAX_DIST / max_exact))
                         * np.float32(REL_BUCKETS - max_exact)).astype(np.int32)
    bucket = np.where(n < max_exact, n, np.minimum(large, REL_BUCKETS - 1))
    assert np.all(np.diff(bucket) >= 0) and np.all(np.diff(bucket) <= 1)
    assert bucket[-1] == REL_BUCKETS - 1
    return [int(np.argmax(bucket >= b)) for b in range(1, REL_BUCKETS)]


def _attn_body(table_ref, qT_ref, qiT_ref, wiT_ref, k_ref, ki_ref, vT_ref, o_ref,
               score_ref, sb_ref, band_ref, tri_ref, qm_ref, m_ref, l_ref, acc_ref, thr_ref,
               need_ref,
               *, topk):
    b = pl.program_id(0)
    qi = pl.program_id(1)
    tq = o_ref.shape[0]
    nk = qi + 1
    n_keys = score_ref.shape[0] * TK
    key_i = lax.broadcasted_iota(jnp.int32, (TK, tq), 0)
    qry_i = lax.broadcasted_iota(jnp.int32, (TK, tq), 1)

    @pl.when((b == 0) & (qi == 0))
    def _():
        r_i = lax.broadcasted_iota(jnp.int32, (TK, TK), 0)
        c_i = lax.broadcasted_iota(jnp.int32, (TK, TK), 1)
        tri_ref[...] = jnp.where(c_i < r_i, 1.0, 0.0).astype(BF16)
        thresholds = _rel_thresholds()
        for part in range(3):
            dist = qry_i - key_i + (2 - part) * TK
            for h in range(N_HEADS):
                bias = jnp.full((TK, tq), table_ref[h], F32)
                for bkt, thr in enumerate(thresholds, start=1):
                    bias = jnp.where(dist >= thr, table_ref[bkt * N_HEADS + h], bias)
                band_ref[h, part] = jnp.where(dist < 0, NEG_INF, bias * LOG2E)

    first_half = lax.broadcasted_iota(jnp.int32, (PAIR, tq), 0) < HEAD_DIM
    for hp in range(N_HEADS // 2):
        qp = qT_ref[0, hp * PAIR:(hp + 1) * PAIR, :]
        zero = jnp.zeros_like(qp)
        qm_ref[2 * hp] = jnp.where(first_half, qp, zero)
        qm_ref[2 * hp + 1] = jnp.where(first_half, zero, qp)

    wiT = wiT_ref[0]

    def score_tile(kj, carry):
        mx, mn = carry
        ki_t = ki_ref[pl.ds(pl.multiple_of(kj * TK, TK), TK), :]
        sc = jnp.zeros((TK, tq), F32)
        for h in range(N_IDX_HEADS):
            s_h = _dot(ki_t, qiT_ref[0, h * IDX_PAD:(h + 1) * IDX_PAD, :])
            sc = sc + jnp.maximum(s_h, 0.0) * wiT[h:h + 1, :]
        causal = (kj * TK + key_i) <= (qi * tq + qry_i)
        masked = jnp.where(causal, sc, NEG_INF)
        score_ref[kj] = masked
        sb_ref[kj] = masked.astype(BF16)
        mx = jnp.maximum(mx, jnp.max(sc, axis=0, keepdims=True))
        mn = jnp.minimum(mn, jnp.min(sc, axis=0, keepdims=True))
        return mx, mn

    row_max, row_min = lax.fori_loop(
        0, nk, score_tile,
        (jnp.full((1, tq), NEG_INF, F32), jnp.full((1, tq), -NEG_INF, F32)))

    def fold(x, op):
        return op(x.reshape(TK // SUBLANES, SUBLANES, tq), axis=0)

    def count_where(pred):
        def body(kj, acc):
            return acc + fold(jnp.where(pred(score_ref[kj], kj), 1.0, 0.0), jnp.sum)
        acc = lax.fori_loop(0, nk, body, jnp.zeros((SUBLANES, tq), F32))
        return jnp.sum(acc, axis=0, keepdims=True)

    def max_where(pred):
        def body(kj, acc):
            s = score_ref[kj]
            return jnp.maximum(acc, fold(jnp.where(pred(s, kj), s, NEG_INF), jnp.max))
        acc = lax.fori_loop(0, nk, body, jnp.full((SUBLANES, tq), NEG_INF, F32))
        return jnp.max(acc, axis=0, keepdims=True)

    kf = float(topk)
    thr_ref[...] = jnp.full((1, tq), NEG_INF, F32)
    need_ref[...] = jnp.full((1, tq), float(n_keys), F32)

    @pl.when(qi * tq + 1 > topk)
    def _():
        def count_above_bf16(mid_b):
            mid_t = jnp.broadcast_to(mid_b, (TK, tq))
            one, zero = jnp.ones((), BF16), jnp.zeros((), BF16)

            def body(kj, acc):
                m = jnp.where(sb_ref[kj] > mid_t, one, zero)
                parts = [m[r * BF16_ROWS:(r + 1) * BF16_ROWS] for r in range(TK // BF16_ROWS)]
                while len(parts) > 1:
                    parts = [a + b for a, b in zip(parts[::2], parts[1::2])]
                return acc + parts[0].astype(F32)
            acc = lax.fori_loop(0, nk, body, jnp.zeros((BF16_ROWS, tq), F32))
            return jnp.sum(acc, axis=0, keepdims=True)

        def widen(v, sign):
            return v + sign * (jnp.abs(v) * BF16_STEP + TINY)

        def bisect_bf16(_, carry):
            lo, hi = carry
            mid_b = (0.5 * (lo + hi)).astype(BF16)
            above = count_above_bf16(mid_b) >= kf
            mid = mid_b.astype(F32)
            return jnp.where(above, mid, lo), jnp.where(above, hi, mid)

        lo, hi = lax.fori_loop(0, N_BISECT_BF16, bisect_bf16,
                               (widen(row_min, -1.0), widen(row_max, 1.0)))

        def bisect(_, carry):
            lo, hi = carry
            mid = 0.5 * (lo + hi)
            above = count_where(lambda s, kj: s > mid) >= kf
            return jnp.where(above, mid, lo), jnp.where(above, hi, mid)

        _, hi = lax.fori_loop(0, N_BISECT_F32, bisect, (widen(lo, -1.0), widen(hi, 1.0)))
        cand = max_where(lambda s, kj: s <= hi)
        n_ge = count_where(lambda s, kj: s >= cand)

        def unresolved(state):
            it, _, n_ge = state
            return (jnp.min(n_ge) < kf) & (it < n_keys)

        def step(state):
            it, cand, n_ge = state
            nxt = max_where(lambda s, kj: s < cand)
            n_nxt = count_where(lambda s, kj: s >= nxt)
            open_ = n_ge < kf
            return it + 1, jnp.where(open_, nxt, cand), jnp.where(open_, n_nxt, n_ge)

        _, thr, n_ge = lax.while_loop(unresolved, step, (jnp.int32(0), cand, n_ge))
        thr_ref[...] = thr

        @pl.when(jnp.max(n_ge) > kf)
        def _():
            need_ref[...] = kf - count_where(lambda s, kj: s > thr)

    m_ref[...] = jnp.full(m_ref.shape, M_INIT, F32)
    l_ref[...] = jnp.zeros(l_ref.shape, F32)
    acc_ref[...] = jnp.zeros(acc_ref.shape, F32)
    thr = thr_ref[...]
    need = need_ref[...]

    def attend(kj, ties_before):
        part = jnp.clip(kj - qi + 2, 0, 2)
        sc = score_ref[kj]
        tied = jnp.where(sc == thr, 1.0, 0.0)
        rank = _dot(tri_ref[...], tied.astype(BF16)) + ties_before
        ties_before = rank[TK - 1:, :] + tied[TK - 1:, :]
        sel_bias = jnp.where(
            sc > thr, 0.0,
            jnp.where(sc == thr, jnp.where(rank < need, 0.0, NEG_INF), NEG_INF))
        k_t = k_ref[pl.ds(pl.multiple_of(kj * TK, TK), TK), :]
        vT_t = vT_ref[kj]
        logits = [_dot(k_t[:, (h // 2) * PAIR:(h // 2 + 1) * PAIR], qm_ref[h])
                  for h in range(N_HEADS)]
        probs, alphas = [], []
        for h in range(N_HEADS):
            s = logits[h] + band_ref[h, part] + sel_bias
            m_old = m_ref[h]
            m_new = jnp.maximum(m_old, jnp.max(s, axis=0, keepdims=True))
            p = jnp.exp2(s - m_new)
            alpha = jnp.exp2(m_old - m_new)
            l_ref[h] = alpha * l_ref[h] + jnp.sum(p, axis=0, keepdims=True)
            m_ref[h] = m_new
            probs.append(p.astype(BF16))
            alphas.append(alpha)
        for hp in range(N_HEADS // 2):
            vTp = vT_t[hp * PAIR:(hp + 1) * PAIR, :]
            outs = [_dot(vTp, probs[2 * hp + e]) for e in range(2)]
            acc_ref[hp] = (acc_ref[hp] * jnp.where(first_half, alphas[2 * hp], alphas[2 * hp + 1])
                           + jnp.where(first_half, outs[0], outs[1]))
        return ties_before

    lax.fori_loop(0, nk, attend, jnp.zeros((1, tq), F32))
    for hp in range(N_HEADS // 2):
        denom = jnp.where(first_half, l_ref[2 * hp], l_ref[2 * hp + 1])
        o_ref[:, hp * PAIR:(hp + 1) * PAIR] = (acc_ref[hp] / denom).T.astype(o_ref.dtype)


def _attention(qT, k, vT, qiT, ki, wiT, rel_bias, batch, seq):
    n = k.shape[0]
    tq = min(TQ, seq)
    assert tq == TQ == TK and seq % tq == 0
    topk = min(TOPK_MAX, seq // 4)
    assert topk == tq or seq == tq
    nq = seq // tq
    nkt = seq // TK
    table = rel_bias.astype(F32).reshape(REL_BUCKETS * N_HEADS)
    qtile = lambda height: pl.BlockSpec((1, height, tq), lambda b, i: (b * nq + i, 0, 0))
    return pl.pallas_call(
        functools.partial(_attn_body, topk=topk),
        grid=(batch, nq),
        in_specs=[
            pl.BlockSpec(memory_space=pltpu.SMEM),
            qtile(ATTN_WIDTH),
            qtile(N_IDX_HEADS * IDX_PAD),
            qtile(SUBLANES),
            pl.BlockSpec((seq, ATTN_WIDTH), lambda b, i: (b, 0)),
            pl.BlockSpec((seq, IDX_PAD), lambda b, i: (b, 0)),
            pl.BlockSpec((nkt, ATTN_WIDTH, TK), lambda b, i: (b, 0, 0)),
        ],
        out_specs=pl.BlockSpec((tq, ATTN_WIDTH), lambda b, i: (b * nq + i, 0)),
        out_shape=jax.ShapeDtypeStruct((n, ATTN_WIDTH), BF16),
        scratch_shapes=[
            pltpu.VMEM((nkt, TK, tq), F32),
            pltpu.VMEM((nkt, TK, tq), BF16),
            pltpu.VMEM((N_HEADS, 3, TK, tq), F32),
            pltpu.VMEM((TK, TK), BF16),
            pltpu.VMEM((N_HEADS, PAIR, tq), BF16),
            pltpu.VMEM((N_HEADS, 1, tq), F32),
            pltpu.VMEM((N_HEADS, 1, tq), F32),
            pltpu.VMEM((N_HEADS // 2, PAIR, tq), F32),
            pltpu.VMEM((1, tq), F32),
            pltpu.VMEM((1, tq), F32),
        ],
        compiler_params=_params(2),
        name="attention",
    )(table, qT, qiT, wiT, k, ki, vT)


def _post_body(attn_ref, pp_ref, g1_ref, x_ref, wba_ref, wout_ref, fg_ref, wr_hi_ref, wr_lo_ref,
               br_ref, x1_ref, h2_ref, lpos_ref, gates_ref, runs_ref, counts_ref, carry_ref):
    i = pl.program_id(0)
    tm = x_ref.shape[0]
    y_attn = _dot(attn_ref[...], wba_ref[...])
    merged = pp_ref[...].astype(F32) + g1_ref[...].astype(F32) * y_attn
    x1 = x_ref[...] + _dot(merged.astype(BF16), wout_ref[...])
    x1_ref[...] = x1
    h2 = _rmsnorm(x1, fg_ref[...])
    h2_ref[...] = h2

    h_hi = h2.astype(BF16)
    h_lo = (h2 - h_hi.astype(F32)).astype(BF16)
    logits = (_dot(h_hi, wr_hi_ref[...]) + _dot(h_hi, wr_lo_ref[...])
              + _dot(h_lo, wr_hi_ref[...]) + br_ref[...])

    lane = lax.broadcasted_iota(jnp.int32, (tm, ROUTER_PAD), 1)
    work = logits
    vals, idxs = [], []
    for _ in range(TOP_K_EXPERTS):
        mx = jnp.max(work, axis=1, keepdims=True)
        ix = jnp.min(jnp.where(work == mx, lane, ROUTER_PAD), axis=1, keepdims=True)
        vals.append(mx)
        idxs.append(ix)
        work = jnp.where(lane == ix, NEG_INF, work)
    exps = [jnp.exp(v - vals[0]) for v in vals]
    denom = exps[0] + exps[1] + exps[2] + exps[3]

    member = jnp.zeros((tm, ROUTER_PAD), F32)
    for ix in idxs:
        member = member + jnp.where(lane == ix, 1.0, 0.0)

    @pl.when(i == 0)
    def _():
        carry_ref[...] = jnp.zeros_like(carry_ref)

    r_i = lax.broadcasted_iota(jnp.int32, (tm, tm), 0)
    c_i = lax.broadcasted_iota(jnp.int32, (tm, tm), 1)
    strict_lower = jnp.where(c_i < r_i, 1.0, 0.0).astype(BF16)
    local = _dot(strict_lower, member.astype(BF16))
    count = jnp.sum(member, axis=0, keepdims=True)
    run_len = jnp.floor((count + (ROW_ALIGN - 1)) * (1.0 / ROW_ALIGN)) * ROW_ALIGN
    e_r = lax.broadcasted_iota(jnp.int32, (ROUTER_PAD, ROUTER_PAD), 0)
    e_c = lax.broadcasted_iota(jnp.int32, (ROUTER_PAD, ROUTER_PAD), 1)
    strict_upper = jnp.where(e_r < e_c, 1.0, 0.0).astype(BF16)
    run_off = _dot(jnp.broadcast_to(run_len, (SUBLANES, ROUTER_PAD)).astype(BF16),
                   strict_upper)[:1, :]
    run_start = carry_ref[...]
    carry_new = run_start + run_len
    carry_ref[...] = carry_new
    counts_ref[...] = jnp.broadcast_to(carry_new, counts_ref.shape)
    sub = lax.broadcasted_iota(jnp.int32, (SUBLANES, ROUTER_PAD), 0)
    runs_ref[0] = jnp.where(sub == 0, run_start, jnp.where(sub == 1, run_len,
                                                          jnp.where(sub == 2, run_off, 0.0)))

    slot = run_off + local
    lpos = jnp.zeros((tm, ROUTER_PAD), jnp.int32)
    gates = jnp.zeros((tm, ROUTER_PAD), F32)
    for k in range(TOP_K_EXPERTS):
        pos = jnp.sum(jnp.where(lane == idxs[k], slot, 0.0), axis=1, keepdims=True)
        lpos = jnp.where(lane == k, pos.astype(jnp.int32), lpos)
        gates = jnp.where(lane == k, exps[k] / denom, gates)
    lpos_ref[...] = lpos
    gates_ref[...] = gates


def _post_attn(attn, pp, g1, xf, w_branch_attn, w_out, ffn_norm, w_router, b_router):
    n, d = xf.shape
    tm = min(TM_POST, n)
    assert n % tm == 0
    wr = jnp.pad(w_router.astype(F32), ((0, 0), (0, ROUTER_PAD - N_EXPERTS)))
    wr_hi = wr.astype(BF16)
    wr_lo = (wr - wr_hi.astype(F32)).astype(BF16)
    br = jnp.pad(b_router.astype(F32), (0, ROUTER_PAD - N_EXPERTS),
                 constant_values=NEG_INF).reshape(1, ROUTER_PAD)
    consts = [w_branch_attn.astype(BF16), w_out.astype(BF16),
              ffn_norm.reshape(1, d).astype(F32), wr_hi, wr_lo, br]
    row = lambda width: pl.BlockSpec((tm, width), lambda i: (i, 0))
    return pl.pallas_call(
        _post_body,
        grid=(n // tm,),
        in_specs=[row(ATTN_WIDTH), row(d), row(d), row(d)] + [_const_spec(c.shape) for c in consts],
        out_specs=[row(d), row(d), row(ROUTER_PAD), row(ROUTER_PAD),
                   pl.BlockSpec((1, SUBLANES, ROUTER_PAD), lambda i: (i, 0, 0)),
                   _const_spec((SUBLANES, ROUTER_PAD))],
        out_shape=[
            jax.ShapeDtypeStruct((n, d), F32),
            jax.ShapeDtypeStruct((n, d), F32),
            jax.ShapeDtypeStruct((n, ROUTER_PAD), jnp.int32),
            jax.ShapeDtypeStruct((n, ROUTER_PAD), F32),
            jax.ShapeDtypeStruct((n // tm, SUBLANES, ROUTER_PAD), F32),
            jax.ShapeDtypeStruct((SUBLANES, ROUTER_PAD), F32),
        ],
        scratch_shapes=[pltpu.VMEM((1, ROUTER_PAD), F32)],
        compiler_params=_params(1),
        name="post_attn",
    )(attn, pp, g1, xf, *consts)


def _block_copies(length, make_copy, max_block):
    block = max_block
    while block >= ROW_ALIGN:
        offset = pl.multiple_of(jnp.bitwise_and(length, -2 * block), ROW_ALIGN)
        yield jnp.bitwise_and(length, block) != 0, make_copy(offset, block)
        block //= 2


def _for_each_block(length, make_copy, max_block, action):
    for pred, cp in _block_copies(length, make_copy, max_block):
        @pl.when(pred)
        def _(cp=cp):
            action(cp)


def _staging_rows(tm):
    return TOP_K_EXPERTS * tm + N_EXPERTS * ROW_ALIGN


def _dispatch_body(run_dst_ref, run_len_ref, run_off_ref, tail_start_ref, tail_len_ref,
                   last_tile_ref, lpos_ref, h2_ref, xs_ref, buf_ref, zero_ref, sem, zsem,
                   *, first_tail_tile):
    i = pl.program_id(0)
    tm = h2_ref.shape[0]
    n_tiles = xs_ref.shape[0] // TM_EXP
    rows = buf_ref.shape[0]

    lpos_t = lpos_ref[...].T
    slot = lax.broadcasted_iota(jnp.int32, (rows, tm), 0)
    onehot = jnp.zeros((rows, tm), F32)
    for k in range(TOP_K_EXPERTS):
        onehot = onehot + jnp.where(slot == lpos_t[k:k + 1, :], 1.0, 0.0)
    buf_ref[...] = _dot(onehot.astype(BF16), h2_ref[...].astype(BF16))

    def run_copy(e):
        base = i * N_EXPERTS + e
        src0 = pl.multiple_of(run_off_ref[base], ROW_ALIGN)
        dst0 = pl.multiple_of(run_dst_ref[base], ROW_ALIGN)
        return run_len_ref[base], lambda off, blk: pltpu.make_async_copy(
            buf_ref.at[pl.ds(src0 + off, blk)], xs_ref.at[pl.ds(dst0 + off, blk)], sem)

    def start_runs(e, carry):
        length, make = run_copy(e)
        _for_each_block(length, make, tm, lambda cp: cp.start())
        return carry

    def wait_runs(e, carry):
        length, make = run_copy(e)
        _for_each_block(length, make, tm, lambda cp: cp.wait())
        return carry

    lax.fori_loop(0, N_EXPERTS, start_runs, 0)

    @pl.when(i == 0)
    def _():
        zero_ref[...] = jnp.zeros_like(zero_ref)

        def tail_copy(e):
            dst0 = pl.multiple_of(tail_start_ref[e], ROW_ALIGN)
            return tail_len_ref[e], lambda off, blk: pltpu.make_async_copy(
                zero_ref.at[pl.ds(0, blk)], xs_ref.at[pl.ds(dst0 + off, blk)], zsem)

        def fill_tail(e, carry):
            length, make = tail_copy(e)
            _for_each_block(length, make, TM_EXP // 2, lambda cp: cp.start())
            _for_each_block(length, make, TM_EXP // 2, lambda cp: cp.wait())
            return carry

        lax.fori_loop(0, N_EXPERTS, fill_tail, 0)

        def tail_tile(t, carry):
            @pl.when(t > last_tile_ref[0])
            def _():
                cp = pltpu.make_async_copy(
                    zero_ref, xs_ref.at[pl.ds(pl.multiple_of(t * TM_EXP, TM_EXP), TM_EXP)], zsem)
                cp.start()
                cp.wait()
            return carry

        lax.fori_loop(first_tail_tile, n_tiles, tail_tile, 0)

    lax.fori_loop(0, N_EXPERTS, wait_runs, 0)


def _dispatch(h2, lpos, plan, n_rows):
    n, d = h2.shape
    tm = min(TM_POST, n)
    assert n % tm == 0 and tm % ROW_ALIGN == 0
    return pl.pallas_call(
        functools.partial(_dispatch_body, first_tail_tile=(n * TOP_K_EXPERTS) // TM_EXP),
        grid_spec=pltpu.PrefetchScalarGridSpec(
            num_scalar_prefetch=6,
            grid=(n // tm,),
            in_specs=[
                pl.BlockSpec((tm, ROUTER_PAD), lambda i, *_: (i, 0)),
                pl.BlockSpec((tm, d), lambda i, *_: (i, 0)),
            ],
            out_specs=pl.BlockSpec(memory_space=pl.ANY),
            scratch_shapes=[pltpu.VMEM((_staging_rows(tm), d), F32), pltpu.VMEM((TM_EXP, d), F32),
                            pltpu.SemaphoreType.DMA, pltpu.SemaphoreType.DMA],
        ),
        out_shape=jax.ShapeDtypeStruct((n_rows, d), F32),
        compiler_params=_params(1),
        name="dispatch",
    )(plan["run_dst"], plan["run_len"], plan["run_off"], plan["tail_start"], plan["tail_len"],
      plan["last_tile"], lpos, h2)


def _experts_body(tile_expert_ref, tile_rows_ref, run_start_ref, next_expert_ref,
                  xs_ref, w1_hbm, b1_ref, w2_hbm, b2_ref, y_ref,
                  w1f_ref, w2f_ref, w1b_ref, w2b_ref, wsem):
    i = pl.program_id(0)

    def fetch(expert):
        return (pltpu.make_async_copy(w1_hbm.at[expert], w1f_ref, wsem.at[0]),
                pltpu.make_async_copy(w2_hbm.at[expert], w2f_ref, wsem.at[1]))

    @pl.when(i == 0)
    def _():
        for cp in fetch(tile_expert_ref[0]):
            cp.start()

    @pl.when(run_start_ref[i] == 1)
    def _():
        for cp in fetch(tile_expert_ref[i]):
            cp.wait()
        w1b_ref[...] = w1f_ref[...].astype(BF16)
        w2b_ref[...] = w2f_ref[...].astype(BF16)

        @pl.when(next_expert_ref[i] >= 0)
        def _():
            for cp in fetch(next_expert_ref[i]):
                cp.start()

    @pl.when(tile_rows_ref[i] > 0)
    def _():
        x = xs_ref[...].astype(BF16)
        gu = _dot(x, w1b_ref[...]) + b1_ref[0]
        g = jnp.minimum(gu[:, :D_FF], SWIGLU_LIMIT)
        u = jnp.clip(gu[:, D_FF:], -SWIGLU_LIMIT, SWIGLU_LIMIT)
        act = g * jax.nn.sigmoid(SWIGLU_ALPHA * g) * (u + 1.0)
        y_ref[...] = _dot(act.astype(BF16), w2b_ref[...]) + b2_ref[0]

    @pl.when(tile_rows_ref[i] == 0)
    def _():
        y_ref[...] = jnp.zeros_like(y_ref)


def _experts(xs, tile_expert, tile_rows, run_start, next_expert, w1, b1, w2, b2):
    n_rows, d = xs.shape
    n_tiles = n_rows // TM_EXP
    tile = lambda i, *_: (i, 0)
    per_expert = lambda i, te, *_: (te[i], 0, 0)
    return pl.pallas_call(
        _experts_body,
        grid_spec=pltpu.PrefetchScalarGridSpec(
            num_scalar_prefetch=4,
            grid=(n_tiles,),
            in_specs=[
                pl.BlockSpec((TM_EXP, d), tile),
                pl.BlockSpec(memory_space=pl.ANY),
                pl.BlockSpec((1, 1, 2 * D_FF), per_expert),
                pl.BlockSpec(memory_space=pl.ANY),
                pl.BlockSpec((1, 1, d), per_expert),
            ],
            out_specs=pl.BlockSpec((TM_EXP, d), tile),
            scratch_shapes=[
                pltpu.VMEM((d, 2 * D_FF), F32), pltpu.VMEM((D_FF, d), F32),
                pltpu.VMEM((d, 2 * D_FF), BF16), pltpu.VMEM((D_FF, d), BF16),
                pltpu.SemaphoreType.DMA((2,)),
            ],
        ),
        out_shape=jax.ShapeDtypeStruct((n_rows, d), F32),
        compiler_params=_params(1),
        name="experts",
    )(tile_expert, tile_rows, run_start, next_expert, xs, w1,
      b1.reshape(N_EXPERTS, 1, 2 * D_FF), w2, b2.reshape(N_EXPERTS, 1, d))


def _combine_body(run_dst_ref, run_len_ref, run_off_ref, lpos_ref, gates_ref, x1_ref, fn_ref,
                  y_ref, o_ref, buf_ref, sem):
    i = pl.program_id(0)
    tm = x1_ref.shape[0]
    rows = buf_ref.shape[0]

    @pl.when(i == 0)
    def _():
        buf_ref[...] = jnp.zeros_like(buf_ref)

    def run_copy(e):
        base = i * N_EXPERTS + e
        src0 = pl.multiple_of(run_dst_ref[base], ROW_ALIGN)
        dst0 = pl.multiple_of(run_off_ref[base], ROW_ALIGN)
        return run_len_ref[base], lambda off, blk: pltpu.make_async_copy(
            y_ref.at[pl.ds(src0 + off, blk)], buf_ref.at[pl.ds(dst0 + off, blk)], sem)

    def start_runs(e, carry):
        length, make = run_copy(e)
        _for_each_block(length, make, tm, lambda cp: cp.start())
        return carry

    def wait_runs(e, carry):
        length, make = run_copy(e)
        _for_each_block(length, make, tm, lambda cp: cp.wait())
        return carry

    lax.fori_loop(0, N_EXPERTS, start_runs, 0)
    lax.fori_loop(0, N_EXPERTS, wait_runs, 0)

    lpos = lpos_ref[...]
    gates = gates_ref[...]
    slot = lax.broadcasted_iota(jnp.int32, (tm, rows), 1)
    weights = jnp.zeros((tm, rows), F32)
    for k in range(TOP_K_EXPERTS):
        weights = weights + jnp.where(slot == lpos[:, k:k + 1], gates[:, k:k + 1], 0.0)
    out = x1_ref[...] + _dot(weights.astype(BF16), buf_ref[...].astype(BF16))
    o_ref[...] = _rmsnorm(out, fn_ref[...])


def _combine(y, lpos, gates, x1, final_norm, plan):
    n, d = x1.shape
    tm = min(TM_POST, n)
    assert n % tm == 0
    row = lambda width: pl.BlockSpec((tm, width), lambda i, *_: (i, 0))
    return pl.pallas_call(
        _combine_body,
        grid_spec=pltpu.PrefetchScalarGridSpec(
            num_scalar_prefetch=3,
            grid=(n // tm,),
            in_specs=[
                row(ROUTER_PAD), row(ROUTER_PAD), row(d),
                pl.BlockSpec((1, d), lambda i, *_: (0, 0)),
                pl.BlockSpec(memory_space=pl.ANY),
            ],
            out_specs=row(d),
            scratch_shapes=[pltpu.VMEM((_staging_rows(tm), d), F32), pltpu.SemaphoreType.DMA],
        ),
        out_shape=jax.ShapeDtypeStruct((n, d), F32),
        compiler_params=_params(1),
        name="combine",
    )(plan["run_dst"], plan["run_len"], plan["run_off"], lpos, gates, x1,
      final_norm.reshape(1, d).astype(F32), y)


def _routing_plan(runs, counts, n_tiles):
    counts = counts[0, :N_EXPERTS].astype(jnp.int32)
    padded = ((counts + TM_EXP - 1) // TM_EXP) * TM_EXP
    ends = jnp.cumsum(padded)
    starts = ends - padded
    runs = runs[:, :, :N_EXPERTS].astype(jnp.int32)
    flat = lambda a: a.reshape(-1).astype(jnp.int32)
    tile_row0 = jnp.arange(n_tiles, dtype=jnp.int32) * TM_EXP
    tile_expert = jnp.minimum(jnp.sum(tile_row0[:, None] >= ends[None, :], axis=1),
                              N_EXPERTS - 1).astype(jnp.int32)
    tile_rows = jnp.clip(counts[tile_expert] - (tile_row0 - starts[tile_expert]), 0, TM_EXP)
    used = tile_row0 < ends[-1]
    tile_rows = jnp.where(used, tile_rows, 0).astype(jnp.int32)
    changed = jnp.concatenate([jnp.ones((1,), bool), tile_expert[1:] != tile_expert[:-1]])
    ids = jnp.where(counts > 0, jnp.arange(N_EXPERTS, dtype=jnp.int32), N_EXPERTS)
    later = jnp.concatenate([lax.cummin(ids, reverse=True)[1:],
                             jnp.full((1,), N_EXPERTS, jnp.int32)])
    return {
        "tile_expert": tile_expert,
        "tile_rows": tile_rows,
        "first_of_expert": (used & changed).astype(jnp.int32),
        "next_expert": jnp.where(later < N_EXPERTS, later, -1)[tile_expert].astype(jnp.int32),
        "last_tile": jnp.maximum(ends[-1] // TM_EXP - 1, 0).astype(jnp.int32).reshape(1),
        "tail_start": (starts + counts).astype(jnp.int32),
        "tail_len": (padded - counts).astype(jnp.int32),
        "run_dst": flat(starts[None, :] + runs[:, 0, :]),
        "run_len": flat(runs[:, 1, :]),
        "run_off": flat(runs[:, 2, :]),
    }


def kernel(x, mix_norm, w_in, pool_w, pool_scale, w_branch_pool, w_branch_attn, rel_bias, w_out,
           ffn_norm, w_router, b_router, w1, b1, w2, b2, final_norm):
    batch, seq, d = x.shape
    n = batch * seq
    depth = mix_norm.shape[0]
    assert depth == 1, "the combine kernel fuses the final norm, so only one layer is supported"
    token_tiles = n // min(TM_POST, n)
    max_rows = (n * TOP_K_EXPERTS + N_EXPERTS * token_tiles * (ROW_ALIGN - 1)
                + N_EXPERTS * (TM_EXP - 1))
    n_tiles = (max_rows + TM_EXP - 1) // TM_EXP
    xf = x.reshape(n, d)
    for l in range(depth):
        qT, k, vT, qiT, ki, wiT, pp, g1 = _inproj(
            xf, mix_norm[l], w_in[l], pool_w[l], pool_scale[l], w_branch_pool[l], seq)
        attn = _attention(qT, k, vT, qiT, ki, wiT, rel_bias, batch, seq)
        x1, h2, lpos, gates, runs, counts = _post_attn(
            attn, pp, g1, xf, w_branch_attn[l], w_out[l], ffn_norm[l], w_router[l], b_router[l])
        plan = _routing_plan(runs, counts, n_tiles)
        xs = _dispatch(h2, lpos, plan, n_tiles * TM_EXP)
        y = _experts(xs, plan["tile_expert"], plan["tile_rows"], plan["first_of_expert"],
                     plan["next_expert"], w1[l], b1[l], w2[l], b2[l])
        xf = _combine(y, lpos, gates, x1, final_norm, plan)
    return xf.reshape(batch, seq, d)
```

```python
import functools
import math

import jax
import jax.numpy as jnp
import numpy as np
from jax import lax
from jax.experimental import pallas as pl
from jax.experimental.pallas import tpu as pltpu

D_MODEL = 1024
POOL_WIDTH = 512
POOL_WINDOWS = (2, 4, 8, 16)
POOL_GROUPS = len(POOL_WINDOWS)
POOL_GROUP_WIDTH = POOL_WIDTH // POOL_GROUPS
N_HEADS = 8
HEAD_DIM = 64
ATTN_WIDTH = N_HEADS * HEAD_DIM
N_IDX_HEADS = 4
IDX_DIM = 64
IDX_SCALE = (IDX_DIM ** -0.5) * (N_IDX_HEADS ** -0.5)
ATTN_SCALE = HEAD_DIM ** -0.5
TOPK_MAX = 256
REL_BUCKETS = 32
REL_MAX_DIST = 128
N_BRANCHES = 2
N_EXPERTS = 32
TOP_K_EXPERTS = 4
D_FF = D_MODEL
SWIGLU_LIMIT = 7.0
SWIGLU_ALPHA = 1.702
RMS_EPS = 1e-5
SPLIT_SIZES = (POOL_WIDTH, ATTN_WIDTH, ATTN_WIDTH, ATTN_WIDTH,
               N_IDX_HEADS * IDX_DIM, IDX_DIM, N_IDX_HEADS, N_BRANCHES * D_MODEL)

LANES = 128
SUBLANES = 8
VMEM_LIMIT_BYTES = 56 * 1024 * 1024

TM_IN = 512
TQ = 256
TK = 256
TM_POST = 512
TM_EXP = 256
ROW_ALIGN = SUBLANES
POOL_HALO = 16
N_BISECT_BF16 = 10
N_BISECT_F32 = 8
BF16_ROWS = 2 * SUBLANES
BF16_STEP = 2.0 ** -7
TINY = 1e-30
PAIR = 2 * HEAD_DIM
IDX_PAD = LANES
ROUTER_PAD = LANES

F32 = jnp.float32
BF16 = jnp.bfloat16
NEG_INF = float("-inf")
M_INIT = -1e30
LOG2E = math.log2(math.e)


def _dot(a, b):
    return jnp.dot(a, b, preferred_element_type=F32)


def _dot_nt(a, b):
    return lax.dot_general(a, b, (((1,), (1,)), ((), ())), preferred_element_type=F32)


def _rmsnorm(x, g):
    ms = jnp.mean(x * x, axis=-1, keepdims=True)
    return x * lax.rsqrt(ms + RMS_EPS) * g


def _const_spec(shape):
    nd = len(shape)
    return pl.BlockSpec(shape, lambda *_: (0,) * nd)


def _params(n_axes):
    return pltpu.CompilerParams(
        dimension_semantics=("arbitrary",) * n_axes,
        vmem_limit_bytes=VMEM_LIMIT_BYTES)


_ROW_SECTIONS = (("pool", POOL_WIDTH), ("k", ATTN_WIDTH), ("ki", IDX_PAD),
                 ("g0", D_MODEL), ("g1", D_MODEL))
_COL_SECTIONS = (("q", ATTN_WIDTH), ("v", ATTN_WIDTH), ("qi", N_IDX_HEADS * IDX_PAD),
                 ("wi", 2 * SUBLANES))


def _section(sections, name):
    start = 0
    for key, width in sections:
        if key == name:
            return slice(start, start + width)
        start += width
    raise KeyError(name)


def _inproj_body(x_ref, g_ref, wrow_ref, wcol_ref, poolw_ref, pscale_ref, wbp_ref,
                 qT_ref, k_ref, vT_ref, qiT_ref, ki_ref, wiT_ref, pp_ref, g1_ref,
                 halo_ref, *, tiles_per_seq):
    i = pl.program_id(0)
    tm = x_ref.shape[0]
    h = _rmsnorm(x_ref[...], g_ref[...]).astype(BF16)
    row_w = lambda name: wrow_ref[:, _section(_ROW_SECTIONS, name)]
    col_w = lambda name: wcol_ref[_section(_COL_SECTIONS, name), :]

    qT = (_dot_nt(col_w("q"), h) * (ATTN_SCALE * LOG2E)).astype(BF16)
    vT = _dot_nt(col_w("v"), h).astype(BF16)
    qiT = _dot_nt(col_w("qi"), h).astype(BF16)
    wiT = _dot_nt(col_w("wi"), h) * IDX_SCALE
    for j in range(tm // TQ):
        qT_ref[j] = qT[:, j * TQ:(j + 1) * TQ]
        qiT_ref[j] = qiT[:, j * TQ:(j + 1) * TQ]
        wiT_ref[j] = wiT[:SUBLANES, j * TQ:(j + 1) * TQ]
    for j in range(tm // TK):
        vT_ref[j] = vT[:, j * TK:(j + 1) * TK]
    k_ref[...] = _dot(h, row_w("k")).astype(BF16)
    ki_ref[...] = _dot(h, row_w("ki")).astype(BF16)
    g1_ref[...] = jax.nn.sigmoid(_dot(h, row_w("g1"))).astype(BF16)

    zp = _dot(h, row_w("pool"))
    seq_tile = lax.rem(i, tiles_per_seq)

    @pl.when(seq_tile == 0)
    def _():
        halo_ref[...] = jnp.zeros_like(halo_ref)

    zext = jnp.concatenate([halo_ref[...], zp], axis=0)
    halo_ref[...] = zp[tm - POOL_HALO:, :]
    gw = POOL_GROUP_WIDTH
    s2 = zext + pltpu.roll(zext, 1, 0)
    s4 = s2[:, gw:] + pltpu.roll(s2[:, gw:], 2, 0)
    s8 = s4[:, gw:] + pltpu.roll(s4[:, gw:], 4, 0)
    s16 = s8[:, gw:] + pltpu.roll(s8[:, gw:], 8, 0)
    wsum = (s2[POOL_HALO:, :gw], s4[POOL_HALO:, :gw], s8[POOL_HALO:, :gw], s16[POOL_HALO:, :])
    t = seq_tile * tm + lax.broadcasted_iota(jnp.int32, (tm, 1), 0)
    mixed = []
    for g, w in enumerate(POOL_WINDOWS):
        cnt = jnp.minimum(t + 1, w).astype(F32)
        pooled = wsum[g] / cnt - zp[:, g * gw:(g + 1) * gw]
        mixed.append(_dot(pooled.astype(BF16), poolw_ref[g]) * pscale_ref[:, g * gw:(g + 1) * gw])
    mixed = jnp.concatenate(mixed, axis=1).astype(BF16)
    y_pool = _dot(mixed, wbp_ref[...])
    gate0 = jax.nn.sigmoid(_dot(h, row_w("g0")))
    pp_ref[...] = (gate0 * y_pool).astype(BF16)


def _inproj(xf, mix_norm, w_in, pool_w, pool_scale, w_branch_pool, seq):
    n, d = xf.shape
    tm = min(TM_IN, seq)
    assert seq % tm == 0 and tm % TK == 0 and tm % TQ == 0 and n % tm == 0
    offs = [0] + [int(o) for o in np.cumsum(SPLIT_SIZES)]
    z_pool, z_q, z_k, z_v, z_qi, z_ki, z_wi, z_gate = (
        w_in[:, offs[j]:offs[j + 1]] for j in range(len(SPLIT_SIZES)))
    pad_cols = lambda a, width: jnp.pad(a, ((0, 0), (0, width - a.shape[1])))
    qi_heads = jnp.pad(z_qi.reshape(d, N_IDX_HEADS, IDX_DIM),
                       ((0, 0), (0, 0), (0, IDX_PAD - IDX_DIM))).reshape(d, N_IDX_HEADS * IDX_PAD)
    parts = {"pool": z_pool, "k": z_k, "ki": pad_cols(z_ki, IDX_PAD),
             "g0": z_gate[:, :D_MODEL], "g1": z_gate[:, D_MODEL:],
             "q": z_q, "v": z_v, "qi": qi_heads, "wi": pad_cols(z_wi, 2 * SUBLANES)}
    w_row = jnp.concatenate([parts[k] for k, _ in _ROW_SECTIONS], axis=1).astype(BF16)
    w_col = jnp.concatenate([parts[k] for k, _ in _COL_SECTIONS], axis=1).astype(BF16).T
    consts = [mix_norm.reshape(1, d).astype(F32), w_row, w_col, pool_w.astype(BF16),
              pool_scale.reshape(1, POOL_WIDTH).astype(F32), w_branch_pool.astype(BF16)]
    grid = (n // tm,)
    row = lambda width: pl.BlockSpec((tm, width), lambda i: (i, 0))
    tiles = lambda t, height: pl.BlockSpec((tm // t, height, t), lambda i: (i, 0, 0))
    out_shape = [
        jax.ShapeDtypeStruct((n // TQ, ATTN_WIDTH, TQ), BF16),
        jax.ShapeDtypeStruct((n, ATTN_WIDTH), BF16),
        jax.ShapeDtypeStruct((n // TK, ATTN_WIDTH, TK), BF16),
        jax.ShapeDtypeStruct((n // TQ, N_IDX_HEADS * IDX_PAD, TQ), BF16),
        jax.ShapeDtypeStruct((n, IDX_PAD), BF16),
        jax.ShapeDtypeStruct((n // TQ, SUBLANES, TQ), F32),
        jax.ShapeDtypeStruct((n, D_MODEL), BF16),
        jax.ShapeDtypeStruct((n, D_MODEL), BF16),
    ]
    out_specs = [
        tiles(TQ, ATTN_WIDTH),
        row(ATTN_WIDTH),
        tiles(TK, ATTN_WIDTH),
        tiles(TQ, N_IDX_HEADS * IDX_PAD),
        row(IDX_PAD),
        tiles(TQ, SUBLANES),
        row(D_MODEL),
        row(D_MODEL),
    ]
    return pl.pallas_call(
        functools.partial(_inproj_body, tiles_per_seq=seq // tm),
        grid=grid,
        in_specs=[row(d)] + [_const_spec(c.shape) for c in consts],
        out_specs=out_specs,
        out_shape=out_shape,
        scratch_shapes=[pltpu.VMEM((POOL_HALO, POOL_WIDTH), F32)],
        compiler_params=_params(1),
        name="inproj",
    )(xf, *consts)


def _rel_thresholds():
    n = np.arange(0, 4 * REL_MAX_DIST)
    max_exact = REL_BUCKETS // 2
    nf = np.maximum(n, 1).astype(np.float32)
    large = max_exact + (np.log(nf / np.float32(max_exact))
                         / np.float32(math.log(REL_MAX_DIST / max_exact))
                         * np.float32(REL_BUCKETS - max_exact)).astype(np.int32)
    bucket = np.where(n < max_exact, n, np.minimum(large, REL_BUCKETS - 1))
    assert np.all(np.diff(bucket) >= 0) and np.all(np.diff(bucket) <= 1)
    assert bucket[-1] == REL_BUCKETS - 1
    return [int(np.argmax(bucket >= b)) for b in range(1, REL_BUCKETS)]


def _attn_body(table_ref, qT_ref, qiT_ref, wiT_ref, k_ref, ki_ref, vT_ref, o_ref,
               score_ref, sb_ref, band_ref, tri_ref, qm_ref, m_ref, l_ref, acc_ref, thr_ref,
               need_ref,
               *, topk):
    b = pl.program_id(0)
    qi = pl.program_id(1)
    tq = o_ref.shape[0]
    nk = qi + 1
    n_keys = score_ref.shape[0] * TK
    key_i = lax.broadcasted_iota(jnp.int32, (TK, tq), 0)
    qry_i = lax.broadcasted_iota(jnp.int32, (TK, tq), 1)

    @pl.when((b == 0) & (qi == 0))
    def _():
        r_i = lax.broadcasted_iota(jnp.int32, (TK, TK), 0)
        c_i = lax.broadcasted_iota(jnp.int32, (TK, TK), 1)
        tri_ref[...] = jnp.where(c_i < r_i, 1.0, 0.0).astype(BF16)
        thresholds = _rel_thresholds()
        for part in range(3):
            dist = qry_i - key_i + (2 - part) * TK
            for h in range(N_HEADS):
                bias = jnp.full((TK, tq), table_ref[h], F32)
                for bkt, thr in enumerate(thresholds, start=1):
                    bias = jnp.where(dist >= thr, table_ref[bkt * N_HEADS + h], bias)
                band_ref[h, part] = jnp.where(dist < 0, NEG_INF, bias * LOG2E)

    first_half = lax.broadcasted_iota(jnp.int32, (PAIR, tq), 0) < HEAD_DIM
    for hp in range(N_HEADS // 2):
        qp = qT_ref[0, hp * PAIR:(hp + 1) * PAIR, :]
        zero = jnp.zeros_like(qp)
        qm_ref[2 * hp] = jnp.where(first_half, qp, zero)
        qm_ref[2 * hp + 1] = jnp.where(first_half, zero, qp)

    wiT = wiT_ref[0]

    def score_tile(kj, carry):
        mx, mn = carry
        ki_t = ki_ref[pl.ds(pl.multiple_of(kj * TK, TK), TK), :]
        heads = [_dot(ki_t, qiT_ref[0, h * IDX_PAD:(h + 1) * IDX_PAD, :])
                 for h in range(N_IDX_HEADS)]
        sc = jnp.maximum(heads[0], 0.0) * wiT[0:1, :]
        for h in range(1, N_IDX_HEADS):
            sc = sc + jnp.maximum(heads[h], 0.0) * wiT[h:h + 1, :]
        causal = (kj * TK + key_i) <= (qi * tq + qry_i)
        masked = jnp.where(causal, sc, NEG_INF)
        score_ref[kj] = masked
        sb_ref[kj] = masked.astype(BF16)
        mx = jnp.maximum(mx, jnp.max(sc, axis=0, keepdims=True))
        mn = jnp.minimum(mn, jnp.min(sc, axis=0, keepdims=True))
        return mx, mn

    row_max, row_min = lax.fori_loop(
        0, nk, score_tile,
        (jnp.full((1, tq), NEG_INF, F32), jnp.full((1, tq), -NEG_INF, F32)))

    def fold(x, op):
        return op(x.reshape(TK // SUBLANES, SUBLANES, tq), axis=0)

    def count_where(pred):
        def body(kj, acc):
            return acc + fold(jnp.where(pred(score_ref[kj], kj), 1.0, 0.0), jnp.sum)
        acc = lax.fori_loop(0, nk, body, jnp.zeros((SUBLANES, tq), F32))
        return jnp.sum(acc, axis=0, keepdims=True)

    def max_where(pred):
        def body(kj, acc):
            s = score_ref[kj]
            return jnp.maximum(acc, fold(jnp.where(pred(s, kj), s, NEG_INF), jnp.max))
        acc = lax.fori_loop(0, nk, body, jnp.full((SUBLANES, tq), NEG_INF, F32))
        return jnp.max(acc, axis=0, keepdims=True)

    kf = float(topk)
    thr_ref[...] = jnp.full((1, tq), NEG_INF, F32)
    need_ref[...] = jnp.full((1, tq), float(n_keys), F32)

    @pl.when(qi * tq + 1 > topk)
    def _():
        def count_above_bf16(mid_b):
            mid_t = jnp.broadcast_to(mid_b, (TK, tq))
            one, zero = jnp.ones((), BF16), jnp.zeros((), BF16)

            def body(kj, acc):
                m = jnp.where(sb_ref[kj] > mid_t, one, zero)
                parts = [m[r * BF16_ROWS:(r + 1) * BF16_ROWS] for r in range(TK // BF16_ROWS)]
                while len(parts) > 1:
                    parts = [a + b for a, b in zip(parts[::2], parts[1::2])]
                return acc + parts[0].astype(F32)
            acc = lax.fori_loop(0, nk, body, jnp.zeros((BF16_ROWS, tq), F32))
            return jnp.sum(acc, axis=0, keepdims=True)

        def widen(v, sign):
            return v + sign * (jnp.abs(v) * BF16_STEP + TINY)

        def bisect_bf16(_, carry):
            lo, hi = carry
            mid_b = (0.5 * (lo + hi)).astype(BF16)
            above = count_above_bf16(mid_b) >= kf
            mid = mid_b.astype(F32)
            return jnp.where(above, mid, lo), jnp.where(above, hi, mid)

        lo, hi = lax.fori_loop(0, N_BISECT_BF16, bisect_bf16,
                               (widen(row_min, -1.0), widen(row_max, 1.0)))

        def bisect(_, carry):
            lo, hi = carry
            mid = 0.5 * (lo + hi)
            above = count_where(lambda s, kj: s > mid) >= kf
            return jnp.where(above, mid, lo), jnp.where(above, hi, mid)

        _, hi = lax.fori_loop(0, N_BISECT_F32, bisect, (widen(lo, -1.0), widen(hi, 1.0)))
        cand = max_where(lambda s, kj: s <= hi)
        n_ge = count_where(lambda s, kj: s >= cand)

        def unresolved(state):
            it, _, n_ge = state
            return (jnp.min(n_ge) < kf) & (it < n_keys)

        def step(state):
            it, cand, n_ge = state
            nxt = max_where(lambda s, kj: s < cand)
            n_nxt = count_where(lambda s, kj: s >= nxt)
            open_ = n_ge < kf
            return it + 1, jnp.where(open_, nxt, cand), jnp.where(open_, n_nxt, n_ge)

        _, thr, n_ge = lax.while_loop(unresolved, step, (jnp.int32(0), cand, n_ge))
        thr_ref[...] = thr

        @pl.when(jnp.max(n_ge) > kf)
        def _():
            need_ref[...] = kf - count_where(lambda s, kj: s > thr)

    m_ref[...] = jnp.full(m_ref.shape, M_INIT, F32)
    l_ref[...] = jnp.zeros(l_ref.shape, F32)
    acc_ref[...] = jnp.zeros(acc_ref.shape, F32)
    thr = thr_ref[...]
    need = need_ref[...]

    def attend(kj, ties_before):
        part = jnp.clip(kj - qi + 2, 0, 2)
        sc = score_ref[kj]
        tied = jnp.where(sc == thr, 1.0, 0.0)
        rank = _dot(tri_ref[...], tied.astype(BF16)) + ties_before
        ties_before = rank[TK - 1:, :] + tied[TK - 1:, :]
        sel_bias = jnp.where(
            sc > thr, 0.0,
            jnp.where(sc == thr, jnp.where(rank < need, 0.0, NEG_INF), NEG_INF))
        k_t = k_ref[pl.ds(pl.multiple_of(kj * TK, TK), TK), :]
        vT_t = vT_ref[kj]
        logits = [_dot(k_t[:, (h // 2) * PAIR:(h // 2 + 1) * PAIR], qm_ref[h])
                  for h in range(N_HEADS)]
        probs, alphas = [], []
        for h in range(N_HEADS):
            s = logits[h] + band_ref[h, part] + sel_bias
            m_old = m_ref[h]
            m_new = jnp.maximum(m_old, jnp.max(s, axis=0, keepdims=True))
            p = jnp.exp2(s - m_new)
            alpha = jnp.exp2(m_old - m_new)
            l_ref[h] = alpha * l_ref[h] + jnp.sum(p, axis=0, keepdims=True)
            m_ref[h] = m_new
            probs.append(p.astype(BF16))
            alphas.append(alpha)
        for hp in range(N_HEADS // 2):
            vTp = vT_t[hp * PAIR:(hp + 1) * PAIR, :]
            outs = [_dot(vTp, probs[2 * hp + e]) for e in range(2)]
            acc_ref[hp] = (acc_ref[hp] * jnp.where(first_half, alphas[2 * hp], alphas[2 * hp + 1])
                           + jnp.where(first_half, outs[0], outs[1]))
        return ties_before

    lax.fori_loop(0, nk, attend, jnp.zeros((1, tq), F32))
    for hp in range(N_HEADS // 2):
        denom = jnp.where(first_half, l_ref[2 * hp], l_ref[2 * hp + 1])
        o_ref[:, hp * PAIR:(hp + 1) * PAIR] = (acc_ref[hp] / denom).T.astype(o_ref.dtype)


def _attention(qT, k, vT, qiT, ki, wiT, rel_bias, batch, seq):
    n = k.shape[0]
    tq = min(TQ, seq)
    assert tq == TQ == TK and seq % tq == 0
    topk = min(TOPK_MAX, seq // 4)
    assert topk == tq or seq == tq
    nq = seq // tq
    nkt = seq // TK
    table = rel_bias.astype(F32).reshape(REL_BUCKETS * N_HEADS)
    qtile = lambda height: pl.BlockSpec((1, height, tq), lambda b, i: (b * nq + i, 0, 0))
    return pl.pallas_call(
        functools.partial(_attn_body, topk=topk),
        grid=(batch, nq),
        in_specs=[
            pl.BlockSpec(memory_space=pltpu.SMEM),
            qtile(ATTN_WIDTH),
            qtile(N_IDX_HEADS * IDX_PAD),
            qtile(SUBLANES),
            pl.BlockSpec((seq, ATTN_WIDTH), lambda b, i: (b, 0)),
            pl.BlockSpec((seq, IDX_PAD), lambda b, i: (b, 0)),
            pl.BlockSpec((nkt, ATTN_WIDTH, TK), lambda b, i: (b, 0, 0)),
        ],
        out_specs=pl.BlockSpec((tq, ATTN_WIDTH), lambda b, i: (b * nq + i, 0)),
        out_shape=jax.ShapeDtypeStruct((n, ATTN_WIDTH), BF16),
        scratch_shapes=[
            pltpu.VMEM((nkt, TK, tq), F32),
            pltpu.VMEM((nkt, TK, tq), BF16),
            pltpu.VMEM((N_HEADS, 3, TK, tq), F32),
            pltpu.VMEM((TK, TK), BF16),
            pltpu.VMEM((N_HEADS, PAIR, tq), BF16),
            pltpu.VMEM((N_HEADS, 1, tq), F32),
            pltpu.VMEM((N_HEADS, 1, tq), F32),
            pltpu.VMEM((N_HEADS // 2, PAIR, tq), F32),
            pltpu.VMEM((1, tq), F32),
            pltpu.VMEM((1, tq), F32),
        ],
        compiler_params=_params(2),
        name="attention",
    )(table, qT, qiT, wiT, k, ki, vT)


def _post_body(attn_ref, pp_ref, g1_ref, x_ref, wba_ref, wout_ref, fg_ref, wr_hi_ref, wr_lo_ref,
               br_ref, x1_ref, h2_ref, lpos_ref, gates_ref, runs_ref, counts_ref, carry_ref):
    i = pl.program_id(0)
    tm = x_ref.shape[0]
    y_attn = _dot(attn_ref[...], wba_ref[...])
    merged = pp_ref[...].astype(F32) + g1_ref[...].astype(F32) * y_attn
    x1 = x_ref[...] + _dot(merged.astype(BF16), wout_ref[...])
    x1_ref[...] = x1
    h2 = _rmsnorm(x1, fg_ref[...])
    h2_ref[...] = h2

    h_hi = h2.astype(BF16)
    h_lo = (h2 - h_hi.astype(F32)).astype(BF16)
    logits = (_dot(h_hi, wr_hi_ref[...]) + _dot(h_hi, wr_lo_ref[...])
              + _dot(h_lo, wr_hi_ref[...]) + br_ref[...])

    lane = lax.broadcasted_iota(jnp.int32, (tm, ROUTER_PAD), 1)
    work = logits
    vals, idxs = [], []
    for _ in range(TOP_K_EXPERTS):
        mx = jnp.max(work, axis=1, keepdims=True)
        ix = jnp.min(jnp.where(work == mx, lane, ROUTER_PAD), axis=1, keepdims=True)
        vals.append(mx)
        idxs.append(ix)
        work = jnp.where(lane == ix, NEG_INF, work)
    exps = [jnp.exp(v - vals[0]) for v in vals]
    denom = exps[0] + exps[1] + exps[2] + exps[3]

    member = jnp.zeros((tm, ROUTER_PAD), F32)
    for ix in idxs:
        member = member + jnp.where(lane == ix, 1.0, 0.0)

    @pl.when(i == 0)
    def _():
        carry_ref[...] = jnp.zeros_like(carry_ref)

    r_i = lax.broadcasted_iota(jnp.int32, (tm, tm), 0)
    c_i = lax.broadcasted_iota(jnp.int32, (tm, tm), 1)
    strict_lower = jnp.where(c_i < r_i, 1.0, 0.0).astype(BF16)
    local = _dot(strict_lower, member.astype(BF16))
    count = jnp.sum(member, axis=0, keepdims=True)
    run_len = jnp.floor((count + (ROW_ALIGN - 1)) * (1.0 / ROW_ALIGN)) * ROW_ALIGN
    e_r = lax.broadcasted_iota(jnp.int32, (ROUTER_PAD, ROUTER_PAD), 0)
    e_c = lax.broadcasted_iota(jnp.int32, (ROUTER_PAD, ROUTER_PAD), 1)
    strict_upper = jnp.where(e_r < e_c, 1.0, 0.0).astype(BF16)
    run_off = _dot(jnp.broadcast_to(run_len, (SUBLANES, ROUTER_PAD)).astype(BF16),
                   strict_upper)[:1, :]
    run_start = carry_ref[...]
    carry_new = run_start + run_len
    carry_ref[...] = carry_new
    counts_ref[...] = jnp.broadcast_to(carry_new, counts_ref.shape)
    sub = lax.broadcasted_iota(jnp.int32, (SUBLANES, ROUTER_PAD), 0)
    runs_ref[0] = jnp.where(sub == 0, run_start, jnp.where(sub == 1, run_len,
                                                          jnp.where(sub == 2, run_off, 0.0)))

    slot = run_off + local
    lpos = jnp.zeros((tm, ROUTER_PAD), jnp.int32)
    gates = jnp.zeros((tm, ROUTER_PAD), F32)
    for k in range(TOP_K_EXPERTS):
        pos = jnp.sum(jnp.where(lane == idxs[k], slot, 0.0), axis=1, keepdims=True)
        lpos = jnp.where(lane == k, pos.astype(jnp.int32), lpos)
        gates = jnp.where(lane == k, exps[k] / denom, gates)
    lpos_ref[...] = lpos
    gates_ref[...] = gates


def _post_attn(attn, pp, g1, xf, w_branch_attn, w_out, ffn_norm, w_router, b_router):
    n, d = xf.shape
    tm = min(TM_POST, n)
    assert n % tm == 0
    wr = jnp.pad(w_router.astype(F32), ((0, 0), (0, ROUTER_PAD - N_EXPERTS)))
    wr_hi = wr.astype(BF16)
    wr_lo = (wr - wr_hi.astype(F32)).astype(BF16)
    br = jnp.pad(b_router.astype(F32), (0, ROUTER_PAD - N_EXPERTS),
                 constant_values=NEG_INF).reshape(1, ROUTER_PAD)
    consts = [w_branch_attn.astype(BF16), w_out.astype(BF16),
              ffn_norm.reshape(1, d).astype(F32), wr_hi, wr_lo, br]
    row = lambda width: pl.BlockSpec((tm, width), lambda i: (i, 0))
    return pl.pallas_call(
        _post_body,
        grid=(n // tm,),
        in_specs=[row(ATTN_WIDTH), row(d), row(d), row(d)] + [_const_spec(c.shape) for c in consts],
        out_specs=[row(d), row(d), row(ROUTER_PAD), row(ROUTER_PAD),
                   pl.BlockSpec((1, SUBLANES, ROUTER_PAD), lambda i: (i, 0, 0)),
                   _const_spec((SUBLANES, ROUTER_PAD))],
        out_shape=[
            jax.ShapeDtypeStruct((n, d), F32),
            jax.ShapeDtypeStruct((n, d), F32),
            jax.ShapeDtypeStruct((n, ROUTER_PAD), jnp.int32),
            jax.ShapeDtypeStruct((n, ROUTER_PAD), F32),
            jax.ShapeDtypeStruct((n // tm, SUBLANES, ROUTER_PAD), F32),
            jax.ShapeDtypeStruct((SUBLANES, ROUTER_PAD), F32),
        ],
        scratch_shapes=[pltpu.VMEM((1, ROUTER_PAD), F32)],
        compiler_params=_params(1),
        name="post_attn",
    )(attn, pp, g1, xf, *consts)


def _block_copies(length, make_copy, max_block):
    block = max_block
    while block >= ROW_ALIGN:
        offset = pl.multiple_of(jnp.bitwise_and(length, -2 * block), ROW_ALIGN)
        yield jnp.bitwise_and(length, block) != 0, make_copy(offset, block)
        block //= 2


def _for_each_block(length, make_copy, max_block, action):
    for pred, cp in _block_copies(length, make_copy, max_block):
        @pl.when(pred)
        def _(cp=cp):
            action(cp)


def _staging_rows(tm):
    return TOP_K_EXPERTS * tm + N_EXPERTS * ROW_ALIGN


def _dispatch_body(run_dst_ref, run_len_ref, run_off_ref, tail_start_ref, tail_len_ref,
                   last_tile_ref, lpos_ref, h2_ref, xs_ref, buf_ref, zero_ref, sem, zsem,
                   *, first_tail_tile):
    i = pl.program_id(0)
    tm = h2_ref.shape[0]
    n_tiles = xs_ref.shape[0] // TM_EXP
    rows = buf_ref.shape[1]

    lpos_t = lpos_ref[...].T
    slot = lax.broadcasted_iota(jnp.int32, (rows, tm), 0)
    onehot = jnp.zeros((rows, tm), F32)
    for k in range(TOP_K_EXPERTS):
        onehot = onehot + jnp.where(slot == lpos_t[k:k + 1, :], 1.0, 0.0)
    def run_copy(tile, e):
        base = tile * N_EXPERTS + e
        half = lax.rem(tile, 2)
        src0 = pl.multiple_of(run_off_ref[base], ROW_ALIGN)
        dst0 = pl.multiple_of(run_dst_ref[base], ROW_ALIGN)
        return run_len_ref[base], lambda off, blk: pltpu.make_async_copy(
            buf_ref.at[half, pl.ds(src0 + off, blk)], xs_ref.at[pl.ds(dst0 + off, blk)],
            sem.at[half])

    def for_runs(tile, action):
        def body(e, carry):
            length, make = run_copy(tile, e)
            _for_each_block(length, make, tm, action)
            return carry
        lax.fori_loop(0, N_EXPERTS, body, 0)

    @pl.when(i >= 2)
    def _():
        for_runs(i - 2, lambda cp: cp.wait())

    buf_ref[lax.rem(i, 2)] = _dot(onehot.astype(BF16), h2_ref[...].astype(BF16))
    for_runs(i, lambda cp: cp.start())

    @pl.when(i == 0)
    def _():
        zero_ref[...] = jnp.zeros_like(zero_ref)

        def tail_copy(e):
            dst0 = pl.multiple_of(tail_start_ref[e], ROW_ALIGN)
            return tail_len_ref[e], lambda off, blk: pltpu.make_async_copy(
                zero_ref.at[pl.ds(0, blk)], xs_ref.at[pl.ds(dst0 + off, blk)], zsem)

        def fill_tail(e, carry):
            length, make = tail_copy(e)
            _for_each_block(length, make, TM_EXP // 2, lambda cp: cp.start())
            _for_each_block(length, make, TM_EXP // 2, lambda cp: cp.wait())
            return carry

        lax.fori_loop(0, N_EXPERTS, fill_tail, 0)

        def tail_tile(t, carry):
            @pl.when(t > last_tile_ref[0])
            def _():
                cp = pltpu.make_async_copy(
                    zero_ref, xs_ref.at[pl.ds(pl.multiple_of(t * TM_EXP, TM_EXP), TM_EXP)], zsem)
                cp.start()
                cp.wait()
            return carry

        lax.fori_loop(first_tail_tile, n_tiles, tail_tile, 0)

    @pl.when(i == pl.num_programs(0) - 1)
    def _():
        @pl.when(i >= 1)
        def _():
            for_runs(i - 1, lambda cp: cp.wait())
        for_runs(i, lambda cp: cp.wait())


def _dispatch(h2, lpos, plan, n_rows):
    n, d = h2.shape
    tm = min(TM_POST, n)
    assert n % tm == 0 and tm % ROW_ALIGN == 0
    return pl.pallas_call(
        functools.partial(_dispatch_body, first_tail_tile=(n * TOP_K_EXPERTS) // TM_EXP),
        grid_spec=pltpu.PrefetchScalarGridSpec(
            num_scalar_prefetch=6,
            grid=(n // tm,),
            in_specs=[
                pl.BlockSpec((tm, ROUTER_PAD), lambda i, *_: (i, 0)),
                pl.BlockSpec((tm, d), lambda i, *_: (i, 0)),
            ],
            out_specs=pl.BlockSpec(memory_space=pl.ANY),
            scratch_shapes=[pltpu.VMEM((2, _staging_rows(tm), d), F32),
                            pltpu.VMEM((TM_EXP, d), F32),
                            pltpu.SemaphoreType.DMA((2,)), pltpu.SemaphoreType.DMA],
        ),
        out_shape=jax.ShapeDtypeStruct((n_rows, d), F32),
        compiler_params=_params(1),
        name="dispatch",
    )(plan["run_dst"], plan["run_len"], plan["run_off"], plan["tail_start"], plan["tail_len"],
      plan["last_tile"], lpos, h2)


def _experts_body(tile_expert_ref, tile_rows_ref, run_start_ref, next_expert_ref,
                  xs_ref, w1_hbm, b1_ref, w2_hbm, b2_ref, y_ref,
                  w1f_ref, w2f_ref, w1b_ref, w2b_ref, wsem):
    i = pl.program_id(0)

    def fetch(expert):
        return (pltpu.make_async_copy(w1_hbm.at[expert], w1f_ref, wsem.at[0]),
                pltpu.make_async_copy(w2_hbm.at[expert], w2f_ref, wsem.at[1]))

    @pl.when(i == 0)
    def _():
        for cp in fetch(tile_expert_ref[0]):
            cp.start()

    @pl.when(run_start_ref[i] == 1)
    def _():
        for cp in fetch(tile_expert_ref[i]):
            cp.wait()
        w1b_ref[...] = w1f_ref[...].astype(BF16)
        w2b_ref[...] = w2f_ref[...].astype(BF16)

        @pl.when(next_expert_ref[i] >= 0)
        def _():
            for cp in fetch(next_expert_ref[i]):
                cp.start()

    @pl.when(tile_rows_ref[i] > 0)
    def _():
        x = xs_ref[...].astype(BF16)
        gu = _dot(x, w1b_ref[...]) + b1_ref[0]
        g = jnp.minimum(gu[:, :D_FF], SWIGLU_LIMIT)
        u = jnp.clip(gu[:, D_FF:], -SWIGLU_LIMIT, SWIGLU_LIMIT)
        act = g * jax.nn.sigmoid(SWIGLU_ALPHA * g) * (u + 1.0)
        y_ref[...] = _dot(act.astype(BF16), w2b_ref[...]) + b2_ref[0]

    @pl.when(tile_rows_ref[i] == 0)
    def _():
        y_ref[...] = jnp.zeros_like(y_ref)


def _experts(xs, tile_expert, tile_rows, run_start, next_expert, w1, b1, w2, b2):
    n_rows, d = xs.shape
    n_tiles = n_rows // TM_EXP
    tile = lambda i, *_: (i, 0)
    per_expert = lambda i, te, *_: (te[i], 0, 0)
    return pl.pallas_call(
        _experts_body,
        grid_spec=pltpu.PrefetchScalarGridSpec(
            num_scalar_prefetch=4,
            grid=(n_tiles,),
            in_specs=[
                pl.BlockSpec((TM_EXP, d), tile),
                pl.BlockSpec(memory_space=pl.ANY),
                pl.BlockSpec((1, 1, 2 * D_FF), per_expert),
                pl.BlockSpec(memory_space=pl.ANY),
                pl.BlockSpec((1, 1, d), per_expert),
            ],
            out_specs=pl.BlockSpec((TM_EXP, d), tile),
            scratch_shapes=[
                pltpu.VMEM((d, 2 * D_FF), F32), pltpu.VMEM((D_FF, d), F32),
                pltpu.VMEM((d, 2 * D_FF), BF16), pltpu.VMEM((D_FF, d), BF16),
                pltpu.SemaphoreType.DMA((2,)),
            ],
        ),
        out_shape=jax.ShapeDtypeStruct((n_rows, d), F32),
        compiler_params=_params(1),
        name="experts",
    )(tile_expert, tile_rows, run_start, next_expert, xs, w1,
      b1.reshape(N_EXPERTS, 1, 2 * D_FF), w2, b2.reshape(N_EXPERTS, 1, d))


def _combine_body(run_dst_ref, run_len_ref, run_off_ref, lpos_ref, gates_ref, x1_ref, fn_ref,
                  y_ref, o_ref, buf_ref, sem):
    i = pl.program_id(0)
    tm = x1_ref.shape[0]
    rows = buf_ref.shape[1]

    @pl.when(i == 0)
    def _():
        buf_ref[...] = jnp.zeros_like(buf_ref)

    def run_copy(tile, e):
        base = tile * N_EXPERTS + e
        half = lax.rem(tile, 2)
        src0 = pl.multiple_of(run_dst_ref[base], ROW_ALIGN)
        dst0 = pl.multiple_of(run_off_ref[base], ROW_ALIGN)
        return run_len_ref[base], lambda off, blk: pltpu.make_async_copy(
            y_ref.at[pl.ds(src0 + off, blk)], buf_ref.at[half, pl.ds(dst0 + off, blk)],
            sem.at[half])

    def for_runs(tile, action):
        def body(e, carry):
            length, make = run_copy(tile, e)
            _for_each_block(length, make, tm, action)
            return carry
        lax.fori_loop(0, N_EXPERTS, body, 0)

    @pl.when(i == 0)
    def _():
        for_runs(i, lambda cp: cp.start())

    @pl.when(i + 1 < pl.num_programs(0))
    def _():
        for_runs(i + 1, lambda cp: cp.start())

    for_runs(i, lambda cp: cp.wait())

    lpos = lpos_ref[...]
    gates = gates_ref[...]
    slot = lax.broadcasted_iota(jnp.int32, (tm, rows), 1)
    weights = jnp.zeros((tm, rows), F32)
    for k in range(TOP_K_EXPERTS):
        weights = weights + jnp.where(slot == lpos[:, k:k + 1], gates[:, k:k + 1], 0.0)
    out = x1_ref[...] + _dot(weights.astype(BF16), buf_ref[lax.rem(i, 2)].astype(BF16))
    o_ref[...] = _rmsnorm(out, fn_ref[...])


def _combine(y, lpos, gates, x1, final_norm, plan):
    n, d = x1.shape
    tm = min(TM_POST, n)
    assert n % tm == 0
    row = lambda width: pl.BlockSpec((tm, width), lambda i, *_: (i, 0))
    return pl.pallas_call(
        _combine_body,
        grid_spec=pltpu.PrefetchScalarGridSpec(
            num_scalar_prefetch=3,
            grid=(n // tm,),
            in_specs=[
                row(ROUTER_PAD), row(ROUTER_PAD), row(d),
                pl.BlockSpec((1, d), lambda i, *_: (0, 0)),
                pl.BlockSpec(memory_space=pl.ANY),
            ],
            out_specs=row(d),
            scratch_shapes=[pltpu.VMEM((2, _staging_rows(tm), d), F32),
                            pltpu.SemaphoreType.DMA((2,))],
        ),
        out_shape=jax.ShapeDtypeStruct((n, d), F32),
        compiler_params=_params(1),
        name="combine",
    )(plan["run_dst"], plan["run_len"], plan["run_off"], lpos, gates, x1,
      final_norm.reshape(1, d).astype(F32), y)


def _routing_plan(runs, counts, n_tiles):
    counts = counts[0, :N_EXPERTS].astype(jnp.int32)
    padded = ((counts + TM_EXP - 1) // TM_EXP) * TM_EXP
    ends = jnp.cumsum(padded)
    starts = ends - padded
    runs = runs[:, :, :N_EXPERTS].astype(jnp.int32)
    flat = lambda a: a.reshape(-1).astype(jnp.int32)
    tile_row0 = jnp.arange(n_tiles, dtype=jnp.int32) * TM_EXP
    tile_expert = jnp.minimum(jnp.sum(tile_row0[:, None] >= ends[None, :], axis=1),
                              N_EXPERTS - 1).astype(jnp.int32)
    tile_rows = jnp.clip(counts[tile_expert] - (tile_row0 - starts[tile_expert]), 0, TM_EXP)
    used = tile_row0 < ends[-1]
    tile_rows = jnp.where(used, tile_rows, 0).astype(jnp.int32)
    changed = jnp.concatenate([jnp.ones((1,), bool), tile_expert[1:] != tile_expert[:-1]])
    ids = jnp.where(counts > 0, jnp.arange(N_EXPERTS, dtype=jnp.int32), N_EXPERTS)
    later = jnp.concatenate([lax.cummin(ids, reverse=True)[1:],
                             jnp.full((1,), N_EXPERTS, jnp.int32)])
    return {
        "tile_expert": tile_expert,
        "tile_rows": tile_rows,
        "first_of_expert": (used & changed).astype(jnp.int32),
        "next_expert": jnp.where(later < N_EXPERTS, later, -1)[tile_expert].astype(jnp.int32),
        "last_tile": jnp.maximum(ends[-1] // TM_EXP - 1, 0).astype(jnp.int32).reshape(1),
        "tail_start": (starts + counts).astype(jnp.int32),
        "tail_len": (padded - counts).astype(jnp.int32),
        "run_dst": flat(starts[None, :] + runs[:, 0, :]),
        "run_len": flat(runs[:, 1, :]),
        "run_off": flat(runs[:, 2, :]),
    }


def kernel(x, mix_norm, w_in, pool_w, pool_scale, w_branch_pool, w_branch_attn, rel_bias, w_out,
           ffn_norm, w_router, b_router, w1, b1, w2, b2, final_norm):
    batch, seq, d = x.shape
    n = batch * seq
    depth = mix_norm.shape[0]
    assert depth == 1, "the combine kernel fuses the final norm, so only one layer is supported"
    token_tiles = n // min(TM_POST, n)
    max_rows = (n * TOP_K_EXPERTS + N_EXPERTS * token_tiles * (ROW_ALIGN - 1)
                + N_EXPERTS * (TM_EXP - 1))
    n_tiles = (max_rows + TM_EXP - 1) // TM_EXP
    xf = x.reshape(n, d)
    for l in range(depth):
        qT, k, vT, qiT, ki, wiT, pp, g1 = _inproj(
            xf, mix_norm[l], w_in[l], pool_w[l], pool_scale[l], w_branch_pool[l], seq)
        attn = _attention(qT, k, vT, qiT, ki, wiT, rel_bias, batch, seq)
        x1, h2, lpos, gates, runs, counts = _post_attn(
            attn, pp, g1, xf, w_branch_attn[l], w_out[l], ffn_norm[l], w_router[l], b_router[l])
        plan = _routing_plan(runs, counts, n_tiles)
        xs = _dispatch(h2, lpos, plan, n_tiles * TM_EXP)
        y = _experts(xs, plan["tile_expert"], plan["tile_rows"], plan["first_of_expert"],
                     plan["next_expert"], w1[l], b1[l], w2[l], b2[l])
        xf = _combine(y, lpos, gates, x1, final_norm, plan)
    return xf.reshape(batch, seq, d)
```

```python
import functools
import math

import jax
import jax.numpy as jnp
import numpy as np
from jax import lax
from jax.experimental import pallas as pl
from jax.experimental.pallas import tpu as pltpu

D_MODEL = 1024
POOL_WIDTH = 512
POOL_WINDOWS = (2, 4, 8, 16)
POOL_GROUPS = len(POOL_WINDOWS)
POOL_GROUP_WIDTH = POOL_WIDTH // POOL_GROUPS
N_HEADS = 8
HEAD_DIM = 64
ATTN_WIDTH = N_HEADS * HEAD_DIM
N_IDX_HEADS = 4
IDX_DIM = 64
IDX_SCALE = (IDX_DIM ** -0.5) * (N_IDX_HEADS ** -0.5)
ATTN_SCALE = HEAD_DIM ** -0.5
TOPK_MAX = 256
REL_BUCKETS = 32
REL_MAX_DIST = 128
N_BRANCHES = 2
N_EXPERTS = 32
TOP_K_EXPERTS = 4
D_FF = D_MODEL
SWIGLU_LIMIT = 7.0
SWIGLU_ALPHA = 1.702
RMS_EPS = 1e-5
SPLIT_SIZES = (POOL_WIDTH, ATTN_WIDTH, ATTN_WIDTH, ATTN_WIDTH,
               N_IDX_HEADS * IDX_DIM, IDX_DIM, N_IDX_HEADS, N_BRANCHES * D_MODEL)

LANES = 128
SUBLANES = 8
VMEM_LIMIT_BYTES = 56 * 1024 * 1024

TM_IN = 512
TQ = 256
TK = 256
TM_POST = 512
TM_EXP = 256
ROW_ALIGN = SUBLANES
PERM_CHUNK = 256
POOL_HALO = 16
N_BISECT_BF16 = 10
N_BISECT_F32 = 8
BF16_ROWS = 2 * SUBLANES
BF16_STEP = 2.0 ** -7
TINY = 1e-30
PAIR = 2 * HEAD_DIM
IDX_PAD = LANES
ROUTER_PAD = LANES

F32 = jnp.float32
BF16 = jnp.bfloat16
NEG_INF = float("-inf")
M_INIT = -1e30
LOG2E = math.log2(math.e)


def _dot(a, b):
    return jnp.dot(a, b, preferred_element_type=F32)


def _dot_nt(a, b):
    return lax.dot_general(a, b, (((1,), (1,)), ((), ())), preferred_element_type=F32)


def _rmsnorm(x, g):
    ms = jnp.mean(x * x, axis=-1, keepdims=True)
    return x * lax.rsqrt(ms + RMS_EPS) * g


def _const_spec(shape):
    nd = len(shape)
    return pl.BlockSpec(shape, lambda *_: (0,) * nd)


def _params(n_axes):
    return pltpu.CompilerParams(
        dimension_semantics=("arbitrary",) * n_axes,
        vmem_limit_bytes=VMEM_LIMIT_BYTES)


_ROW_SECTIONS = (("pool", POOL_WIDTH), ("k", ATTN_WIDTH), ("ki", IDX_PAD),
                 ("g0", D_MODEL), ("g1", D_MODEL))
_COL_SECTIONS = (("q", ATTN_WIDTH), ("v", ATTN_WIDTH), ("qi", N_IDX_HEADS * IDX_PAD),
                 ("wi", 2 * SUBLANES))


def _section(sections, name):
    start = 0
    for key, width in sections:
        if key == name:
            return slice(start, start + width)
        start += width
    raise KeyError(name)


def _inproj_body(x_ref, g_ref, wrow_ref, wcol_ref, poolw_ref, pscale_ref, wbp_ref,
                 qT_ref, k_ref, vT_ref, qiT_ref, ki_ref, wiT_ref, pp_ref, g1_ref,
                 halo_ref, *, tiles_per_seq):
    i = pl.program_id(0)
    tm = x_ref.shape[0]
    h = _rmsnorm(x_ref[...], g_ref[...]).astype(BF16)
    row_w = lambda name: wrow_ref[:, _section(_ROW_SECTIONS, name)]
    col_w = lambda name: wcol_ref[_section(_COL_SECTIONS, name), :]

    qT = (_dot_nt(col_w("q"), h) * (ATTN_SCALE * LOG2E)).astype(BF16)
    vT = _dot_nt(col_w("v"), h).astype(BF16)
    qiT = _dot_nt(col_w("qi"), h).astype(BF16)
    wiT = _dot_nt(col_w("wi"), h) * IDX_SCALE
    for j in range(tm // TQ):
        qT_ref[j] = qT[:, j * TQ:(j + 1) * TQ]
        qiT_ref[j] = qiT[:, j * TQ:(j + 1) * TQ]
        wiT_ref[j] = wiT[:SUBLANES, j * TQ:(j + 1) * TQ]
    for j in range(tm // TK):
        vT_ref[j] = vT[:, j * TK:(j + 1) * TK]
    k_ref[...] = _dot(h, row_w("k")).astype(BF16)
    ki_ref[...] = _dot(h, row_w("ki")).astype(BF16)
    g1_ref[...] = jax.nn.sigmoid(_dot(h, row_w("g1"))).astype(BF16)

    zp = _dot(h, row_w("pool"))
    seq_tile = lax.rem(i, tiles_per_seq)

    @pl.when(seq_tile == 0)
    def _():
        halo_ref[...] = jnp.zeros_like(halo_ref)

    zext = jnp.concatenate([halo_ref[...], zp], axis=0)
    halo_ref[...] = zp[tm - POOL_HALO:, :]
    gw = POOL_GROUP_WIDTH
    s2 = zext + pltpu.roll(zext, 1, 0)
    s4 = s2[:, gw:] + pltpu.roll(s2[:, gw:], 2, 0)
    s8 = s4[:, gw:] + pltpu.roll(s4[:, gw:], 4, 0)
    s16 = s8[:, gw:] + pltpu.roll(s8[:, gw:], 8, 0)
    wsum = (s2[POOL_HALO:, :gw], s4[POOL_HALO:, :gw], s8[POOL_HALO:, :gw], s16[POOL_HALO:, :])
    t = seq_tile * tm + lax.broadcasted_iota(jnp.int32, (tm, 1), 0)
    mixed = []
    for g, w in enumerate(POOL_WINDOWS):
        cnt = jnp.minimum(t + 1, w).astype(F32)
        pooled = wsum[g] / cnt - zp[:, g * gw:(g + 1) * gw]
        mixed.append(_dot(pooled.astype(BF16), poolw_ref[g]) * pscale_ref[:, g * gw:(g + 1) * gw])
    mixed = jnp.concatenate(mixed, axis=1).astype(BF16)
    y_pool = _dot(mixed, wbp_ref[...])
    gate0 = jax.nn.sigmoid(_dot(h, row_w("g0")))
    pp_ref[...] = (gate0 * y_pool).astype(BF16)


def _inproj(xf, mix_norm, w_in, pool_w, pool_scale, w_branch_pool, seq):
    n, d = xf.shape
    tm = min(TM_IN, seq)
    assert seq % tm == 0 and tm % TK == 0 and tm % TQ == 0 and n % tm == 0
    offs = [0] + [int(o) for o in np.cumsum(SPLIT_SIZES)]
    z_pool, z_q, z_k, z_v, z_qi, z_ki, z_wi, z_gate = (
        w_in[:, offs[j]:offs[j + 1]] for j in range(len(SPLIT_SIZES)))
    pad_cols = lambda a, width: jnp.pad(a, ((0, 0), (0, width - a.shape[1])))
    qi_heads = jnp.pad(z_qi.reshape(d, N_IDX_HEADS, IDX_DIM),
                       ((0, 0), (0, 0), (0, IDX_PAD - IDX_DIM))).reshape(d, N_IDX_HEADS * IDX_PAD)
    parts = {"pool": z_pool, "k": z_k, "ki": pad_cols(z_ki, IDX_PAD),
             "g0": z_gate[:, :D_MODEL], "g1": z_gate[:, D_MODEL:],
             "q": z_q, "v": z_v, "qi": qi_heads, "wi": pad_cols(z_wi, 2 * SUBLANES)}
    w_row = jnp.concatenate([parts[k] for k, _ in _ROW_SECTIONS], axis=1).astype(BF16)
    w_col = jnp.concatenate([parts[k] for k, _ in _COL_SECTIONS], axis=1).astype(BF16).T
    consts = [mix_norm.reshape(1, d).astype(F32), w_row, w_col, pool_w.astype(BF16),
              pool_scale.reshape(1, POOL_WIDTH).astype(F32), w_branch_pool.astype(BF16)]
    grid = (n // tm,)
    row = lambda width: pl.BlockSpec((tm, width), lambda i: (i, 0))
    tiles = lambda t, height: pl.BlockSpec((tm // t, height, t), lambda i: (i, 0, 0))
    out_shape = [
        jax.ShapeDtypeStruct((n // TQ, ATTN_WIDTH, TQ), BF16),
        jax.ShapeDtypeStruct((n, ATTN_WIDTH), BF16),
        jax.ShapeDtypeStruct((n // TK, ATTN_WIDTH, TK), BF16),
        jax.ShapeDtypeStruct((n // TQ, N_IDX_HEADS * IDX_PAD, TQ), BF16),
        jax.ShapeDtypeStruct((n, IDX_PAD), BF16),
        jax.ShapeDtypeStruct((n // TQ, SUBLANES, TQ), F32),
        jax.ShapeDtypeStruct((n, D_MODEL), BF16),
        jax.ShapeDtypeStruct((n, D_MODEL), BF16),
    ]
    out_specs = [
        tiles(TQ, ATTN_WIDTH),
        row(ATTN_WIDTH),
        tiles(TK, ATTN_WIDTH),
        tiles(TQ, N_IDX_HEADS * IDX_PAD),
        row(IDX_PAD),
        tiles(TQ, SUBLANES),
        row(D_MODEL),
        row(D_MODEL),
    ]
    return pl.pallas_call(
        functools.partial(_inproj_body, tiles_per_seq=seq // tm),
        grid=grid,
        in_specs=[row(d)] + [_const_spec(c.shape) for c in consts],
        out_specs=out_specs,
        out_shape=out_shape,
        scratch_shapes=[pltpu.VMEM((POOL_HALO, POOL_WIDTH), F32)],
        compiler_params=_params(1),
        name="inproj",
    )(xf, *consts)


def _rel_thresholds():
    n = np.arange(0, 4 * REL_MAX_DIST)
    max_exact = REL_BUCKETS // 2
    nf = np.maximum(n, 1).astype(np.float32)
    large = max_exact + (np.log(nf / np.float32(max_exact))
                         / np.float32(math.log(REL_MAX_DIST / max_exact))
                         * np.float32(REL_BUCKETS - max_exact)).astype(np.int32)
    bucket = np.where(n < max_exact, n, np.minimum(large, REL_BUCKETS - 1))
    assert np.all(np.diff(bucket) >= 0) and np.all(np.diff(bucket) <= 1)
    assert bucket[-1] == REL_BUCKETS - 1
    return [int(np.argmax(bucket >= b)) for b in range(1, REL_BUCKETS)]


def _attn_body(table_ref, qT_ref, qiT_ref, wiT_ref, k_ref, ki_ref, vT_ref, o_ref,
               score_ref, sb_ref, band_ref, tri_ref, qm_ref, m_ref, l_ref, acc_ref, thr_ref,
               need_ref,
               *, topk):
    b = pl.program_id(0)
    qi = pl.program_id(1)
    tq = o_ref.shape[0]
    nk = qi + 1
    n_keys = score_ref.shape[0] * TK
    key_i = lax.broadcasted_iota(jnp.int32, (TK, tq), 0)
    qry_i = lax.broadcasted_iota(jnp.int32, (TK, tq), 1)

    @pl.when((b == 0) & (qi == 0))
    def _():
        r_i = lax.broadcasted_iota(jnp.int32, (TK, TK), 0)
        c_i = lax.broadcasted_iota(jnp.int32, (TK, TK), 1)
        tri_ref[...] = jnp.where(c_i < r_i, 1.0, 0.0).astype(BF16)
        thresholds = _rel_thresholds()
        assert thresholds[-1] <= TK
        for part in range(2):
            dist = qry_i - key_i + (1 - part) * TK
            for h in range(N_HEADS):
                bias = jnp.full((TK, tq), table_ref[h], F32)
                for bkt, thr in enumerate(thresholds, start=1):
                    bias = jnp.where(dist >= thr, table_ref[bkt * N_HEADS + h], bias)
                band_ref[h, part] = jnp.where(dist < 0, NEG_INF, bias * LOG2E)

    first_half = lax.broadcasted_iota(jnp.int32, (PAIR, tq), 0) < HEAD_DIM
    for hp in range(N_HEADS // 2):
        qp = qT_ref[0, hp * PAIR:(hp + 1) * PAIR, :]
        zero = jnp.zeros_like(qp)
        qm_ref[2 * hp] = jnp.where(first_half, qp, zero)
        qm_ref[2 * hp + 1] = jnp.where(first_half, zero, qp)

    wiT = wiT_ref[0]

    def score_tile(kj, carry):
        mx, mn = carry
        ki_t = ki_ref[pl.ds(pl.multiple_of(kj * TK, TK), TK), :]
        heads = [_dot(ki_t, qiT_ref[0, h * IDX_PAD:(h + 1) * IDX_PAD, :])
                 for h in range(N_IDX_HEADS)]
        sc = jnp.maximum(heads[0], 0.0) * wiT[0:1, :]
        for h in range(1, N_IDX_HEADS):
            sc = sc + jnp.maximum(heads[h], 0.0) * wiT[h:h + 1, :]
        causal = (kj * TK + key_i) <= (qi * tq + qry_i)
        masked = jnp.where(causal, sc, NEG_INF)
        score_ref[kj] = masked
        sb_ref[kj] = masked.astype(BF16)
        mx = jnp.maximum(mx, jnp.max(sc, axis=0, keepdims=True))
        mn = jnp.minimum(mn, jnp.min(sc, axis=0, keepdims=True))
        return mx, mn

    row_max, row_min = lax.fori_loop(
        0, nk, score_tile,
        (jnp.full((1, tq), NEG_INF, F32), jnp.full((1, tq), -NEG_INF, F32)))

    def fold(x, op):
        return op(x.reshape(TK // SUBLANES, SUBLANES, tq), axis=0)

    def count_where(pred):
        def body(kj, acc):
            return acc + fold(jnp.where(pred(score_ref[kj], kj), 1.0, 0.0), jnp.sum)
        acc = lax.fori_loop(0, nk, body, jnp.zeros((SUBLANES, tq), F32))
        return jnp.sum(acc, axis=0, keepdims=True)

    def max_where(pred):
        def body(kj, acc):
            s = score_ref[kj]
            return jnp.maximum(acc, fold(jnp.where(pred(s, kj), s, NEG_INF), jnp.max))
        acc = lax.fori_loop(0, nk, body, jnp.full((SUBLANES, tq), NEG_INF, F32))
        return jnp.max(acc, axis=0, keepdims=True)

    kf = float(topk)
    thr_ref[...] = jnp.full((1, tq), NEG_INF, F32)
    need_ref[...] = jnp.full((1, tq), float(n_keys), F32)

    @pl.when(qi * tq + 1 > topk)
    def _():
        def count_above_bf16(mid_b):
            mid_t = jnp.broadcast_to(mid_b, (TK, tq))
            one, zero = jnp.ones((), BF16), jnp.zeros((), BF16)

            def body(kj, acc):
                m = jnp.where(sb_ref[kj] > mid_t, one, zero)
                parts = [m[r * BF16_ROWS:(r + 1) * BF16_ROWS] for r in range(TK // BF16_ROWS)]
                while len(parts) > 1:
                    parts = [a + b for a, b in zip(parts[::2], parts[1::2])]
                return acc + parts[0].astype(F32)
            acc = lax.fori_loop(0, nk, body, jnp.zeros((BF16_ROWS, tq), F32))
            return jnp.sum(acc, axis=0, keepdims=True)

        def widen(v, sign):
            return v + sign * (jnp.abs(v) * BF16_STEP + TINY)

        def bisect_bf16(_, carry):
            lo, hi = carry
            mid_b = (0.5 * (lo + hi)).astype(BF16)
            above = count_above_bf16(mid_b) >= kf
            mid = mid_b.astype(F32)
            return jnp.where(above, mid, lo), jnp.where(above, hi, mid)

        lo, hi = lax.fori_loop(0, N_BISECT_BF16, bisect_bf16,
                               (widen(row_min, -1.0), widen(row_max, 1.0)))

        def bisect(_, carry):
            lo, hi = carry
            mid = 0.5 * (lo + hi)
            above = count_where(lambda s, kj: s > mid) >= kf
            return jnp.where(above, mid, lo), jnp.where(above, hi, mid)

        _, hi = lax.fori_loop(0, N_BISECT_F32, bisect, (widen(lo, -1.0), widen(hi, 1.0)))
        cand = max_where(lambda s, kj: s <= hi)
        n_ge = count_where(lambda s, kj: s >= cand)

        def unresolved(state):
            it, _, n_ge = state
            return (jnp.min(n_ge) < kf) & (it < n_keys)

        def step(state):
            it, cand, n_ge = state
            nxt = max_where(lambda s, kj: s < cand)
            n_nxt = count_where(lambda s, kj: s >= nxt)
            open_ = n_ge < kf
            return it + 1, jnp.where(open_, nxt, cand), jnp.where(open_, n_nxt, n_ge)

        _, thr, n_ge = lax.while_loop(unresolved, step, (jnp.int32(0), cand, n_ge))
        thr_ref[...] = thr

        @pl.when(jnp.max(n_ge) > kf)
        def _():
            need_ref[...] = kf - count_where(lambda s, kj: s > thr)

    m_ref[...] = jnp.full(m_ref.shape, M_INIT, F32)
    l_ref[...] = jnp.zeros(l_ref.shape, F32)
    acc_ref[...] = jnp.zeros(acc_ref.shape, F32)
    thr = thr_ref[...]
    need = need_ref[...]

    far_bias = [table_ref[(REL_BUCKETS - 1) * N_HEADS + h] * LOG2E for h in range(N_HEADS)]

    def attend(kj, ties_before, near):
        sc = score_ref[kj]
        tied = jnp.where(sc == thr, 1.0, 0.0)
        rank = _dot(tri_ref[...], tied.astype(BF16)) + ties_before
        ties_before = rank[TK - 1:, :] + tied[TK - 1:, :]
        sel_bias = jnp.where(
            sc > thr, 0.0,
            jnp.where(sc == thr, jnp.where(rank < need, 0.0, NEG_INF), NEG_INF))
        k_t = k_ref[pl.ds(pl.multiple_of(kj * TK, TK), TK), :]
        vT_t = vT_ref[kj]
        logits = [_dot(k_t[:, (h // 2) * PAIR:(h // 2 + 1) * PAIR], qm_ref[h])
                  for h in range(N_HEADS)]
        probs, alphas = [], []
        for h in range(N_HEADS):
            m_old = m_ref[h]
            if near:
                s = logits[h] + band_ref[h, kj - qi + 1] + sel_bias
                m_new = jnp.maximum(m_old, jnp.max(s, axis=0, keepdims=True))
                p = jnp.exp2(s - m_new)
            else:
                s = logits[h] + sel_bias
                m_new = jnp.maximum(m_old, jnp.max(s, axis=0, keepdims=True) + far_bias[h])
                p = jnp.exp2(s - (m_new - far_bias[h]))
            alpha = jnp.exp2(m_old - m_new)
            l_ref[h] = alpha * l_ref[h] + jnp.sum(p, axis=0, keepdims=True)
            m_ref[h] = m_new
            probs.append(p.astype(BF16))
            alphas.append(alpha)
        for hp in range(N_HEADS // 2):
            vTp = vT_t[hp * PAIR:(hp + 1) * PAIR, :]
            outs = [_dot(vTp, probs[2 * hp + e]) for e in range(2)]
            acc_ref[hp] = (acc_ref[hp] * jnp.where(first_half, alphas[2 * hp], alphas[2 * hp + 1])
                           + jnp.where(first_half, outs[0], outs[1]))
        return ties_before

    n_far = jnp.maximum(qi - 1, 0)
    ties = lax.fori_loop(0, n_far, functools.partial(attend, near=False),
                         jnp.zeros((1, tq), F32))
    lax.fori_loop(n_far, nk, functools.partial(attend, near=True), ties)
    for hp in range(N_HEADS // 2):
        denom = jnp.where(first_half, l_ref[2 * hp], l_ref[2 * hp + 1])
        o_ref[:, hp * PAIR:(hp + 1) * PAIR] = (acc_ref[hp] / denom).T.astype(o_ref.dtype)


def _attention(qT, k, vT, qiT, ki, wiT, rel_bias, batch, seq):
    n = k.shape[0]
    tq = min(TQ, seq)
    assert tq == TQ == TK and seq % tq == 0
    topk = min(TOPK_MAX, seq // 4)
    assert topk == tq or seq == tq
    nq = seq // tq
    nkt = seq // TK
    table = rel_bias.astype(F32).reshape(REL_BUCKETS * N_HEADS)
    qtile = lambda height: pl.BlockSpec((1, height, tq), lambda b, i: (b * nq + i, 0, 0))
    return pl.pallas_call(
        functools.partial(_attn_body, topk=topk),
        grid=(batch, nq),
        in_specs=[
            pl.BlockSpec(memory_space=pltpu.SMEM),
            qtile(ATTN_WIDTH),
            qtile(N_IDX_HEADS * IDX_PAD),
            qtile(SUBLANES),
            pl.BlockSpec((seq, ATTN_WIDTH), lambda b, i: (b, 0)),
            pl.BlockSpec((seq, IDX_PAD), lambda b, i: (b, 0)),
            pl.BlockSpec((nkt, ATTN_WIDTH, TK), lambda b, i: (b, 0, 0)),
        ],
        out_specs=pl.BlockSpec((tq, ATTN_WIDTH), lambda b, i: (b * nq + i, 0)),
        out_shape=jax.ShapeDtypeStruct((n, ATTN_WIDTH), BF16),
        scratch_shapes=[
            pltpu.VMEM((nkt, TK, tq), F32),
            pltpu.VMEM((nkt, TK, tq), BF16),
            pltpu.VMEM((N_HEADS, 2, TK, tq), F32),
            pltpu.VMEM((TK, TK), BF16),
            pltpu.VMEM((N_HEADS, PAIR, tq), BF16),
            pltpu.VMEM((N_HEADS, 1, tq), F32),
            pltpu.VMEM((N_HEADS, 1, tq), F32),
            pltpu.VMEM((N_HEADS // 2, PAIR, tq), F32),
            pltpu.VMEM((1, tq), F32),
            pltpu.VMEM((1, tq), F32),
        ],
        compiler_params=_params(2),
        name="attention",
    )(table, qT, qiT, wiT, k, ki, vT)


def _post_body(attn_ref, pp_ref, g1_ref, x_ref, wba_ref, wout_ref, fg_ref, wr_hi_ref, wr_lo_ref,
               br_ref, x1_ref, h2_ref, lpos_ref, gates_ref, runs_ref, counts_ref, carry_ref):
    i = pl.program_id(0)
    tm = x_ref.shape[0]
    y_attn = _dot(attn_ref[...], wba_ref[...])
    merged = pp_ref[...].astype(F32) + g1_ref[...].astype(F32) * y_attn
    x1 = x_ref[...] + _dot(merged.astype(BF16), wout_ref[...])
    x1_ref[...] = x1
    h2 = _rmsnorm(x1, fg_ref[...])
    h2_ref[...] = h2

    h_hi = h2.astype(BF16)
    h_lo = (h2 - h_hi.astype(F32)).astype(BF16)
    logits = (_dot(h_hi, wr_hi_ref[...]) + _dot(h_hi, wr_lo_ref[...])
              + _dot(h_lo, wr_hi_ref[...]) + br_ref[...])

    lane = lax.broadcasted_iota(jnp.int32, (tm, ROUTER_PAD), 1)
    work = logits
    vals, idxs = [], []
    for _ in range(TOP_K_EXPERTS):
        mx = jnp.max(work, axis=1, keepdims=True)
        ix = jnp.min(jnp.where(work == mx, lane, ROUTER_PAD), axis=1, keepdims=True)
        vals.append(mx)
        idxs.append(ix)
        work = jnp.where(lane == ix, NEG_INF, work)
    exps = [jnp.exp(v - vals[0]) for v in vals]
    denom = exps[0] + exps[1] + exps[2] + exps[3]

    member = jnp.zeros((tm, ROUTER_PAD), F32)
    for ix in idxs:
        member = member + jnp.where(lane == ix, 1.0, 0.0)

    @pl.when(i == 0)
    def _():
        carry_ref[...] = jnp.zeros_like(carry_ref)

    r_i = lax.broadcasted_iota(jnp.int32, (tm, tm), 0)
    c_i = lax.broadcasted_iota(jnp.int32, (tm, tm), 1)
    strict_lower = jnp.where(c_i < r_i, 1.0, 0.0).astype(BF16)
    local = _dot(strict_lower, member.astype(BF16))
    count = jnp.sum(member, axis=0, keepdims=True)
    run_len = jnp.floor((count + (ROW_ALIGN - 1)) * (1.0 / ROW_ALIGN)) * ROW_ALIGN
    e_r = lax.broadcasted_iota(jnp.int32, (ROUTER_PAD, ROUTER_PAD), 0)
    e_c = lax.broadcasted_iota(jnp.int32, (ROUTER_PAD, ROUTER_PAD), 1)
    strict_upper = jnp.where(e_r < e_c, 1.0, 0.0).astype(BF16)
    run_off = _dot(jnp.broadcast_to(run_len, (SUBLANES, ROUTER_PAD)).astype(BF16),
                   strict_upper)[:1, :]
    run_start = carry_ref[...]
    carry_new = run_start + run_len
    carry_ref[...] = carry_new
    counts_ref[...] = jnp.broadcast_to(carry_new, counts_ref.shape)
    sub = lax.broadcasted_iota(jnp.int32, (SUBLANES, ROUTER_PAD), 0)
    runs_ref[0] = jnp.where(sub == 0, run_start, jnp.where(sub == 1, run_len,
                                                          jnp.where(sub == 2, run_off, 0.0)))

    slot = run_off + local
    lpos = jnp.zeros((tm, ROUTER_PAD), jnp.int32)
    gates = jnp.zeros((tm, ROUTER_PAD), F32)
    for k in range(TOP_K_EXPERTS):
        pos = jnp.sum(jnp.where(lane == idxs[k], slot, 0.0), axis=1, keepdims=True)
        lpos = jnp.where(lane == k, pos.astype(jnp.int32), lpos)
        gates = jnp.where(lane == k, exps[k] / denom, gates)
    lpos_ref[...] = lpos
    gates_ref[...] = gates


def _post_attn(attn, pp, g1, xf, w_branch_attn, w_out, ffn_norm, w_router, b_router):
    n, d = xf.shape
    tm = min(TM_POST, n)
    assert n % tm == 0
    wr = jnp.pad(w_router.astype(F32), ((0, 0), (0, ROUTER_PAD - N_EXPERTS)))
    wr_hi = wr.astype(BF16)
    wr_lo = (wr - wr_hi.astype(F32)).astype(BF16)
    br = jnp.pad(b_router.astype(F32), (0, ROUTER_PAD - N_EXPERTS),
                 constant_values=NEG_INF).reshape(1, ROUTER_PAD)
    consts = [w_branch_attn.astype(BF16), w_out.astype(BF16),
              ffn_norm.reshape(1, d).astype(F32), wr_hi, wr_lo, br]
    row = lambda width: pl.BlockSpec((tm, width), lambda i: (i, 0))
    return pl.pallas_call(
        _post_body,
        grid=(n // tm,),
        in_specs=[row(ATTN_WIDTH), row(d), row(d), row(d)] + [_const_spec(c.shape) for c in consts],
        out_specs=[row(d), row(d), row(ROUTER_PAD), row(ROUTER_PAD),
                   pl.BlockSpec((1, SUBLANES, ROUTER_PAD), lambda i: (i, 0, 0)),
                   _const_spec((SUBLANES, ROUTER_PAD))],
        out_shape=[
            jax.ShapeDtypeStruct((n, d), F32),
            jax.ShapeDtypeStruct((n, d), F32),
            jax.ShapeDtypeStruct((n, ROUTER_PAD), jnp.int32),
            jax.ShapeDtypeStruct((n, ROUTER_PAD), F32),
            jax.ShapeDtypeStruct((n // tm, SUBLANES, ROUTER_PAD), F32),
            jax.ShapeDtypeStruct((SUBLANES, ROUTER_PAD), F32),
        ],
        scratch_shapes=[pltpu.VMEM((1, ROUTER_PAD), F32)],
        compiler_params=_params(1),
        name="post_attn",
    )(attn, pp, g1, xf, *consts)


def _block_copies(length, make_copy, max_block):
    block = max_block
    while block >= ROW_ALIGN:
        offset = pl.multiple_of(jnp.bitwise_and(length, -2 * block), ROW_ALIGN)
        yield jnp.bitwise_and(length, block) != 0, make_copy(offset, block)
        block //= 2


def _for_each_block(length, make_copy, max_block, action):
    for pred, cp in _block_copies(length, make_copy, max_block):
        @pl.when(pred)
        def _(cp=cp):
            action(cp)


def _staging_rows(tm):
    return TOP_K_EXPERTS * tm + N_EXPERTS * ROW_ALIGN


def _dispatch_body(run_dst_ref, run_len_ref, run_off_ref, tail_start_ref, tail_len_ref,
                   last_tile_ref, lpos_ref, h2_ref, xs_ref, buf_ref, zero_ref, sem, zsem,
                   *, first_tail_tile):
    i = pl.program_id(0)
    tm = h2_ref.shape[0]
    n_tiles = xs_ref.shape[0] // TM_EXP
    rows = buf_ref.shape[1]

    lpos_t = lpos_ref[...].T
    tokens = h2_ref[...].astype(BF16)
    half_now = lax.rem(i, 2)

    def permute_chunk(c, carry):
        r0 = pl.multiple_of(c * PERM_CHUNK, PERM_CHUNK)
        slot = r0 + lax.broadcasted_iota(jnp.int32, (PERM_CHUNK, tm), 0)
        onehot = jnp.where(slot == lpos_t[0:1, :], 1.0, 0.0)
        for k in range(1, TOP_K_EXPERTS):
            onehot = onehot + jnp.where(slot == lpos_t[k:k + 1, :], 1.0, 0.0)
        buf_ref[half_now, pl.ds(r0, PERM_CHUNK), :] = _dot(onehot.astype(BF16), tokens)
        return carry

    def run_copy(tile, e):
        base = tile * N_EXPERTS + e
        half = lax.rem(tile, 2)
        src0 = pl.multiple_of(run_off_ref[base], ROW_ALIGN)
        dst0 = pl.multiple_of(run_dst_ref[base], ROW_ALIGN)
        return run_len_ref[base], lambda off, blk: pltpu.make_async_copy(
            buf_ref.at[half, pl.ds(src0 + off, blk)], xs_ref.at[pl.ds(dst0 + off, blk)],
            sem.at[half])

    def for_runs(tile, action):
        def body(e, carry):
            length, make = run_copy(tile, e)
            _for_each_block(length, make, tm, action)
            return carry
        lax.fori_loop(0, N_EXPERTS, body, 0)

    @pl.when(i >= 2)
    def _():
        for_runs(i - 2, lambda cp: cp.wait())

    lax.fori_loop(0, rows // PERM_CHUNK, permute_chunk, 0)
    for_runs(i, lambda cp: cp.start())

    @pl.when(i == 0)
    def _():
        zero_ref[...] = jnp.zeros_like(zero_ref)

        def tail_copy(e):
            dst0 = pl.multiple_of(tail_start_ref[e], ROW_ALIGN)
            return tail_len_ref[e], lambda off, blk: pltpu.make_async_copy(
                zero_ref.at[pl.ds(0, blk)], xs_ref.at[pl.ds(dst0 + off, blk)], zsem)

        def fill_tail(e, carry):
            length, make = tail_copy(e)
            _for_each_block(length, make, TM_EXP // 2, lambda cp: cp.start())
            _for_each_block(length, make, TM_EXP // 2, lambda cp: cp.wait())
            return carry

        lax.fori_loop(0, N_EXPERTS, fill_tail, 0)

        def tail_tile(t, carry):
            @pl.when(t > last_tile_ref[0])
            def _():
                cp = pltpu.make_async_copy(
                    zero_ref, xs_ref.at[pl.ds(pl.multiple_of(t * TM_EXP, TM_EXP), TM_EXP)], zsem)
                cp.start()
                cp.wait()
            return carry

        lax.fori_loop(first_tail_tile, n_tiles, tail_tile, 0)

    @pl.when(i == pl.num_programs(0) - 1)
    def _():
        @pl.when(i >= 1)
        def _():
            for_runs(i - 1, lambda cp: cp.wait())
        for_runs(i, lambda cp: cp.wait())


def _dispatch(h2, lpos, plan, n_rows):
    n, d = h2.shape
    tm = min(TM_POST, n)
    assert n % tm == 0 and tm % ROW_ALIGN == 0 and _staging_rows(tm) % PERM_CHUNK == 0
    return pl.pallas_call(
        functools.partial(_dispatch_body, first_tail_tile=(n * TOP_K_EXPERTS) // TM_EXP),
        grid_spec=pltpu.PrefetchScalarGridSpec(
            num_scalar_prefetch=6,
            grid=(n // tm,),
            in_specs=[
                pl.BlockSpec((tm, ROUTER_PAD), lambda i, *_: (i, 0)),
                pl.BlockSpec((tm, d), lambda i, *_: (i, 0)),
            ],
            out_specs=pl.BlockSpec(memory_space=pl.ANY),
            scratch_shapes=[pltpu.VMEM((2, _staging_rows(tm), d), F32),
                            pltpu.VMEM((TM_EXP, d), F32),
                            pltpu.SemaphoreType.DMA((2,)), pltpu.SemaphoreType.DMA],
        ),
        out_shape=jax.ShapeDtypeStruct((n_rows, d), F32),
        compiler_params=_params(1),
        name="dispatch",
    )(plan["run_dst"], plan["run_len"], plan["run_off"], plan["tail_start"], plan["tail_len"],
      plan["last_tile"], lpos, h2)


def _experts_body(tile_expert_ref, tile_rows_ref, run_start_ref, next_expert_ref,
                  xs_ref, w1_hbm, b1_ref, w2_hbm, b2_ref, y_ref,
                  w1f_ref, w2f_ref, w1b_ref, w2b_ref, wsem):
    i = pl.program_id(0)

    def fetch(expert):
        return (pltpu.make_async_copy(w1_hbm.at[expert], w1f_ref, wsem.at[0]),
                pltpu.make_async_copy(w2_hbm.at[expert], w2f_ref, wsem.at[1]))

    @pl.when(i == 0)
    def _():
        for cp in fetch(tile_expert_ref[0]):
            cp.start()

    @pl.when(run_start_ref[i] == 1)
    def _():
        for cp in fetch(tile_expert_ref[i]):
            cp.wait()
        w1b_ref[...] = w1f_ref[...].astype(BF16)
        w2b_ref[...] = w2f_ref[...].astype(BF16)

        @pl.when(next_expert_ref[i] >= 0)
        def _():
            for cp in fetch(next_expert_ref[i]):
                cp.start()

    @pl.when(tile_rows_ref[i] > 0)
    def _():
        x = xs_ref[...].astype(BF16)
        gu = _dot(x, w1b_ref[...]) + b1_ref[0]
        g = jnp.minimum(gu[:, :D_FF], SWIGLU_LIMIT)
        u = jnp.clip(gu[:, D_FF:], -SWIGLU_LIMIT, SWIGLU_LIMIT)
        act = g * jax.nn.sigmoid(SWIGLU_ALPHA * g) * (u + 1.0)
        y_ref[...] = _dot(act.astype(BF16), w2b_ref[...]) + b2_ref[0]

    @pl.when(tile_rows_ref[i] == 0)
    def _():
        y_ref[...] = jnp.zeros_like(y_ref)


def _experts(xs, tile_expert, tile_rows, run_start, next_expert, w1, b1, w2, b2):
    n_rows, d = xs.shape
    n_tiles = n_rows // TM_EXP
    tile = lambda i, *_: (i, 0)
    per_expert = lambda i, te, *_: (te[i], 0, 0)
    return pl.pallas_call(
        _experts_body,
        grid_spec=pltpu.PrefetchScalarGridSpec(
            num_scalar_prefetch=4,
            grid=(n_tiles,),
            in_specs=[
                pl.BlockSpec((TM_EXP, d), tile),
                pl.BlockSpec(memory_space=pl.ANY),
                pl.BlockSpec((1, 1, 2 * D_FF), per_expert),
                pl.BlockSpec(memory_space=pl.ANY),
                pl.BlockSpec((1, 1, d), per_expert),
            ],
            out_specs=pl.BlockSpec((TM_EXP, d), tile),
            scratch_shapes=[
                pltpu.VMEM((d, 2 * D_FF), F32), pltpu.VMEM((D_FF, d), F32),
                pltpu.VMEM((d, 2 * D_FF), BF16), pltpu.VMEM((D_FF, d), BF16),
                pltpu.SemaphoreType.DMA((2,)),
            ],
        ),
        out_shape=jax.ShapeDtypeStruct((n_rows, d), F32),
        compiler_params=_params(1),
        name="experts",
    )(tile_expert, tile_rows, run_start, next_expert, xs, w1,
      b1.reshape(N_EXPERTS, 1, 2 * D_FF), w2, b2.reshape(N_EXPERTS, 1, d))


def _combine_body(run_dst_ref, run_len_ref, run_off_ref, lpos_ref, gates_ref, x1_ref, fn_ref,
                  y_ref, o_ref, buf_ref, sem):
    i = pl.program_id(0)
    tm = x1_ref.shape[0]
    rows = buf_ref.shape[1]

    @pl.when(i == 0)
    def _():
        buf_ref[...] = jnp.zeros_like(buf_ref)

    def run_copy(tile, e):
        base = tile * N_EXPERTS + e
        half = lax.rem(tile, 2)
        src0 = pl.multiple_of(run_dst_ref[base], ROW_ALIGN)
        dst0 = pl.multiple_of(run_off_ref[base], ROW_ALIGN)
        return run_len_ref[base], lambda off, blk: pltpu.make_async_copy(
            y_ref.at[pl.ds(src0 + off, blk)], buf_ref.at[half, pl.ds(dst0 + off, blk)],
            sem.at[half])

    def for_runs(tile, action):
        def body(e, carry):
            length, make = run_copy(tile, e)
            _for_each_block(length, make, tm, action)
            return carry
        lax.fori_loop(0, N_EXPERTS, body, 0)

    @pl.when(i == 0)
    def _():
        for_runs(i, lambda cp: cp.start())

    @pl.when(i + 1 < pl.num_programs(0))
    def _():
        for_runs(i + 1, lambda cp: cp.start())

    for_runs(i, lambda cp: cp.wait())

    lpos = lpos_ref[...]
    gates = gates_ref[...]
    slot = lax.broadcasted_iota(jnp.int32, (tm, rows), 1)
    weights = jnp.zeros((tm, rows), F32)
    for k in range(TOP_K_EXPERTS):
        weights = weights + jnp.where(slot == lpos[:, k:k + 1], gates[:, k:k + 1], 0.0)
    out = x1_ref[...] + _dot(weights.astype(BF16), buf_ref[lax.rem(i, 2)].astype(BF16))
    o_ref[...] = _rmsnorm(out, fn_ref[...])


def _combine(y, lpos, gates, x1, final_norm, plan):
    n, d = x1.shape
    tm = min(TM_POST, n)
    assert n % tm == 0
    row = lambda width: pl.BlockSpec((tm, width), lambda i, *_: (i, 0))
    return pl.pallas_call(
        _combine_body,
        grid_spec=pltpu.PrefetchScalarGridSpec(
            num_scalar_prefetch=3,
            grid=(n // tm,),
            in_specs=[
                row(ROUTER_PAD), row(ROUTER_PAD), row(d),
                pl.BlockSpec((1, d), lambda i, *_: (0, 0)),
                pl.BlockSpec(memory_space=pl.ANY),
            ],
            out_specs=row(d),
            scratch_shapes=[pltpu.VMEM((2, _staging_rows(tm), d), F32),
                            pltpu.SemaphoreType.DMA((2,))],
        ),
        out_shape=jax.ShapeDtypeStruct((n, d), F32),
        compiler_params=_params(1),
        name="combine",
    )(plan["run_dst"], plan["run_len"], plan["run_off"], lpos, gates, x1,
      final_norm.reshape(1, d).astype(F32), y)


def _routing_plan(runs, counts, n_tiles):
    counts = counts[0, :N_EXPERTS].astype(jnp.int32)
    padded = ((counts + TM_EXP - 1) // TM_EXP) * TM_EXP
    ends = jnp.cumsum(padded)
    starts = ends - padded
    runs = runs[:, :, :N_EXPERTS].astype(jnp.int32)
    flat = lambda a: a.reshape(-1).astype(jnp.int32)
    tile_row0 = jnp.arange(n_tiles, dtype=jnp.int32) * TM_EXP
    tile_expert = jnp.minimum(jnp.sum(tile_row0[:, None] >= ends[None, :], axis=1),
                              N_EXPERTS - 1).astype(jnp.int32)
    tile_rows = jnp.clip(counts[tile_expert] - (tile_row0 - starts[tile_expert]), 0, TM_EXP)
    used = tile_row0 < ends[-1]
    tile_rows = jnp.where(used, tile_rows, 0).astype(jnp.int32)
    changed = jnp.concatenate([jnp.ones((1,), bool), tile_expert[1:] != tile_expert[:-1]])
    ids = jnp.where(counts > 0, jnp.arange(N_EXPERTS, dtype=jnp.int32), N_EXPERTS)
    later = jnp.concatenate([lax.cummin(ids, reverse=True)[1:],
                             jnp.full((1,), N_EXPERTS, jnp.int32)])
    return {
        "tile_expert": tile_expert,
        "tile_rows": tile_rows,
        "first_of_expert": (used & changed).astype(jnp.int32),
        "next_expert": jnp.where(later < N_EXPERTS, later, -1)[tile_expert].astype(jnp.int32),
        "last_tile": jnp.maximum(ends[-1] // TM_EXP - 1, 0).astype(jnp.int32).reshape(1),
        "tail_start": (starts + counts).astype(jnp.int32),
        "tail_len": (padded - counts).astype(jnp.int32),
        "run_dst": flat(starts[None, :] + runs[:, 0, :]),
        "run_len": flat(runs[:, 1, :]),
        "run_off": flat(runs[:, 2, :]),
    }


def kernel(x, mix_norm, w_in, pool_w, pool_scale, w_branch_pool, w_branch_attn, rel_bias, w_out,
           ffn_norm, w_router, b_router, w1, b1, w2, b2, final_norm):
    batch, seq, d = x.shape
    n = batch * seq
    depth = mix_norm.shape[0]
    assert depth == 1, "the combine kernel fuses the final norm, so only one layer is supported"
    token_tiles = n // min(TM_POST, n)
    max_rows = (n * TOP_K_EXPERTS + N_EXPERTS * token_tiles * (ROW_ALIGN - 1)
                + N_EXPERTS * (TM_EXP - 1))
    n_tiles = (max_rows + TM_EXP - 1) // TM_EXP
    xf = x.reshape(n, d)
    for l in range(depth):
        qT, k, vT, qiT, ki, wiT, pp, g1 = _inproj(
            xf, mix_norm[l], w_in[l], pool_w[l], pool_scale[l], w_branch_pool[l], seq)
        attn = _attention(qT, k, vT, qiT, ki, wiT, rel_bias, batch, seq)
        x1, h2, lpos, gates, runs, counts = _post_attn(
            attn, pp, g1, xf, w_branch_attn[l], w_out[l], ffn_norm[l], w_router[l], b_router[l])
        plan = _routing_plan(runs, counts, n_tiles)
        xs = _dispatch(h2, lpos, plan, n_tiles * TM_EXP)
        y = _experts(xs, plan["tile_expert"], plan["tile_rows"], plan["first_of_expert"],
                     plan["next_expert"], w1[l], b1[l], w2[l], b2[l])
        xf = _combine(y, lpos, gates, x1, final_norm, plan)
    return xf.reshape(batch, seq, d)
```

```python
import functools
import math

import jax
import jax.numpy as jnp
import numpy as np
from jax import lax
from jax.experimental import pallas as pl
from jax.experimental.pallas import tpu as pltpu

D_MODEL = 1024
POOL_WIDTH = 512
POOL_WINDOWS = (2, 4, 8, 16)
POOL_GROUPS = len(POOL_WINDOWS)
POOL_GROUP_WIDTH = POOL_WIDTH // POOL_GROUPS
N_HEADS = 8
HEAD_DIM = 64
ATTN_WIDTH = N_HEADS * HEAD_DIM
N_IDX_HEADS = 4
IDX_DIM = 64
IDX_SCALE = (IDX_DIM ** -0.5) * (N_IDX_HEADS ** -0.5)
ATTN_SCALE = HEAD_DIM ** -0.5
TOPK_MAX = 256
REL_BUCKETS = 32
REL_MAX_DIST = 128
N_BRANCHES = 2
N_EXPERTS = 32
TOP_K_EXPERTS = 4
D_FF = D_MODEL
SWIGLU_LIMIT = 7.0
SWIGLU_ALPHA = 1.702
RMS_EPS = 1e-5
SPLIT_SIZES = (POOL_WIDTH, ATTN_WIDTH, ATTN_WIDTH, ATTN_WIDTH,
               N_IDX_HEADS * IDX_DIM, IDX_DIM, N_IDX_HEADS, N_BRANCHES * D_MODEL)

LANES = 128
SUBLANES = 8
VMEM_LIMIT_BYTES = 56 * 1024 * 1024

TM_IN = 512
TQ = 256
TK = 256
TM_POST = 512
TM_EXP = 256
ROW_ALIGN = SUBLANES
PERM_CHUNK = 256
POOL_HALO = 16
N_BISECT_BF16 = 10
N_BISECT_F32 = 8
BF16_ROWS = 2 * SUBLANES
BF16_STEP = 2.0 ** -7
TINY = 1e-30
PAIR = 2 * HEAD_DIM
IDX_PAD = LANES
ROUTER_PAD = LANES

F32 = jnp.float32
BF16 = jnp.bfloat16
NEG_INF = float("-inf")
M_INIT = -1e30
LOG2E = math.log2(math.e)


def _dot(a, b):
    return jnp.dot(a, b, preferred_element_type=F32)


def _dot_nt(a, b):
    return lax.dot_general(a, b, (((1,), (1,)), ((), ())), preferred_element_type=F32)


def _rmsnorm(x, g):
    ms = jnp.mean(x * x, axis=-1, keepdims=True)
    return x * lax.rsqrt(ms + RMS_EPS) * g


def _const_spec(shape):
    nd = len(shape)
    return pl.BlockSpec(shape, lambda *_: (0,) * nd)


def _params(n_axes):
    return pltpu.CompilerParams(
        dimension_semantics=("arbitrary",) * n_axes,
        vmem_limit_bytes=VMEM_LIMIT_BYTES)


_ROW_SECTIONS = (("pool", POOL_WIDTH), ("k", ATTN_WIDTH), ("ki", IDX_PAD),
                 ("g0", D_MODEL), ("g1", D_MODEL))
_COL_SECTIONS = (("q", ATTN_WIDTH), ("v", ATTN_WIDTH), ("qi", N_IDX_HEADS * IDX_PAD),
                 ("wi", 2 * SUBLANES))


def _section(sections, name):
    start = 0
    for key, width in sections:
        if key == name:
            return slice(start, start + width)
        start += width
    raise KeyError(name)


def _inproj_body(x_ref, g_ref, wrow_ref, wcol_ref, poolw_ref, pscale_ref, wbp_ref,
                 qT_ref, k_ref, vT_ref, qiT_ref, ki_ref, wiT_ref, pp_ref, g1_ref,
                 halo_ref, *, tiles_per_seq):
    i = pl.program_id(0)
    tm = x_ref.shape[0]
    h = _rmsnorm(x_ref[...], g_ref[...]).astype(BF16)
    row_w = lambda name: wrow_ref[:, _section(_ROW_SECTIONS, name)]
    col_w = lambda name: wcol_ref[_section(_COL_SECTIONS, name), :]

    qT = (_dot_nt(col_w("q"), h) * (ATTN_SCALE * LOG2E)).astype(BF16)
    vT = _dot_nt(col_w("v"), h).astype(BF16)
    qiT = _dot_nt(col_w("qi"), h).astype(BF16)
    wiT = _dot_nt(col_w("wi"), h) * IDX_SCALE
    for j in range(tm // TQ):
        qT_ref[j] = qT[:, j * TQ:(j + 1) * TQ]
        qiT_ref[j] = qiT[:, j * TQ:(j + 1) * TQ]
        wiT_ref[j] = wiT[:SUBLANES, j * TQ:(j + 1) * TQ]
    for j in range(tm // TK):
        vT_ref[j] = vT[:, j * TK:(j + 1) * TK]
    k_ref[...] = _dot(h, row_w("k")).astype(BF16)
    ki_ref[...] = _dot(h, row_w("ki")).astype(BF16)
    g1_ref[...] = jax.nn.sigmoid(_dot(h, row_w("g1"))).astype(BF16)

    zp = _dot(h, row_w("pool"))
    seq_tile = lax.rem(i, tiles_per_seq)

    @pl.when(seq_tile == 0)
    def _():
        halo_ref[...] = jnp.zeros_like(halo_ref)

    zext = jnp.concatenate([halo_ref[...], zp], axis=0)
    halo_ref[...] = zp[tm - POOL_HALO:, :]
    gw = POOL_GROUP_WIDTH
    s2 = zext + pltpu.roll(zext, 1, 0)
    s4 = s2[:, gw:] + pltpu.roll(s2[:, gw:], 2, 0)
    s8 = s4[:, gw:] + pltpu.roll(s4[:, gw:], 4, 0)
    s16 = s8[:, gw:] + pltpu.roll(s8[:, gw:], 8, 0)
    wsum = (s2[POOL_HALO:, :gw], s4[POOL_HALO:, :gw], s8[POOL_HALO:, :gw], s16[POOL_HALO:, :])
    t = seq_tile * tm + lax.broadcasted_iota(jnp.int32, (tm, 1), 0)
    mixed = []
    for g, w in enumerate(POOL_WINDOWS):
        cnt = jnp.minimum(t + 1, w).astype(F32)
        pooled = wsum[g] / cnt - zp[:, g * gw:(g + 1) * gw]
        mixed.append(_dot(pooled.astype(BF16), poolw_ref[g]) * pscale_ref[:, g * gw:(g + 1) * gw])
    mixed = jnp.concatenate(mixed, axis=1).astype(BF16)
    y_pool = _dot(mixed, wbp_ref[...])
    gate0 = jax.nn.sigmoid(_dot(h, row_w("g0")))
    pp_ref[...] = (gate0 * y_pool).astype(BF16)


def _inproj(xf, mix_norm, w_in, pool_w, pool_scale, w_branch_pool, seq):
    n, d = xf.shape
    tm = min(TM_IN, seq)
    assert seq % tm == 0 and tm % TK == 0 and tm % TQ == 0 and n % tm == 0
    offs = [0] + [int(o) for o in np.cumsum(SPLIT_SIZES)]
    z_pool, z_q, z_k, z_v, z_qi, z_ki, z_wi, z_gate = (
        w_in[:, offs[j]:offs[j + 1]] for j in range(len(SPLIT_SIZES)))
    pad_cols = lambda a, width: jnp.pad(a, ((0, 0), (0, width - a.shape[1])))
    qi_heads = jnp.pad(z_qi.reshape(d, N_IDX_HEADS, IDX_DIM),
                       ((0, 0), (0, 0), (0, IDX_PAD - IDX_DIM))).reshape(d, N_IDX_HEADS * IDX_PAD)
    parts = {"pool": z_pool, "k": z_k, "ki": pad_cols(z_ki, IDX_PAD),
             "g0": z_gate[:, :D_MODEL], "g1": z_gate[:, D_MODEL:],
             "q": z_q, "v": z_v, "qi": qi_heads, "wi": pad_cols(z_wi, 2 * SUBLANES)}
    w_row = jnp.concatenate([parts[k] for k, _ in _ROW_SECTIONS], axis=1).astype(BF16)
    w_col = jnp.concatenate([parts[k] for k, _ in _COL_SECTIONS], axis=1).astype(BF16).T
    consts = [mix_norm.reshape(1, d).astype(F32), w_row, w_col, pool_w.astype(BF16),
              pool_scale.reshape(1, POOL_WIDTH).astype(F32), w_branch_pool.astype(BF16)]
    grid = (n // tm,)
    row = lambda width: pl.BlockSpec((tm, width), lambda i: (i, 0))
    tiles = lambda t, height: pl.BlockSpec((tm // t, height, t), lambda i: (i, 0, 0))
    out_shape = [
        jax.ShapeDtypeStruct((n // TQ, ATTN_WIDTH, TQ), BF16),
        jax.ShapeDtypeStruct((n, ATTN_WIDTH), BF16),
        jax.ShapeDtypeStruct((n // TK, ATTN_WIDTH, TK), BF16),
        jax.ShapeDtypeStruct((n // TQ, N_IDX_HEADS * IDX_PAD, TQ), BF16),
        jax.ShapeDtypeStruct((n, IDX_PAD), BF16),
        jax.ShapeDtypeStruct((n // TQ, SUBLANES, TQ), F32),
        jax.ShapeDtypeStruct((n, D_MODEL), BF16),
        jax.ShapeDtypeStruct((n, D_MODEL), BF16),
    ]
    out_specs = [
        tiles(TQ, ATTN_WIDTH),
        row(ATTN_WIDTH),
        tiles(TK, ATTN_WIDTH),
        tiles(TQ, N_IDX_HEADS * IDX_PAD),
        row(IDX_PAD),
        tiles(TQ, SUBLANES),
        row(D_MODEL),
        row(D_MODEL),
    ]
    return pl.pallas_call(
        functools.partial(_inproj_body, tiles_per_seq=seq // tm),
        grid=grid,
        in_specs=[row(d)] + [_const_spec(c.shape) for c in consts],
        out_specs=out_specs,
        out_shape=out_shape,
        scratch_shapes=[pltpu.VMEM((POOL_HALO, POOL_WIDTH), F32)],
        compiler_params=_params(1),
        name="inproj",
    )(xf, *consts)


def _rel_thresholds():
    n = np.arange(0, 4 * REL_MAX_DIST)
    max_exact = REL_BUCKETS // 2
    nf = np.maximum(n, 1).astype(np.float32)
    large = max_exact + (np.log(nf / np.float32(max_exact))
                         / np.float32(math.log(REL_MAX_DIST / max_exact))
                         * np.float32(REL_BUCKETS - max_exact)).astype(np.int32)
    bucket = np.where(n < max_exact, n, np.minimum(large, REL_BUCKETS - 1))
    assert np.all(np.diff(bucket) >= 0) and np.all(np.diff(bucket) <= 1)
    assert bucket[-1] == REL_BUCKETS - 1
    return [int(np.argmax(bucket >= b)) for b in range(1, REL_BUCKETS)]


def _attn_body(table_ref, qT_ref, qiT_ref, wiT_ref, k_ref, ki_ref, vT_ref, o_ref,
               score_ref, sb_ref, band_ref, tri_ref, qm_ref, m_ref, l_ref, acc_ref, thr_ref,
               need_ref,
               *, topk):
    b = pl.program_id(0)
    qi = pl.program_id(1)
    tq = o_ref.shape[0]
    nk = qi + 1
    n_keys = score_ref.shape[0] * TK
    key_i = lax.broadcasted_iota(jnp.int32, (TK, tq), 0)
    qry_i = lax.broadcasted_iota(jnp.int32, (TK, tq), 1)

    @pl.when((b == 0) & (qi == 0))
    def _():
        r_i = lax.broadcasted_iota(jnp.int32, (TK, TK), 0)
        c_i = lax.broadcasted_iota(jnp.int32, (TK, TK), 1)
        tri_ref[...] = jnp.where(c_i < r_i, 1.0, 0.0).astype(BF16)
        thresholds = _rel_thresholds()
        assert thresholds[-1] <= TK
        for part in range(2):
            dist = qry_i - key_i + (1 - part) * TK
            for h in range(N_HEADS):
                bias = jnp.full((TK, tq), table_ref[h], F32)
                for bkt, thr in enumerate(thresholds, start=1):
                    bias = jnp.where(dist >= thr, table_ref[bkt * N_HEADS + h], bias)
                band_ref[h, part] = jnp.where(dist < 0, NEG_INF, bias * LOG2E)

    first_half = lax.broadcasted_iota(jnp.int32, (PAIR, tq), 0) < HEAD_DIM
    for hp in range(N_HEADS // 2):
        qp = qT_ref[0, hp * PAIR:(hp + 1) * PAIR, :]
        zero = jnp.zeros_like(qp)
        qm_ref[2 * hp] = jnp.where(first_half, qp, zero)
        qm_ref[2 * hp + 1] = jnp.where(first_half, zero, qp)

    wiT = wiT_ref[0]

    def score_tile(kj, carry):
        mx, mn = carry
        ki_t = ki_ref[pl.ds(pl.multiple_of(kj * TK, TK), TK), :]
        heads = [_dot(ki_t, qiT_ref[0, h * IDX_PAD:(h + 1) * IDX_PAD, :])
                 for h in range(N_IDX_HEADS)]
        sc = jnp.maximum(heads[0], 0.0) * wiT[0:1, :]
        for h in range(1, N_IDX_HEADS):
            sc = sc + jnp.maximum(heads[h], 0.0) * wiT[h:h + 1, :]
        causal = (kj * TK + key_i) <= (qi * tq + qry_i)
        masked = jnp.where(causal, sc, NEG_INF)
        score_ref[kj] = masked
        sb_ref[kj] = masked.astype(BF16)
        mx = jnp.maximum(mx, jnp.max(sc, axis=0, keepdims=True))
        mn = jnp.minimum(mn, jnp.min(sc, axis=0, keepdims=True))
        return mx, mn

    row_max, row_min = lax.fori_loop(
        0, nk, score_tile,
        (jnp.full((1, tq), NEG_INF, F32), jnp.full((1, tq), -NEG_INF, F32)))

    def fold(x, op):
        return op(x.reshape(TK // SUBLANES, SUBLANES, tq), axis=0)

    @pl.when(lax.rem(nk, 2) == 1)
    def _():
        score_ref[nk] = jnp.full((TK, tq), NEG_INF, F32)
        sb_ref[nk] = jnp.full((TK, tq), NEG_INF, BF16)

    n_pairs = (nk + 1) // 2

    def count_where(pred):
        def body(j, acc):
            for kj in (2 * j, 2 * j + 1):
                acc = acc + fold(jnp.where(pred(score_ref[kj]), 1.0, 0.0), jnp.sum)
            return acc
        acc = lax.fori_loop(0, n_pairs, body, jnp.zeros((SUBLANES, tq), F32))
        return jnp.sum(acc, axis=0, keepdims=True)

    def max_where(pred):
        def body(j, acc):
            for kj in (2 * j, 2 * j + 1):
                s = score_ref[kj]
                acc = jnp.maximum(acc, fold(jnp.where(pred(s), s, NEG_INF), jnp.max))
            return acc
        acc = lax.fori_loop(0, n_pairs, body, jnp.full((SUBLANES, tq), NEG_INF, F32))
        return jnp.max(acc, axis=0, keepdims=True)

    kf = float(topk)
    thr_ref[...] = jnp.full((1, tq), NEG_INF, F32)
    need_ref[...] = jnp.full((1, tq), float(n_keys), F32)

    @pl.when(qi * tq + 1 > topk)
    def _():
        def count_above_bf16(mid_b):
            mid_t = jnp.broadcast_to(mid_b, (TK, tq))
            one, zero = jnp.ones((), BF16), jnp.zeros((), BF16)

            def body(j, acc):
                for kj in (2 * j, 2 * j + 1):
                    m = jnp.where(sb_ref[kj] > mid_t, one, zero)
                    parts = [m[r * BF16_ROWS:(r + 1) * BF16_ROWS]
                             for r in range(TK // BF16_ROWS)]
                    while len(parts) > 1:
                        parts = [a + b for a, b in zip(parts[::2], parts[1::2])]
                    acc = acc + parts[0].astype(F32)
                return acc
            acc = lax.fori_loop(0, n_pairs, body, jnp.zeros((BF16_ROWS, tq), F32))
            return jnp.sum(acc, axis=0, keepdims=True)

        def widen(v, sign):
            return v + sign * (jnp.abs(v) * BF16_STEP + TINY)

        def bisect_bf16(_, carry):
            lo, hi = carry
            mid_b = (0.5 * (lo + hi)).astype(BF16)
            above = count_above_bf16(mid_b) >= kf
            mid = mid_b.astype(F32)
            return jnp.where(above, mid, lo), jnp.where(above, hi, mid)

        lo, hi = lax.fori_loop(0, N_BISECT_BF16, bisect_bf16,
                               (widen(row_min, -1.0), widen(row_max, 1.0)))

        def bisect(_, carry):
            lo, hi = carry
            mid = 0.5 * (lo + hi)
            above = count_where(lambda s: s > mid) >= kf
            return jnp.where(above, mid, lo), jnp.where(above, hi, mid)

        _, hi = lax.fori_loop(0, N_BISECT_F32, bisect, (widen(lo, -1.0), widen(hi, 1.0)))
        cand = max_where(lambda s: s <= hi)
        n_ge = count_where(lambda s: s >= cand)

        def unresolved(state):
            it, _, n_ge = state
            return (jnp.min(n_ge) < kf) & (it < n_keys)

        def step(state):
            it, cand, n_ge = state
            nxt = max_where(lambda s: s < cand)
            n_nxt = count_where(lambda s: s >= nxt)
            open_ = n_ge < kf
            return it + 1, jnp.where(open_, nxt, cand), jnp.where(open_, n_nxt, n_ge)

        _, thr, n_ge = lax.while_loop(unresolved, step, (jnp.int32(0), cand, n_ge))
        thr_ref[...] = thr

        @pl.when(jnp.max(n_ge) > kf)
        def _():
            need_ref[...] = kf - count_where(lambda s: s > thr)

    m_ref[...] = jnp.full(m_ref.shape, M_INIT, F32)
    l_ref[...] = jnp.zeros(l_ref.shape, F32)
    acc_ref[...] = jnp.zeros(acc_ref.shape, F32)
    thr = thr_ref[...]
    need = need_ref[...]

    far_bias = [table_ref[(REL_BUCKETS - 1) * N_HEADS + h] * LOG2E for h in range(N_HEADS)]

    def attend(kj, ties_before, near):
        sc = score_ref[kj]
        tied = jnp.where(sc == thr, 1.0, 0.0)
        rank = _dot(tri_ref[...], tied.astype(BF16)) + ties_before
        ties_before = rank[TK - 1:, :] + tied[TK - 1:, :]
        sel_bias = jnp.where(
            sc > thr, 0.0,
            jnp.where(sc == thr, jnp.where(rank < need, 0.0, NEG_INF), NEG_INF))
        k_t = k_ref[pl.ds(pl.multiple_of(kj * TK, TK), TK), :]
        vT_t = vT_ref[kj]
        logits = [_dot(k_t[:, (h // 2) * PAIR:(h // 2 + 1) * PAIR], qm_ref[h])
                  for h in range(N_HEADS)]
        probs, alphas = [], []
        for h in range(N_HEADS):
            m_old = m_ref[h]
            if near:
                s = logits[h] + band_ref[h, kj - qi + 1] + sel_bias
                m_new = jnp.maximum(m_old, jnp.max(s, axis=0, keepdims=True))
                p = jnp.exp2(s - m_new)
            else:
                s = logits[h] + sel_bias
                m_new = jnp.maximum(m_old, jnp.max(s, axis=0, keepdims=True) + far_bias[h])
                p = jnp.exp2(s - (m_new - far_bias[h]))
            alpha = jnp.exp2(m_old - m_new)
            l_ref[h] = alpha * l_ref[h] + jnp.sum(p, axis=0, keepdims=True)
            m_ref[h] = m_new
            probs.append(p.astype(BF16))
            alphas.append(alpha)
        for hp in range(N_HEADS // 2):
            vTp = vT_t[hp * PAIR:(hp + 1) * PAIR, :]
            outs = [_dot(vTp, probs[2 * hp + e]) for e in range(2)]
            acc_ref[hp] = (acc_ref[hp] * jnp.where(first_half, alphas[2 * hp], alphas[2 * hp + 1])
                           + jnp.where(first_half, outs[0], outs[1]))
        return ties_before

    n_far = jnp.maximum(qi - 1, 0)
    ties = lax.fori_loop(0, n_far, functools.partial(attend, near=False),
                         jnp.zeros((1, tq), F32))
    lax.fori_loop(n_far, nk, functools.partial(attend, near=True), ties)
    for hp in range(N_HEADS // 2):
        denom = jnp.where(first_half, l_ref[2 * hp], l_ref[2 * hp + 1])
        o_ref[:, hp * PAIR:(hp + 1) * PAIR] = (acc_ref[hp] / denom).T.astype(o_ref.dtype)


def _attention(qT, k, vT, qiT, ki, wiT, rel_bias, batch, seq):
    n = k.shape[0]
    tq = min(TQ, seq)
    assert tq == TQ == TK and seq % tq == 0
    topk = min(TOPK_MAX, seq // 4)
    assert topk == tq or seq == tq
    nq = seq // tq
    nkt = seq // TK
    assert nkt % 2 == 0
    table = rel_bias.astype(F32).reshape(REL_BUCKETS * N_HEADS)
    qtile = lambda height: pl.BlockSpec((1, height, tq), lambda b, i: (b * nq + i, 0, 0))
    return pl.pallas_call(
        functools.partial(_attn_body, topk=topk),
        grid=(batch, nq),
        in_specs=[
            pl.BlockSpec(memory_space=pltpu.SMEM),
            qtile(ATTN_WIDTH),
            qtile(N_IDX_HEADS * IDX_PAD),
            qtile(SUBLANES),
            pl.BlockSpec((seq, ATTN_WIDTH), lambda b, i: (b, 0)),
            pl.BlockSpec((seq, IDX_PAD), lambda b, i: (b, 0)),
            pl.BlockSpec((nkt, ATTN_WIDTH, TK), lambda b, i: (b, 0, 0)),
        ],
        out_specs=pl.BlockSpec((tq, ATTN_WIDTH), lambda b, i: (b * nq + i, 0)),
        out_shape=jax.ShapeDtypeStruct((n, ATTN_WIDTH), BF16),
        scratch_shapes=[
            pltpu.VMEM((nkt, TK, tq), F32),
            pltpu.VMEM((nkt, TK, tq), BF16),
            pltpu.VMEM((N_HEADS, 2, TK, tq), F32),
            pltpu.VMEM((TK, TK), BF16),
            pltpu.VMEM((N_HEADS, PAIR, tq), BF16),
            pltpu.VMEM((N_HEADS, 1, tq), F32),
            pltpu.VMEM((N_HEADS, 1, tq), F32),
            pltpu.VMEM((N_HEADS // 2, PAIR, tq), F32),
            pltpu.VMEM((1, tq), F32),
            pltpu.VMEM((1, tq), F32),
        ],
        compiler_params=_params(2),
        name="attention",
    )(table, qT, qiT, wiT, k, ki, vT)


def _post_body(attn_ref, pp_ref, g1_ref, x_ref, wba_ref, wout_ref, fg_ref, wr_hi_ref, wr_lo_ref,
               br_ref, x1_ref, h2_ref, lpos_ref, gates_ref, runs_ref, counts_ref, carry_ref):
    i = pl.program_id(0)
    tm = x_ref.shape[0]
    y_attn = _dot(attn_ref[...], wba_ref[...])
    merged = pp_ref[...].astype(F32) + g1_ref[...].astype(F32) * y_attn
    x1 = x_ref[...] + _dot(merged.astype(BF16), wout_ref[...])
    x1_ref[...] = x1
    h2 = _rmsnorm(x1, fg_ref[...])
    h2_ref[...] = h2

    h_hi = h2.astype(BF16)
    h_lo = (h2 - h_hi.astype(F32)).astype(BF16)
    logits = (_dot(h_hi, wr_hi_ref[...]) + _dot(h_hi, wr_lo_ref[...])
              + _dot(h_lo, wr_hi_ref[...]) + br_ref[...])

    lane = lax.broadcasted_iota(jnp.int32, (tm, ROUTER_PAD), 1)
    work = logits
    vals, idxs = [], []
    for _ in range(TOP_K_EXPERTS):
        mx = jnp.max(work, axis=1, keepdims=True)
        ix = jnp.min(jnp.where(work == mx, lane, ROUTER_PAD), axis=1, keepdims=True)
        vals.append(mx)
        idxs.append(ix)
        work = jnp.where(lane == ix, NEG_INF, work)
    exps = [jnp.exp(v - vals[0]) for v in vals]
    denom = exps[0] + exps[1] + exps[2] + exps[3]

    member = jnp.zeros((tm, ROUTER_PAD), F32)
    for ix in idxs:
        member = member + jnp.where(lane == ix, 1.0, 0.0)

    @pl.when(i == 0)
    def _():
        carry_ref[...] = jnp.zeros_like(carry_ref)

    r_i = lax.broadcasted_iota(jnp.int32, (tm, tm), 0)
    c_i = lax.broadcasted_iota(jnp.int32, (tm, tm), 1)
    strict_lower = jnp.where(c_i < r_i, 1.0, 0.0).astype(BF16)
    local = _dot(strict_lower, member.astype(BF16))
    count = jnp.sum(member, axis=0, keepdims=True)
    run_len = jnp.floor((count + (ROW_ALIGN - 1)) * (1.0 / ROW_ALIGN)) * ROW_ALIGN
    e_r = lax.broadcasted_iota(jnp.int32, (ROUTER_PAD, ROUTER_PAD), 0)
    e_c = lax.broadcasted_iota(jnp.int32, (ROUTER_PAD, ROUTER_PAD), 1)
    strict_upper = jnp.where(e_r < e_c, 1.0, 0.0).astype(BF16)
    run_off = _dot(jnp.broadcast_to(run_len, (SUBLANES, ROUTER_PAD)).astype(BF16),
                   strict_upper)[:1, :]
    run_start = carry_ref[...]
    carry_new = run_start + run_len
    carry_ref[...] = carry_new
    counts_ref[...] = jnp.broadcast_to(carry_new, counts_ref.shape)
    sub = lax.broadcasted_iota(jnp.int32, (SUBLANES, ROUTER_PAD), 0)
    runs_ref[0] = jnp.where(sub == 0, run_start, jnp.where(sub == 1, run_len,
                                                          jnp.where(sub == 2, run_off, 0.0)))

    slot = run_off + local
    lpos = jnp.zeros((tm, ROUTER_PAD), jnp.int32)
    gates = jnp.zeros((tm, ROUTER_PAD), F32)
    for k in range(TOP_K_EXPERTS):
        pos = jnp.sum(jnp.where(lane == idxs[k], slot, 0.0), axis=1, keepdims=True)
        lpos = jnp.where(lane == k, pos.astype(jnp.int32), lpos)
        gates = jnp.where(lane == k, exps[k] / denom, gates)
    lpos_ref[...] = lpos
    gates_ref[...] = gates


def _post_attn(attn, pp, g1, xf, w_branch_attn, w_out, ffn_norm, w_router, b_router):
    n, d = xf.shape
    tm = min(TM_POST, n)
    assert n % tm == 0
    wr = jnp.pad(w_router.astype(F32), ((0, 0), (0, ROUTER_PAD - N_EXPERTS)))
    wr_hi = wr.astype(BF16)
    wr_lo = (wr - wr_hi.astype(F32)).astype(BF16)
    br = jnp.pad(b_router.astype(F32), (0, ROUTER_PAD - N_EXPERTS),
                 constant_values=NEG_INF).reshape(1, ROUTER_PAD)
    consts = [w_branch_attn.astype(BF16), w_out.astype(BF16),
              ffn_norm.reshape(1, d).astype(F32), wr_hi, wr_lo, br]
    row = lambda width: pl.BlockSpec((tm, width), lambda i: (i, 0))
    return pl.pallas_call(
        _post_body,
        grid=(n // tm,),
        in_specs=[row(ATTN_WIDTH), row(d), row(d), row(d)] + [_const_spec(c.shape) for c in consts],
        out_specs=[row(d), row(d), row(ROUTER_PAD), row(ROUTER_PAD),
                   pl.BlockSpec((1, SUBLANES, ROUTER_PAD), lambda i: (i, 0, 0)),
                   _const_spec((SUBLANES, ROUTER_PAD))],
        out_shape=[
            jax.ShapeDtypeStruct((n, d), F32),
            jax.ShapeDtypeStruct((n, d), F32),
            jax.ShapeDtypeStruct((n, ROUTER_PAD), jnp.int32),
            jax.ShapeDtypeStruct((n, ROUTER_PAD), F32),
            jax.ShapeDtypeStruct((n // tm, SUBLANES, ROUTER_PAD), F32),
            jax.ShapeDtypeStruct((SUBLANES, ROUTER_PAD), F32),
        ],
        scratch_shapes=[pltpu.VMEM((1, ROUTER_PAD), F32)],
        compiler_params=_params(1),
        name="post_attn",
    )(attn, pp, g1, xf, *consts)


def _block_copies(length, make_copy, max_block):
    block = max_block
    while block >= ROW_ALIGN:
        offset = pl.multiple_of(jnp.bitwise_and(length, -2 * block), ROW_ALIGN)
        yield jnp.bitwise_and(length, block) != 0, make_copy(offset, block)
        block //= 2


def _for_each_block(length, make_copy, max_block, action):
    for pred, cp in _block_copies(length, make_copy, max_block):
        @pl.when(pred)
        def _(cp=cp):
            action(cp)


def _staging_rows(tm):
    return TOP_K_EXPERTS * tm + N_EXPERTS * ROW_ALIGN


def _dispatch_body(run_dst_ref, run_len_ref, run_off_ref, tail_start_ref, tail_len_ref,
                   last_tile_ref, lpos_ref, h2_ref, xs_ref, buf_ref, zero_ref, sem, zsem,
                   *, first_tail_tile):
    i = pl.program_id(0)
    tm = h2_ref.shape[0]
    n_tiles = xs_ref.shape[0] // TM_EXP
    rows = buf_ref.shape[1]

    lpos_t = lpos_ref[...].T
    tokens = h2_ref[...].astype(BF16)
    half_now = lax.rem(i, 2)

    def permute_chunk(c, carry):
        r0 = pl.multiple_of(c * PERM_CHUNK, PERM_CHUNK)
        slot = r0 + lax.broadcasted_iota(jnp.int32, (PERM_CHUNK, tm), 0)
        onehot = jnp.where(slot == lpos_t[0:1, :], 1.0, 0.0)
        for k in range(1, TOP_K_EXPERTS):
            onehot = onehot + jnp.where(slot == lpos_t[k:k + 1, :], 1.0, 0.0)
        buf_ref[half_now, pl.ds(r0, PERM_CHUNK), :] = _dot(onehot.astype(BF16), tokens)
        return carry

    def run_copy(tile, e):
        base = tile * N_EXPERTS + e
        half = lax.rem(tile, 2)
        src0 = pl.multiple_of(run_off_ref[base], ROW_ALIGN)
        dst0 = pl.multiple_of(run_dst_ref[base], ROW_ALIGN)
        return run_len_ref[base], lambda off, blk: pltpu.make_async_copy(
            buf_ref.at[half, pl.ds(src0 + off, blk)], xs_ref.at[pl.ds(dst0 + off, blk)],
            sem.at[half])

    def for_runs(tile, action):
        def body(e, carry):
            length, make = run_copy(tile, e)
            _for_each_block(length, make, tm, action)
            return carry
        lax.fori_loop(0, N_EXPERTS, body, 0)

    @pl.when(i >= 2)
    def _():
        for_runs(i - 2, lambda cp: cp.wait())

    lax.fori_loop(0, rows // PERM_CHUNK, permute_chunk, 0)
    for_runs(i, lambda cp: cp.start())

    def zero_fill(action):
        def tail_copy(e):
            dst0 = pl.multiple_of(tail_start_ref[e], ROW_ALIGN)
            return tail_len_ref[e], lambda off, blk: pltpu.make_async_copy(
                zero_ref.at[pl.ds(0, blk)], xs_ref.at[pl.ds(dst0 + off, blk)], zsem)

        def fill_tail(e, carry):
            length, make = tail_copy(e)
            _for_each_block(length, make, TM_EXP // 2, action)
            return carry

        lax.fori_loop(0, N_EXPERTS, fill_tail, 0)

        def tail_tile(t, carry):
            @pl.when(t > last_tile_ref[0])
            def _():
                action(pltpu.make_async_copy(
                    zero_ref, xs_ref.at[pl.ds(pl.multiple_of(t * TM_EXP, TM_EXP), TM_EXP)], zsem))
            return carry

        lax.fori_loop(first_tail_tile, n_tiles, tail_tile, 0)

    @pl.when(i == 0)
    def _():
        zero_ref[...] = jnp.zeros_like(zero_ref)
        zero_fill(lambda cp: cp.start())

    @pl.when(i == pl.num_programs(0) - 1)
    def _():
        @pl.when(i >= 1)
        def _():
            for_runs(i - 1, lambda cp: cp.wait())
        for_runs(i, lambda cp: cp.wait())
        zero_fill(lambda cp: cp.wait())


def _dispatch(h2, lpos, plan, n_rows):
    n, d = h2.shape
    tm = min(TM_POST, n)
    assert n % tm == 0 and tm % ROW_ALIGN == 0 and _staging_rows(tm) % PERM_CHUNK == 0
    return pl.pallas_call(
        functools.partial(_dispatch_body, first_tail_tile=(n * TOP_K_EXPERTS) // TM_EXP),
        grid_spec=pltpu.PrefetchScalarGridSpec(
            num_scalar_prefetch=6,
            grid=(n // tm,),
            in_specs=[
                pl.BlockSpec((tm, ROUTER_PAD), lambda i, *_: (i, 0)),
                pl.BlockSpec((tm, d), lambda i, *_: (i, 0)),
            ],
            out_specs=pl.BlockSpec(memory_space=pl.ANY),
            scratch_shapes=[pltpu.VMEM((2, _staging_rows(tm), d), F32),
                            pltpu.VMEM((TM_EXP, d), F32),
                            pltpu.SemaphoreType.DMA((2,)), pltpu.SemaphoreType.DMA],
        ),
        out_shape=jax.ShapeDtypeStruct((n_rows, d), F32),
        compiler_params=_params(1),
        name="dispatch",
    )(plan["run_dst"], plan["run_len"], plan["run_off"], plan["tail_start"], plan["tail_len"],
      plan["last_tile"], lpos, h2)


def _experts_body(tile_expert_ref, tile_rows_ref, run_start_ref, next_expert_ref,
                  xs_ref, w1_hbm, b1_ref, w2_hbm, b2_ref, y_ref,
                  w1f_ref, w2f_ref, w1b_ref, w2b_ref, wsem):
    i = pl.program_id(0)

    def fetch(expert):
        return (pltpu.make_async_copy(w1_hbm.at[expert], w1f_ref, wsem.at[0]),
                pltpu.make_async_copy(w2_hbm.at[expert], w2f_ref, wsem.at[1]))

    @pl.when(i == 0)
    def _():
        for cp in fetch(tile_expert_ref[0]):
            cp.start()

    @pl.when(run_start_ref[i] == 1)
    def _():
        for cp in fetch(tile_expert_ref[i]):
            cp.wait()
        w1b_ref[...] = w1f_ref[...].astype(BF16)
        w2b_ref[...] = w2f_ref[...].astype(BF16)

        @pl.when(next_expert_ref[i] >= 0)
        def _():
            for cp in fetch(next_expert_ref[i]):
                cp.start()

    @pl.when(tile_rows_ref[i] > 0)
    def _():
        x = xs_ref[...].astype(BF16)
        gu = _dot(x, w1b_ref[...]) + b1_ref[0]
        g = jnp.minimum(gu[:, :D_FF], SWIGLU_LIMIT)
        u = jnp.clip(gu[:, D_FF:], -SWIGLU_LIMIT, SWIGLU_LIMIT)
        act = g * jax.nn.sigmoid(SWIGLU_ALPHA * g) * (u + 1.0)
        y_ref[...] = _dot(act.astype(BF16), w2b_ref[...]) + b2_ref[0]

    @pl.when(tile_rows_ref[i] == 0)
    def _():
        y_ref[...] = jnp.zeros_like(y_ref)


def _experts(xs, tile_expert, tile_rows, run_start, next_expert, w1, b1, w2, b2):
    n_rows, d = xs.shape
    n_tiles = n_rows // TM_EXP
    tile = lambda i, *_: (i, 0)
    per_expert = lambda i, te, *_: (te[i], 0, 0)
    return pl.pallas_call(
        _experts_body,
        grid_spec=pltpu.PrefetchScalarGridSpec(
            num_scalar_prefetch=4,
            grid=(n_tiles,),
            in_specs=[
                pl.BlockSpec((TM_EXP, d), tile),
                pl.BlockSpec(memory_space=pl.ANY),
                pl.BlockSpec((1, 1, 2 * D_FF), per_expert),
                pl.BlockSpec(memory_space=pl.ANY),
                pl.BlockSpec((1, 1, d), per_expert),
            ],
            out_specs=pl.BlockSpec((TM_EXP, d), tile),
            scratch_shapes=[
                pltpu.VMEM((d, 2 * D_FF), F32), pltpu.VMEM((D_FF, d), F32),
                pltpu.VMEM((d, 2 * D_FF), BF16), pltpu.VMEM((D_FF, d), BF16),
                pltpu.SemaphoreType.DMA((2,)),
            ],
        ),
        out_shape=jax.ShapeDtypeStruct((n_rows, d), F32),
        compiler_params=_params(1),
        name="experts",
    )(tile_expert, tile_rows, run_start, next_expert, xs, w1,
      b1.reshape(N_EXPERTS, 1, 2 * D_FF), w2, b2.reshape(N_EXPERTS, 1, d))


def _combine_body(run_dst_ref, run_len_ref, run_off_ref, lpos_ref, gates_ref, x1_ref, fn_ref,
                  y_ref, o_ref, buf_ref, sem):
    i = pl.program_id(0)
    tm = x1_ref.shape[0]
    rows = buf_ref.shape[1]

    @pl.when(i == 0)
    def _():
        buf_ref[...] = jnp.zeros_like(buf_ref)

    def run_copy(tile, e):
        base = tile * N_EXPERTS + e
        half = lax.rem(tile, 2)
        src0 = pl.multiple_of(run_dst_ref[base], ROW_ALIGN)
        dst0 = pl.multiple_of(run_off_ref[base], ROW_ALIGN)
        return run_len_ref[base], lambda off, blk: pltpu.make_async_copy(
            y_ref.at[pl.ds(src0 + off, blk)], buf_ref.at[half, pl.ds(dst0 + off, blk)],
            sem.at[half])

    def for_runs(tile, action):
        def body(e, carry):
            length, make = run_copy(tile, e)
            _for_each_block(length, make, tm, action)
            return carry
        lax.fori_loop(0, N_EXPERTS, body, 0)

    @pl.when(i == 0)
    def _():
        for_runs(i, lambda cp: cp.start())

    @pl.when(i + 1 < pl.num_programs(0))
    def _():
        for_runs(i + 1, lambda cp: cp.start())

    for_runs(i, lambda cp: cp.wait())

    lpos = lpos_ref[...]
    gates = gates_ref[...]
    slot = lax.broadcasted_iota(jnp.int32, (tm, rows), 1)
    weights = jnp.zeros((tm, rows), F32)
    for k in range(TOP_K_EXPERTS):
        weights = weights + jnp.where(slot == lpos[:, k:k + 1], gates[:, k:k + 1], 0.0)
    out = x1_ref[...] + _dot(weights.astype(BF16), buf_ref[lax.rem(i, 2)].astype(BF16))
    o_ref[...] = _rmsnorm(out, fn_ref[...])


def _combine(y, lpos, gates, x1, final_norm, plan):
    n, d = x1.shape
    tm = min(TM_POST, n)
    assert n % tm == 0
    row = lambda width: pl.BlockSpec((tm, width), lambda i, *_: (i, 0))
    return pl.pallas_call(
        _combine_body,
        grid_spec=pltpu.PrefetchScalarGridSpec(
            num_scalar_prefetch=3,
            grid=(n // tm,),
            in_specs=[
                row(ROUTER_PAD), row(ROUTER_PAD), row(d),
                pl.BlockSpec((1, d), lambda i, *_: (0, 0)),
                pl.BlockSpec(memory_space=pl.ANY),
            ],
            out_specs=row(d),
            scratch_shapes=[pltpu.VMEM((2, _staging_rows(tm), d), F32),
                            pltpu.SemaphoreType.DMA((2,))],
        ),
        out_shape=jax.ShapeDtypeStruct((n, d), F32),
        compiler_params=_params(1),
        name="combine",
    )(plan["run_dst"], plan["run_len"], plan["run_off"], lpos, gates, x1,
      final_norm.reshape(1, d).astype(F32), y)


def _routing_plan(runs, counts, n_tiles):
    counts = counts[0, :N_EXPERTS].astype(jnp.int32)
    padded = ((counts + TM_EXP - 1) // TM_EXP) * TM_EXP
    ends = jnp.cumsum(padded)
    starts = ends - padded
    runs = runs[:, :, :N_EXPERTS].astype(jnp.int32)
    flat = lambda a: a.reshape(-1).astype(jnp.int32)
    tile_row0 = jnp.arange(n_tiles, dtype=jnp.int32) * TM_EXP
    tile_expert = jnp.minimum(jnp.sum(tile_row0[:, None] >= ends[None, :], axis=1),
                              N_EXPERTS - 1).astype(jnp.int32)
    tile_rows = jnp.clip(counts[tile_expert] - (tile_row0 - starts[tile_expert]), 0, TM_EXP)
    used = tile_row0 < ends[-1]
    tile_rows = jnp.where(used, tile_rows, 0).astype(jnp.int32)
    changed = jnp.concatenate([jnp.ones((1,), bool), tile_expert[1:] != tile_expert[:-1]])
    ids = jnp.where(counts > 0, jnp.arange(N_EXPERTS, dtype=jnp.int32), N_EXPERTS)
    later = jnp.concatenate([lax.cummin(ids, reverse=True)[1:],
                             jnp.full((1,), N_EXPERTS, jnp.int32)])
    return {
        "tile_expert": tile_expert,
        "tile_rows": tile_rows,
        "first_of_expert": (used & changed).astype(jnp.int32),
        "next_expert": jnp.where(later < N_EXPERTS, later, -1)[tile_expert].astype(jnp.int32),
        "last_tile": jnp.maximum(ends[-1] // TM_EXP - 1, 0).astype(jnp.int32).reshape(1),
        "tail_start": (starts + counts).astype(jnp.int32),
        "tail_len": (padded - counts).astype(jnp.int32),
        "run_dst": flat(starts[None, :] + runs[:, 0, :]),
        "run_len": flat(runs[:, 1, :]),
        "run_off": flat(runs[:, 2, :]),
    }


def kernel(x, mix_norm, w_in, pool_w, pool_scale, w_branch_pool, w_branch_attn, rel_bias, w_out,
           ffn_norm, w_router, b_router, w1, b1, w2, b2, final_norm):
    batch, seq, d = x.shape
    n = batch * seq
    depth = mix_norm.shape[0]
    assert depth == 1, "the combine kernel fuses the final norm, so only one layer is supported"
    token_tiles = n // min(TM_POST, n)
    max_rows = (n * TOP_K_EXPERTS + N_EXPERTS * token_tiles * (ROW_ALIGN - 1)
                + N_EXPERTS * (TM_EXP - 1))
    n_tiles = (max_rows + TM_EXP - 1) // TM_EXP
    xf = x.reshape(n, d)
    for l in range(depth):
        qT, k, vT, qiT, ki, wiT, pp, g1 = _inproj(
            xf, mix_norm[l], w_in[l], pool_w[l], pool_scale[l], w_branch_pool[l], seq)
        attn = _attention(qT, k, vT, qiT, ki, wiT, rel_bias, batch, seq)
        x1, h2, lpos, gates, runs, counts = _post_attn(
            attn, pp, g1, xf, w_branch_attn[l], w_out[l], ffn_norm[l], w_router[l], b_router[l])
        plan = _routing_plan(runs, counts, n_tiles)
        xs = _dispatch(h2, lpos, plan, n_tiles * TM_EXP)
        y = _experts(xs, plan["tile_expert"], plan["tile_rows"], plan["first_of_expert"],
                     plan["next_expert"], w1[l], b1[l], w2[l], b2[l])
        xf = _combine(y, lpos, gates, x1, final_norm, plan)
    return xf.reshape(batch, seq, d)
```

```python
import functools
import math

import jax
import jax.numpy as jnp
import numpy as np
from jax import lax
from jax.experimental import pallas as pl
from jax.experimental.pallas import tpu as pltpu

D_MODEL = 1024
POOL_WIDTH = 512
POOL_WINDOWS = (2, 4, 8, 16)
POOL_GROUPS = len(POOL_WINDOWS)
POOL_GROUP_WIDTH = POOL_WIDTH // POOL_GROUPS
N_HEADS = 8
HEAD_DIM = 64
ATTN_WIDTH = N_HEADS * HEAD_DIM
N_IDX_HEADS = 4
IDX_DIM = 64
IDX_SCALE = (IDX_DIM ** -0.5) * (N_IDX_HEADS ** -0.5)
ATTN_SCALE = HEAD_DIM ** -0.5
TOPK_MAX = 256
REL_BUCKETS = 32
REL_MAX_DIST = 128
N_BRANCHES = 2
N_EXPERTS = 32
TOP_K_EXPERTS = 4
D_FF = D_MODEL
SWIGLU_LIMIT = 7.0
SWIGLU_ALPHA = 1.702
RMS_EPS = 1e-5
SPLIT_SIZES = (POOL_WIDTH, ATTN_WIDTH, ATTN_WIDTH, ATTN_WIDTH,
               N_IDX_HEADS * IDX_DIM, IDX_DIM, N_IDX_HEADS, N_BRANCHES * D_MODEL)

LANES = 128
SUBLANES = 8
VMEM_LIMIT_BYTES = 56 * 1024 * 1024

TM_IN = 512
TQ = 256
TK = 256
TM_POST = 512
TM_EXP = 512
ROW_ALIGN = SUBLANES
PERM_CHUNK = 256
POOL_HALO = 16
N_BISECT_BF16 = 10
N_BISECT_F32 = 8
BF16_ROWS = 2 * SUBLANES
BF16_STEP = 2.0 ** -7
TINY = 1e-30
PAIR = 2 * HEAD_DIM
IDX_PAD = LANES
ROUTER_PAD = LANES

F32 = jnp.float32
BF16 = jnp.bfloat16
NEG_INF = float("-inf")
M_INIT = -1e30
LOG2E = math.log2(math.e)


def _dot(a, b):
    return jnp.dot(a, b, preferred_element_type=F32)


def _dot_nt(a, b):
    return lax.dot_general(a, b, (((1,), (1,)), ((), ())), preferred_element_type=F32)


def _rmsnorm(x, g):
    ms = jnp.mean(x * x, axis=-1, keepdims=True)
    return x * lax.rsqrt(ms + RMS_EPS) * g


def _const_spec(shape):
    nd = len(shape)
    return pl.BlockSpec(shape, lambda *_: (0,) * nd)


def _params(n_axes):
    return pltpu.CompilerParams(
        dimension_semantics=("arbitrary",) * n_axes,
        vmem_limit_bytes=VMEM_LIMIT_BYTES)


_ROW_SECTIONS = (("pool", POOL_WIDTH), ("k", ATTN_WIDTH), ("ki", IDX_PAD),
                 ("g0", D_MODEL), ("g1", D_MODEL))
_COL_SECTIONS = (("q", ATTN_WIDTH), ("v", ATTN_WIDTH), ("qi", N_IDX_HEADS * IDX_PAD),
                 ("wi", 2 * SUBLANES))


def _section(sections, name):
    start = 0
    for key, width in sections:
        if key == name:
            return slice(start, start + width)
        start += width
    raise KeyError(name)


def _inproj_body(x_ref, g_ref, wrow_ref, wcol_ref, poolw_ref, pscale_ref, wbp_ref,
                 qT_ref, k_ref, vT_ref, qiT_ref, ki_ref, wiT_ref, pp_ref, g1_ref,
                 halo_ref, *, tiles_per_seq):
    i = pl.program_id(0)
    tm = x_ref.shape[0]
    h = _rmsnorm(x_ref[...], g_ref[...]).astype(BF16)
    row_w = lambda name: wrow_ref[:, _section(_ROW_SECTIONS, name)]
    col_w = lambda name: wcol_ref[_section(_COL_SECTIONS, name), :]

    qT = (_dot_nt(col_w("q"), h) * (ATTN_SCALE * LOG2E)).astype(BF16)
    vT = _dot_nt(col_w("v"), h).astype(BF16)
    qiT = _dot_nt(col_w("qi"), h).astype(BF16)
    wiT = _dot_nt(col_w("wi"), h) * IDX_SCALE
    for j in range(tm // TQ):
        qT_ref[j] = qT[:, j * TQ:(j + 1) * TQ]
        qiT_ref[j] = qiT[:, j * TQ:(j + 1) * TQ]
        wiT_ref[j] = wiT[:SUBLANES, j * TQ:(j + 1) * TQ]
    for j in range(tm // TK):
        vT_ref[j] = vT[:, j * TK:(j + 1) * TK]
    k_ref[...] = _dot(h, row_w("k")).astype(BF16)
    ki_ref[...] = _dot(h, row_w("ki")).astype(BF16)
    g1_ref[...] = jax.nn.sigmoid(_dot(h, row_w("g1"))).astype(BF16)

    zp = _dot(h, row_w("pool"))
    seq_tile = lax.rem(i, tiles_per_seq)

    @pl.when(seq_tile == 0)
    def _():
        halo_ref[...] = jnp.zeros_like(halo_ref)

    zext = jnp.concatenate([halo_ref[...], zp], axis=0)
    halo_ref[...] = zp[tm - POOL_HALO:, :]
    gw = POOL_GROUP_WIDTH
    s2 = zext + pltpu.roll(zext, 1, 0)
    s4 = s2[:, gw:] + pltpu.roll(s2[:, gw:], 2, 0)
    s8 = s4[:, gw:] + pltpu.roll(s4[:, gw:], 4, 0)
    s16 = s8[:, gw:] + pltpu.roll(s8[:, gw:], 8, 0)
    wsum = (s2[POOL_HALO:, :gw], s4[POOL_HALO:, :gw], s8[POOL_HALO:, :gw], s16[POOL_HALO:, :])
    t = seq_tile * tm + lax.broadcasted_iota(jnp.int32, (tm, 1), 0)
    mixed = []
    for g, w in enumerate(POOL_WINDOWS):
        cnt = jnp.minimum(t + 1, w).astype(F32)
        pooled = wsum[g] / cnt - zp[:, g * gw:(g + 1) * gw]
        mixed.append(_dot(pooled.astype(BF16), poolw_ref[g]) * pscale_ref[:, g * gw:(g + 1) * gw])
    mixed = jnp.concatenate(mixed, axis=1).astype(BF16)
    y_pool = _dot(mixed, wbp_ref[...])
    gate0 = jax.nn.sigmoid(_dot(h, row_w("g0")))
    pp_ref[...] = (gate0 * y_pool).astype(BF16)


def _inproj(xf, mix_norm, w_in, pool_w, pool_scale, w_branch_pool, seq):
    n, d = xf.shape
    tm = min(TM_IN, seq)
    assert seq % tm == 0 and tm % TK == 0 and tm % TQ == 0 and n % tm == 0
    offs = [0] + [int(o) for o in np.cumsum(SPLIT_SIZES)]
    z_pool, z_q, z_k, z_v, z_qi, z_ki, z_wi, z_gate = (
        w_in[:, offs[j]:offs[j + 1]] for j in range(len(SPLIT_SIZES)))
    pad_cols = lambda a, width: jnp.pad(a, ((0, 0), (0, width - a.shape[1])))
    qi_heads = jnp.pad(z_qi.reshape(d, N_IDX_HEADS, IDX_DIM),
                       ((0, 0), (0, 0), (0, IDX_PAD - IDX_DIM))).reshape(d, N_IDX_HEADS * IDX_PAD)
    parts = {"pool": z_pool, "k": z_k, "ki": pad_cols(z_ki, IDX_PAD),
             "g0": z_gate[:, :D_MODEL], "g1": z_gate[:, D_MODEL:],
             "q": z_q, "v": z_v, "qi": qi_heads, "wi": pad_cols(z_wi, 2 * SUBLANES)}
    w_row = jnp.concatenate([parts[k] for k, _ in _ROW_SECTIONS], axis=1).astype(BF16)
    w_col = jnp.concatenate([parts[k] for k, _ in _COL_SECTIONS], axis=1).astype(BF16).T
    consts = [mix_norm.reshape(1, d).astype(F32), w_row, w_col, pool_w.astype(BF16),
              pool_scale.reshape(1, POOL_WIDTH).astype(F32), w_branch_pool.astype(BF16)]
    grid = (n // tm,)
    row = lambda width: pl.BlockSpec((tm, width), lambda i: (i, 0))
    tiles = lambda t, height: pl.BlockSpec((tm // t, height, t), lambda i: (i, 0, 0))
    out_shape = [
        jax.ShapeDtypeStruct((n // TQ, ATTN_WIDTH, TQ), BF16),
        jax.ShapeDtypeStruct((n, ATTN_WIDTH), BF16),
        jax.ShapeDtypeStruct((n // TK, ATTN_WIDTH, TK), BF16),
        jax.ShapeDtypeStruct((n // TQ, N_IDX_HEADS * IDX_PAD, TQ), BF16),
        jax.ShapeDtypeStruct((n, IDX_PAD), BF16),
        jax.ShapeDtypeStruct((n // TQ, SUBLANES, TQ), F32),
        jax.ShapeDtypeStruct((n, D_MODEL), BF16),
        jax.ShapeDtypeStruct((n, D_MODEL), BF16),
    ]
    out_specs = [
        tiles(TQ, ATTN_WIDTH),
        row(ATTN_WIDTH),
        tiles(TK, ATTN_WIDTH),
        tiles(TQ, N_IDX_HEADS * IDX_PAD),
        row(IDX_PAD),
        tiles(TQ, SUBLANES),
        row(D_MODEL),
        row(D_MODEL),
    ]
    return pl.pallas_call(
        functools.partial(_inproj_body, tiles_per_seq=seq // tm),
        grid=grid,
        in_specs=[row(d)] + [_const_spec(c.shape) for c in consts],
        out_specs=out_specs,
        out_shape=out_shape,
        scratch_shapes=[pltpu.VMEM((POOL_HALO, POOL_WIDTH), F32)],
        compiler_params=_params(1),
        name="inproj",
    )(xf, *consts)


def _rel_thresholds():
    n = np.arange(0, 4 * REL_MAX_DIST)
    max_exact = REL_BUCKETS // 2
    nf = np.maximum(n, 1).astype(np.float32)
    large = max_exact + (np.log(nf / np.float32(max_exact))
                         / np.float32(math.log(REL_MAX_DIST / max_exact))
                         * np.float32(REL_BUCKETS - max_exact)).astype(np.int32)
    bucket = np.where(n < max_exact, n, np.minimum(large, REL_BUCKETS - 1))
    assert np.all(np.diff(bucket) >= 0) and np.all(np.diff(bucket) <= 1)
    assert bucket[-1] == REL_BUCKETS - 1
    return [int(np.argmax(bucket >= b)) for b in range(1, REL_BUCKETS)]


def _attn_body(table_ref, qT_ref, qiT_ref, wiT_ref, k_ref, ki_ref, vT_ref, o_ref,
               score_ref, sb_ref, band_ref, tri_ref, qm_ref, m_ref, l_ref, acc_ref, thr_ref,
               need_ref,
               *, topk):
    b = pl.program_id(0)
    qi = pl.program_id(1)
    tq = o_ref.shape[0]
    nk = qi + 1
    n_keys = score_ref.shape[0] * TK
    key_i = lax.broadcasted_iota(jnp.int32, (TK, tq), 0)
    qry_i = lax.broadcasted_iota(jnp.int32, (TK, tq), 1)

    @pl.when((b == 0) & (qi == 0))
    def _():
        r_i = lax.broadcasted_iota(jnp.int32, (TK, TK), 0)
        c_i = lax.broadcasted_iota(jnp.int32, (TK, TK), 1)
        tri_ref[...] = jnp.where(c_i < r_i, 1.0, 0.0).astype(BF16)
        thresholds = _rel_thresholds()
        assert thresholds[-1] <= TK
        for part in range(2):
            dist = qry_i - key_i + (1 - part) * TK
            for h in range(N_HEADS):
                bias = jnp.full((TK, tq), table_ref[h], F32)
                for bkt, thr in enumerate(thresholds, start=1):
                    bias = jnp.where(dist >= thr, table_ref[bkt * N_HEADS + h], bias)
                band_ref[h, part] = jnp.where(dist < 0, NEG_INF, bias * LOG2E)

    first_half = lax.broadcasted_iota(jnp.int32, (PAIR, tq), 0) < HEAD_DIM
    for hp in range(N_HEADS // 2):
        qp = qT_ref[0, hp * PAIR:(hp + 1) * PAIR, :]
        zero = jnp.zeros_like(qp)
        qm_ref[2 * hp] = jnp.where(first_half, qp, zero)
        qm_ref[2 * hp + 1] = jnp.where(first_half, zero, qp)

    wiT = wiT_ref[0]

    n_pairs = (nk + 1) // 2

    def score_pair(j, carry):
        mx, mn = carry
        tiles = (2 * j, 2 * j + 1)
        heads = [[_dot(ki_ref[pl.ds(pl.multiple_of(kj * TK, TK), TK), :],
                       qiT_ref[0, h * IDX_PAD:(h + 1) * IDX_PAD, :])
                  for h in range(N_IDX_HEADS)] for kj in tiles]
        for kj, dots in zip(tiles, heads):
            sc = jnp.maximum(dots[0], 0.0) * wiT[0:1, :]
            for h in range(1, N_IDX_HEADS):
                sc = sc + jnp.maximum(dots[h], 0.0) * wiT[h:h + 1, :]
            causal = (kj * TK + key_i) <= (qi * tq + qry_i)
            masked = jnp.where(causal, sc, NEG_INF)
            score_ref[kj] = masked
            sb_ref[kj] = masked.astype(BF16)
            mx = jnp.maximum(mx, jnp.max(sc, axis=0, keepdims=True))
            mn = jnp.minimum(mn, jnp.min(sc, axis=0, keepdims=True))
        return mx, mn

    row_max, row_min = lax.fori_loop(
        0, n_pairs, score_pair,
        (jnp.full((1, tq), NEG_INF, F32), jnp.full((1, tq), -NEG_INF, F32)))

    def fold(x, op):
        return op(x.reshape(TK // SUBLANES, SUBLANES, tq), axis=0)

    def count_where(pred):
        def body(j, acc):
            for kj in (2 * j, 2 * j + 1):
                acc = acc + fold(jnp.where(pred(score_ref[kj]), 1.0, 0.0), jnp.sum)
            return acc
        acc = lax.fori_loop(0, n_pairs, body, jnp.zeros((SUBLANES, tq), F32))
        return jnp.sum(acc, axis=0, keepdims=True)

    def max_where(pred):
        def body(j, acc):
            for kj in (2 * j, 2 * j + 1):
                s = score_ref[kj]
                acc = jnp.maximum(acc, fold(jnp.where(pred(s), s, NEG_INF), jnp.max))
            return acc
        acc = lax.fori_loop(0, n_pairs, body, jnp.full((SUBLANES, tq), NEG_INF, F32))
        return jnp.max(acc, axis=0, keepdims=True)

    kf = float(topk)
    thr_ref[...] = jnp.full((1, tq), NEG_INF, F32)
    need_ref[...] = jnp.full((1, tq), float(n_keys), F32)

    @pl.when(qi * tq + 1 > topk)
    def _():
        def count_above_bf16(mid_b):
            mid_t = jnp.broadcast_to(mid_b, (TK, tq))
            one, zero = jnp.ones((), BF16), jnp.zeros((), BF16)

            def body(j, acc):
                for kj in (2 * j, 2 * j + 1):
                    m = jnp.where(sb_ref[kj] > mid_t, one, zero)
                    parts = [m[r * BF16_ROWS:(r + 1) * BF16_ROWS]
                             for r in range(TK // BF16_ROWS)]
                    while len(parts) > 1:
                        parts = [a + b for a, b in zip(parts[::2], parts[1::2])]
                    acc = acc + parts[0].astype(F32)
                return acc
            acc = lax.fori_loop(0, n_pairs, body, jnp.zeros((BF16_ROWS, tq), F32))
            return jnp.sum(acc, axis=0, keepdims=True)

        def widen(v, sign):
            return v + sign * (jnp.abs(v) * BF16_STEP + TINY)

        def bisect_bf16(_, carry):
            lo, hi = carry
            mid_b = (0.5 * (lo + hi)).astype(BF16)
            above = count_above_bf16(mid_b) >= kf
            mid = mid_b.astype(F32)
            return jnp.where(above, mid, lo), jnp.where(above, hi, mid)

        lo, hi = lax.fori_loop(0, N_BISECT_BF16, bisect_bf16,
                               (widen(row_min, -1.0), widen(row_max, 1.0)))

        def bisect(_, carry):
            lo, hi = carry
            mid = 0.5 * (lo + hi)
            above = count_where(lambda s: s > mid) >= kf
            return jnp.where(above, mid, lo), jnp.where(above, hi, mid)

        _, hi = lax.fori_loop(0, N_BISECT_F32, bisect, (widen(lo, -1.0), widen(hi, 1.0)))
        cand = max_where(lambda s: s <= hi)
        n_ge = count_where(lambda s: s >= cand)

        def unresolved(state):
            it, _, n_ge = state
            return (jnp.min(n_ge) < kf) & (it < n_keys)

        def step(state):
            it, cand, n_ge = state
            nxt = max_where(lambda s: s < cand)
            n_nxt = count_where(lambda s: s >= nxt)
            open_ = n_ge < kf
            return it + 1, jnp.where(open_, nxt, cand), jnp.where(open_, n_nxt, n_ge)

        _, thr, n_ge = lax.while_loop(unresolved, step, (jnp.int32(0), cand, n_ge))
        thr_ref[...] = thr

        @pl.when(jnp.max(n_ge) > kf)
        def _():
            need_ref[...] = kf - count_where(lambda s: s > thr)

    m_ref[...] = jnp.full(m_ref.shape, M_INIT, F32)
    l_ref[...] = jnp.zeros(l_ref.shape, F32)
    acc_ref[...] = jnp.zeros(acc_ref.shape, F32)
    thr = thr_ref[...]
    need = need_ref[...]

    far_bias = [table_ref[(REL_BUCKETS - 1) * N_HEADS + h] * LOG2E for h in range(N_HEADS)]

    def attend(kj, ties_before, near):
        sc = score_ref[kj]
        tied = jnp.where(sc == thr, 1.0, 0.0)
        rank = _dot(tri_ref[...], tied.astype(BF16)) + ties_before
        ties_before = rank[TK - 1:, :] + tied[TK - 1:, :]
        sel_bias = jnp.where(
            sc > thr, 0.0,
            jnp.where(sc == thr, jnp.where(rank < need, 0.0, NEG_INF), NEG_INF))
        k_t = k_ref[pl.ds(pl.multiple_of(kj * TK, TK), TK), :]
        vT_t = vT_ref[kj]
        logits = [_dot(k_t[:, (h // 2) * PAIR:(h // 2 + 1) * PAIR], qm_ref[h])
                  for h in range(N_HEADS)]
        probs, alphas = [], []
        for h in range(N_HEADS):
            m_old = m_ref[h]
            if near:
                s = logits[h] + band_ref[h, kj - qi + 1] + sel_bias
                m_new = jnp.maximum(m_old, jnp.max(s, axis=0, keepdims=True))
                p = jnp.exp2(s - m_new)
            else:
                s = logits[h] + sel_bias
                m_new = jnp.maximum(m_old, jnp.max(s, axis=0, keepdims=True) + far_bias[h])
                p = jnp.exp2(s - (m_new - far_bias[h]))
            alpha = jnp.exp2(m_old - m_new)
            l_ref[h] = alpha * l_ref[h] + jnp.sum(p, axis=0, keepdims=True)
            m_ref[h] = m_new
            probs.append(p.astype(BF16))
            alphas.append(alpha)
        for hp in range(N_HEADS // 2):
            vTp = vT_t[hp * PAIR:(hp + 1) * PAIR, :]
            outs = [_dot(vTp, probs[2 * hp + e]) for e in range(2)]
            acc_ref[hp] = (acc_ref[hp] * jnp.where(first_half, alphas[2 * hp], alphas[2 * hp + 1])
                           + jnp.where(first_half, outs[0], outs[1]))
        return ties_before

    n_far = jnp.maximum(qi - 1, 0)
    ties = lax.fori_loop(0, n_far, functools.partial(attend, near=False),
                         jnp.zeros((1, tq), F32))
    lax.fori_loop(n_far, nk, functools.partial(attend, near=True), ties)
    for hp in range(N_HEADS // 2):
        denom = jnp.where(first_half, l_ref[2 * hp], l_ref[2 * hp + 1])
        o_ref[:, hp * PAIR:(hp + 1) * PAIR] = (acc_ref[hp] / denom).T.astype(o_ref.dtype)


def _attention(qT, k, vT, qiT, ki, wiT, rel_bias, batch, seq):
    n = k.shape[0]
    tq = min(TQ, seq)
    assert tq == TQ == TK and seq % tq == 0
    topk = min(TOPK_MAX, seq // 4)
    assert topk == tq or seq == tq
    nq = seq // tq
    nkt = seq // TK
    assert nkt % 2 == 0
    table = rel_bias.astype(F32).reshape(REL_BUCKETS * N_HEADS)
    qtile = lambda height: pl.BlockSpec((1, height, tq), lambda b, i: (b * nq + i, 0, 0))
    return pl.pallas_call(
        functools.partial(_attn_body, topk=topk),
        grid=(batch, nq),
        in_specs=[
            pl.BlockSpec(memory_space=pltpu.SMEM),
            qtile(ATTN_WIDTH),
            qtile(N_IDX_HEADS * IDX_PAD),
            qtile(SUBLANES),
            pl.BlockSpec((seq, ATTN_WIDTH), lambda b, i: (b, 0)),
            pl.BlockSpec((seq, IDX_PAD), lambda b, i: (b, 0)),
            pl.BlockSpec((nkt, ATTN_WIDTH, TK), lambda b, i: (b, 0, 0)),
        ],
        out_specs=pl.BlockSpec((tq, ATTN_WIDTH), lambda b, i: (b * nq + i, 0)),
        out_shape=jax.ShapeDtypeStruct((n, ATTN_WIDTH), BF16),
        scratch_shapes=[
            pltpu.VMEM((nkt, TK, tq), F32),
            pltpu.VMEM((nkt, TK, tq), BF16),
            pltpu.VMEM((N_HEADS, 2, TK, tq), F32),
            pltpu.VMEM((TK, TK), BF16),
            pltpu.VMEM((N_HEADS, PAIR, tq), BF16),
            pltpu.VMEM((N_HEADS, 1, tq), F32),
            pltpu.VMEM((N_HEADS, 1, tq), F32),
            pltpu.VMEM((N_HEADS // 2, PAIR, tq), F32),
            pltpu.VMEM((1, tq), F32),
            pltpu.VMEM((1, tq), F32),
        ],
        compiler_params=_params(2),
        name="attention",
    )(table, qT, qiT, wiT, k, ki, vT)


def _post_body(attn_ref, pp_ref, g1_ref, x_ref, wba_ref, wout_ref, fg_ref, wr_hi_ref, wr_lo_ref,
               br_ref, x1_ref, h2_ref, lpos_ref, gates_ref, runs_ref, counts_ref, carry_ref):
    i = pl.program_id(0)
    tm = x_ref.shape[0]
    y_attn = _dot(attn_ref[...], wba_ref[...])
    merged = pp_ref[...].astype(F32) + g1_ref[...].astype(F32) * y_attn
    x1 = x_ref[...] + _dot(merged.astype(BF16), wout_ref[...])
    x1_ref[...] = x1
    h2 = _rmsnorm(x1, fg_ref[...])
    h2_ref[...] = h2

    h_hi = h2.astype(BF16)
    h_lo = (h2 - h_hi.astype(F32)).astype(BF16)
    logits = (_dot(h_hi, wr_hi_ref[...]) + _dot(h_hi, wr_lo_ref[...])
              + _dot(h_lo, wr_hi_ref[...]) + br_ref[...])

    lane = lax.broadcasted_iota(jnp.int32, (tm, ROUTER_PAD), 1)
    work = logits
    vals, idxs = [], []
    for _ in range(TOP_K_EXPERTS):
        mx = jnp.max(work, axis=1, keepdims=True)
        ix = jnp.min(jnp.where(work == mx, lane, ROUTER_PAD), axis=1, keepdims=True)
        vals.append(mx)
        idxs.append(ix)
        work = jnp.where(lane == ix, NEG_INF, work)
    exps = [jnp.exp(v - vals[0]) for v in vals]
    denom = exps[0] + exps[1] + exps[2] + exps[3]

    member = jnp.zeros((tm, ROUTER_PAD), F32)
    for ix in idxs:
        member = member + jnp.where(lane == ix, 1.0, 0.0)

    @pl.when(i == 0)
    def _():
        carry_ref[...] = jnp.zeros_like(carry_ref)

    r_i = lax.broadcasted_iota(jnp.int32, (tm, tm), 0)
    c_i = lax.broadcasted_iota(jnp.int32, (tm, tm), 1)
    strict_lower = jnp.where(c_i < r_i, 1.0, 0.0).astype(BF16)
    local = _dot(strict_lower, member.astype(BF16))
    count = jnp.sum(member, axis=0, keepdims=True)
    run_len = jnp.floor((count + (ROW_ALIGN - 1)) * (1.0 / ROW_ALIGN)) * ROW_ALIGN
    e_r = lax.broadcasted_iota(jnp.int32, (ROUTER_PAD, ROUTER_PAD), 0)
    e_c = lax.broadcasted_iota(jnp.int32, (ROUTER_PAD, ROUTER_PAD), 1)
    strict_upper = jnp.where(e_r < e_c, 1.0, 0.0).astype(BF16)
    run_off = _dot(jnp.broadcast_to(run_len, (SUBLANES, ROUTER_PAD)).astype(BF16),
                   strict_upper)[:1, :]
    run_start = carry_ref[...]
    carry_new = run_start + run_len
    carry_ref[...] = carry_new
    counts_ref[...] = jnp.broadcast_to(carry_new, counts_ref.shape)
    sub = lax.broadcasted_iota(jnp.int32, (SUBLANES, ROUTER_PAD), 0)
    runs_ref[0] = jnp.where(sub == 0, run_start, jnp.where(sub == 1, run_len,
                                                          jnp.where(sub == 2, run_off, 0.0)))

    slot = run_off + local
    lpos = jnp.zeros((tm, ROUTER_PAD), jnp.int32)
    gates = jnp.zeros((tm, ROUTER_PAD), F32)
    for k in range(TOP_K_EXPERTS):
        pos = jnp.sum(jnp.where(lane == idxs[k], slot, 0.0), axis=1, keepdims=True)
        lpos = jnp.where(lane == k, pos.astype(jnp.int32), lpos)
        gates = jnp.where(lane == k, exps[k] / denom, gates)
    lpos_ref[...] = lpos
    gates_ref[...] = gates


def _post_attn(attn, pp, g1, xf, w_branch_attn, w_out, ffn_norm, w_router, b_router):
    n, d = xf.shape
    tm = min(TM_POST, n)
    assert n % tm == 0
    wr = jnp.pad(w_router.astype(F32), ((0, 0), (0, ROUTER_PAD - N_EXPERTS)))
    wr_hi = wr.astype(BF16)
    wr_lo = (wr - wr_hi.astype(F32)).astype(BF16)
    br = jnp.pad(b_router.astype(F32), (0, ROUTER_PAD - N_EXPERTS),
                 constant_values=NEG_INF).reshape(1, ROUTER_PAD)
    consts = [w_branch_attn.astype(BF16), w_out.astype(BF16),
              ffn_norm.reshape(1, d).astype(F32), wr_hi, wr_lo, br]
    row = lambda width: pl.BlockSpec((tm, width), lambda i: (i, 0))
    return pl.pallas_call(
        _post_body,
        grid=(n // tm,),
        in_specs=[row(ATTN_WIDTH), row(d), row(d), row(d)] + [_const_spec(c.shape) for c in consts],
        out_specs=[row(d), row(d), row(ROUTER_PAD), row(ROUTER_PAD),
                   pl.BlockSpec((1, SUBLANES, ROUTER_PAD), lambda i: (i, 0, 0)),
                   _const_spec((SUBLANES, ROUTER_PAD))],
        out_shape=[
            jax.ShapeDtypeStruct((n, d), F32),
            jax.ShapeDtypeStruct((n, d), F32),
            jax.ShapeDtypeStruct((n, ROUTER_PAD), jnp.int32),
            jax.ShapeDtypeStruct((n, ROUTER_PAD), F32),
            jax.ShapeDtypeStruct((n // tm, SUBLANES, ROUTER_PAD), F32),
            jax.ShapeDtypeStruct((SUBLANES, ROUTER_PAD), F32),
        ],
        scratch_shapes=[pltpu.VMEM((1, ROUTER_PAD), F32)],
        compiler_params=_params(1),
        name="post_attn",
    )(attn, pp, g1, xf, *consts)


def _block_copies(length, make_copy, max_block):
    block = max_block
    while block >= ROW_ALIGN:
        offset = pl.multiple_of(jnp.bitwise_and(length, -2 * block), ROW_ALIGN)
        yield jnp.bitwise_and(length, block) != 0, make_copy(offset, block)
        block //= 2


def _for_each_block(length, make_copy, max_block, action):
    for pred, cp in _block_copies(length, make_copy, max_block):
        @pl.when(pred)
        def _(cp=cp):
            action(cp)


def _staging_rows(tm):
    return TOP_K_EXPERTS * tm + N_EXPERTS * ROW_ALIGN


def _dispatch_body(run_dst_ref, run_len_ref, run_off_ref, tail_start_ref, tail_len_ref,
                   last_tile_ref, lpos_ref, h2_ref, xs_ref, buf_ref, zero_ref, sem, zsem,
                   *, first_tail_tile):
    i = pl.program_id(0)
    tm = h2_ref.shape[0]
    n_tiles = xs_ref.shape[0] // TM_EXP
    rows = buf_ref.shape[1]

    lpos_t = lpos_ref[...].T
    tokens = h2_ref[...].astype(BF16)
    half_now = lax.rem(i, 2)

    def permute_chunk(c, carry):
        r0 = pl.multiple_of(c * PERM_CHUNK, PERM_CHUNK)
        slot = r0 + lax.broadcasted_iota(jnp.int32, (PERM_CHUNK, tm), 0)
        onehot = jnp.where(slot == lpos_t[0:1, :], 1.0, 0.0)
        for k in range(1, TOP_K_EXPERTS):
            onehot = onehot + jnp.where(slot == lpos_t[k:k + 1, :], 1.0, 0.0)
        buf_ref[half_now, pl.ds(r0, PERM_CHUNK), :] = _dot(onehot.astype(BF16), tokens)
        return carry

    def run_copy(tile, e):
        base = tile * N_EXPERTS + e
        half = lax.rem(tile, 2)
        src0 = pl.multiple_of(run_off_ref[base], ROW_ALIGN)
        dst0 = pl.multiple_of(run_dst_ref[base], ROW_ALIGN)
        return run_len_ref[base], lambda off, blk: pltpu.make_async_copy(
            buf_ref.at[half, pl.ds(src0 + off, blk)], xs_ref.at[pl.ds(dst0 + off, blk)],
            sem.at[half])

    def for_runs(tile, action):
        def body(e, carry):
            length, make = run_copy(tile, e)
            _for_each_block(length, make, tm, action)
            return carry
        lax.fori_loop(0, N_EXPERTS, body, 0)

    @pl.when(i >= 2)
    def _():
        for_runs(i - 2, lambda cp: cp.wait())

    lax.fori_loop(0, rows // PERM_CHUNK, permute_chunk, 0)
    for_runs(i, lambda cp: cp.start())

    def zero_fill(action):
        def tail_copy(e):
            dst0 = pl.multiple_of(tail_start_ref[e], ROW_ALIGN)
            return tail_len_ref[e], lambda off, blk: pltpu.make_async_copy(
                zero_ref.at[pl.ds(0, blk)], xs_ref.at[pl.ds(dst0 + off, blk)], zsem)

        def fill_tail(e, carry):
            length, make = tail_copy(e)
            _for_each_block(length, make, TM_EXP // 2, action)
            return carry

        lax.fori_loop(0, N_EXPERTS, fill_tail, 0)

        def tail_tile(t, carry):
            @pl.when(t > last_tile_ref[0])
            def _():
                action(pltpu.make_async_copy(
                    zero_ref, xs_ref.at[pl.ds(pl.multiple_of(t * TM_EXP, TM_EXP), TM_EXP)], zsem))
            return carry

        lax.fori_loop(first_tail_tile, n_tiles, tail_tile, 0)

    @pl.when(i == 0)
    def _():
        zero_ref[...] = jnp.zeros_like(zero_ref)
        zero_fill(lambda cp: cp.start())

    @pl.when(i == pl.num_programs(0) - 1)
    def _():
        @pl.when(i >= 1)
        def _():
            for_runs(i - 1, lambda cp: cp.wait())
        for_runs(i, lambda cp: cp.wait())
        zero_fill(lambda cp: cp.wait())


def _dispatch(h2, lpos, plan, n_rows):
    n, d = h2.shape
    tm = min(TM_POST, n)
    assert n % tm == 0 and tm % ROW_ALIGN == 0 and _staging_rows(tm) % PERM_CHUNK == 0
    return pl.pallas_call(
        functools.partial(_dispatch_body, first_tail_tile=(n * TOP_K_EXPERTS) // TM_EXP),
        grid_spec=pltpu.PrefetchScalarGridSpec(
            num_scalar_prefetch=6,
            grid=(n // tm,),
            in_specs=[
                pl.BlockSpec((tm, ROUTER_PAD), lambda i, *_: (i, 0)),
                pl.BlockSpec((tm, d), lambda i, *_: (i, 0)),
            ],
            out_specs=pl.BlockSpec(memory_space=pl.ANY),
            scratch_shapes=[pltpu.VMEM((2, _staging_rows(tm), d), F32),
                            pltpu.VMEM((TM_EXP, d), F32),
                            pltpu.SemaphoreType.DMA((2,)), pltpu.SemaphoreType.DMA],
        ),
        out_shape=jax.ShapeDtypeStruct((n_rows, d), F32),
        compiler_params=_params(1),
        name="dispatch",
    )(plan["run_dst"], plan["run_len"], plan["run_off"], plan["tail_start"], plan["tail_len"],
      plan["last_tile"], lpos, h2)


def _experts_body(tile_expert_ref, tile_rows_ref, run_start_ref, next_expert_ref,
                  xs_ref, w1_hbm, b1_ref, w2_hbm, b2_ref, y_ref,
                  w1f_ref, w2f_ref, w1b_ref, w2b_ref, wsem):
    i = pl.program_id(0)

    def fetch(expert):
        return (pltpu.make_async_copy(w1_hbm.at[expert], w1f_ref, wsem.at[0]),
                pltpu.make_async_copy(w2_hbm.at[expert], w2f_ref, wsem.at[1]))

    @pl.when(i == 0)
    def _():
        for cp in fetch(tile_expert_ref[0]):
            cp.start()

    @pl.when(run_start_ref[i] == 1)
    def _():
        for cp in fetch(tile_expert_ref[i]):
            cp.wait()
        w1b_ref[...] = w1f_ref[...].astype(BF16)
        w2b_ref[...] = w2f_ref[...].astype(BF16)

        @pl.when(next_expert_ref[i] >= 0)
        def _():
            for cp in fetch(next_expert_ref[i]):
                cp.start()

    @pl.when(tile_rows_ref[i] > 0)
    def _():
        x = xs_ref[...].astype(BF16)
        gu = _dot(x, w1b_ref[...]) + b1_ref[0]
        g = jnp.minimum(gu[:, :D_FF], SWIGLU_LIMIT)
        u = jnp.clip(gu[:, D_FF:], -SWIGLU_LIMIT, SWIGLU_LIMIT)
        act = g * jax.nn.sigmoid(SWIGLU_ALPHA * g) * (u + 1.0)
        y_ref[...] = _dot(act.astype(BF16), w2b_ref[...]) + b2_ref[0]

    @pl.when(tile_rows_ref[i] == 0)
    def _():
        y_ref[...] = jnp.zeros_like(y_ref)


def _experts(xs, tile_expert, tile_rows, run_start, next_expert, w1, b1, w2, b2):
    n_rows, d = xs.shape
    n_tiles = n_rows // TM_EXP
    tile = lambda i, *_: (i, 0)
    per_expert = lambda i, te, *_: (te[i], 0, 0)
    return pl.pallas_call(
        _experts_body,
        grid_spec=pltpu.PrefetchScalarGridSpec(
            num_scalar_prefetch=4,
            grid=(n_tiles,),
            in_specs=[
                pl.BlockSpec((TM_EXP, d), tile),
                pl.BlockSpec(memory_space=pl.ANY),
                pl.BlockSpec((1, 1, 2 * D_FF), per_expert),
                pl.BlockSpec(memory_space=pl.ANY),
                pl.BlockSpec((1, 1, d), per_expert),
            ],
            out_specs=pl.BlockSpec((TM_EXP, d), tile),
            scratch_shapes=[
                pltpu.VMEM((d, 2 * D_FF), F32), pltpu.VMEM((D_FF, d), F32),
                pltpu.VMEM((d, 2 * D_FF), BF16), pltpu.VMEM((D_FF, d), BF16),
                pltpu.SemaphoreType.DMA((2,)),
            ],
        ),
        out_shape=jax.ShapeDtypeStruct((n_rows, d), F32),
        compiler_params=_params(1),
        name="experts",
    )(tile_expert, tile_rows, run_start, next_expert, xs, w1,
      b1.reshape(N_EXPERTS, 1, 2 * D_FF), w2, b2.reshape(N_EXPERTS, 1, d))


def _combine_body(run_dst_ref, run_len_ref, run_off_ref, lpos_ref, gates_ref, x1_ref, fn_ref,
                  y_ref, o_ref, buf_ref, sem):
    i = pl.program_id(0)
    tm = x1_ref.shape[0]
    rows = buf_ref.shape[1]

    @pl.when(i == 0)
    def _():
        buf_ref[...] = jnp.zeros_like(buf_ref)

    def run_copy(tile, e):
        base = tile * N_EXPERTS + e
        half = lax.rem(tile, 2)
        src0 = pl.multiple_of(run_dst_ref[base], ROW_ALIGN)
        dst0 = pl.multiple_of(run_off_ref[base], ROW_ALIGN)
        return run_len_ref[base], lambda off, blk: pltpu.make_async_copy(
            y_ref.at[pl.ds(src0 + off, blk)], buf_ref.at[half, pl.ds(dst0 + off, blk)],
            sem.at[half])

    def for_runs(tile, action):
        def body(e, carry):
            length, make = run_copy(tile, e)
            _for_each_block(length, make, tm, action)
            return carry
        lax.fori_loop(0, N_EXPERTS, body, 0)

    @pl.when(i == 0)
    def _():
        for_runs(i, lambda cp: cp.start())

    @pl.when(i + 1 < pl.num_programs(0))
    def _():
        for_runs(i + 1, lambda cp: cp.start())

    for_runs(i, lambda cp: cp.wait())

    lpos = lpos_ref[...]
    gates = gates_ref[...]
    slot = lax.broadcasted_iota(jnp.int32, (tm, rows), 1)
    weights = jnp.zeros((tm, rows), F32)
    for k in range(TOP_K_EXPERTS):
        weights = weights + jnp.where(slot == lpos[:, k:k + 1], gates[:, k:k + 1], 0.0)
    out = x1_ref[...] + _dot(weights.astype(BF16), buf_ref[lax.rem(i, 2)].astype(BF16))
    o_ref[...] = _rmsnorm(out, fn_ref[...])


def _combine(y, lpos, gates, x1, final_norm, plan):
    n, d = x1.shape
    tm = min(TM_POST, n)
    assert n % tm == 0
    row = lambda width: pl.BlockSpec((tm, width), lambda i, *_: (i, 0))
    return pl.pallas_call(
        _combine_body,
        grid_spec=pltpu.PrefetchScalarGridSpec(
            num_scalar_prefetch=3,
            grid=(n // tm,),
            in_specs=[
                row(ROUTER_PAD), row(ROUTER_PAD), row(d),
                pl.BlockSpec((1, d), lambda i, *_: (0, 0)),
                pl.BlockSpec(memory_space=pl.ANY),
            ],
            out_specs=row(d),
            scratch_shapes=[pltpu.VMEM((2, _staging_rows(tm), d), F32),
                            pltpu.SemaphoreType.DMA((2,))],
        ),
        out_shape=jax.ShapeDtypeStruct((n, d), F32),
        compiler_params=_params(1),
        name="combine",
    )(plan["run_dst"], plan["run_len"], plan["run_off"], lpos, gates, x1,
      final_norm.reshape(1, d).astype(F32), y)


def _routing_plan(runs, counts, n_tiles):
    counts = counts[0, :N_EXPERTS].astype(jnp.int32)
    padded = ((counts + TM_EXP - 1) // TM_EXP) * TM_EXP
    ends = jnp.cumsum(padded)
    starts = ends - padded
    runs = runs[:, :, :N_EXPERTS].astype(jnp.int32)
    flat = lambda a: a.reshape(-1).astype(jnp.int32)
    tile_row0 = jnp.arange(n_tiles, dtype=jnp.int32) * TM_EXP
    tile_expert = jnp.minimum(jnp.sum(tile_row0[:, None] >= ends[None, :], axis=1),
                              N_EXPERTS - 1).astype(jnp.int32)
    tile_rows = jnp.clip(counts[tile_expert] - (tile_row0 - starts[tile_expert]), 0, TM_EXP)
    used = tile_row0 < ends[-1]
    tile_rows = jnp.where(used, tile_rows, 0).astype(jnp.int32)
    changed = jnp.concatenate([jnp.ones((1,), bool), tile_expert[1:] != tile_expert[:-1]])
    ids = jnp.where(counts > 0, jnp.arange(N_EXPERTS, dtype=jnp.int32), N_EXPERTS)
    later = jnp.concatenate([lax.cummin(ids, reverse=True)[1:],
                             jnp.full((1,), N_EXPERTS, jnp.int32)])
    return {
        "tile_expert": tile_expert,
        "tile_rows": tile_rows,
        "first_of_expert": (used & changed).astype(jnp.int32),
        "next_expert": jnp.where(later < N_EXPERTS, later, -1)[tile_expert].astype(jnp.int32),
        "last_tile": jnp.maximum(ends[-1] // TM_EXP - 1, 0).astype(jnp.int32).reshape(1),
        "tail_start": (starts + counts).astype(jnp.int32),
        "tail_len": (padded - counts).astype(jnp.int32),
        "run_dst": flat(starts[None, :] + runs[:, 0, :]),
        "run_len": flat(runs[:, 1, :]),
        "run_off": flat(runs[:, 2, :]),
    }


def kernel(x, mix_norm, w_in, pool_w, pool_scale, w_branch_pool, w_branch_attn, rel_bias, w_out,
           ffn_norm, w_router, b_router, w1, b1, w2, b2, final_norm):
    batch, seq, d = x.shape
    n = batch * seq
    depth = mix_norm.shape[0]
    assert depth == 1, "the combine kernel fuses the final norm, so only one layer is supported"
    token_tiles = n // min(TM_POST, n)
    max_rows = (n * TOP_K_EXPERTS + N_EXPERTS * token_tiles * (ROW_ALIGN - 1)
                + N_EXPERTS * (TM_EXP - 1))
    n_tiles = (max_rows + TM_EXP - 1) // TM_EXP
    xf = x.reshape(n, d)
    for l in range(depth):
        qT, k, vT, qiT, ki, wiT, pp, g1 = _inproj(
            xf, mix_norm[l], w_in[l], pool_w[l], pool_scale[l], w_branch_pool[l], seq)
        attn = _attention(qT, k, vT, qiT, ki, wiT, rel_bias, batch, seq)
        x1, h2, lpos, gates, runs, counts = _post_attn(
            attn, pp, g1, xf, w_branch_attn[l], w_out[l], ffn_norm[l], w_router[l], b_router[l])
        plan = _routing_plan(runs, counts, n_tiles)
        xs = _dispatch(h2, lpos, plan, n_tiles * TM_EXP)
        y = _experts(xs, plan["tile_expert"], plan["tile_rows"], plan["first_of_expert"],
                     plan["next_expert"], w1[l], b1[l], w2[l], b2[l])
        xf = _combine(y, lpos, gates, x1, final_norm, plan)
    return xf.reshape(batch, seq, d)
```

```python
import functools
import math

import jax
import jax.numpy as jnp
import numpy as np
from jax import lax
from jax.experimental import pallas as pl
from jax.experimental.pallas import tpu as pltpu

D_MODEL = 1024
POOL_WIDTH = 512
POOL_WINDOWS = (2, 4, 8, 16)
POOL_GROUPS = len(POOL_WINDOWS)
POOL_GROUP_WIDTH = POOL_WIDTH // POOL_GROUPS
N_HEADS = 8
HEAD_DIM = 64
ATTN_WIDTH = N_HEADS * HEAD_DIM
N_IDX_HEADS = 4
IDX_DIM = 64
IDX_SCALE = (IDX_DIM ** -0.5) * (N_IDX_HEADS ** -0.5)
ATTN_SCALE = HEAD_DIM ** -0.5
TOPK_MAX = 256
REL_BUCKETS = 32
REL_MAX_DIST = 128
N_BRANCHES = 2
N_EXPERTS = 32
TOP_K_EXPERTS = 4
D_FF = D_MODEL
SWIGLU_LIMIT = 7.0
SWIGLU_ALPHA = 1.702
RMS_EPS = 1e-5
SPLIT_SIZES = (POOL_WIDTH, ATTN_WIDTH, ATTN_WIDTH, ATTN_WIDTH,
               N_IDX_HEADS * IDX_DIM, IDX_DIM, N_IDX_HEADS, N_BRANCHES * D_MODEL)

LANES = 128
SUBLANES = 8
VMEM_LIMIT_BYTES = 56 * 1024 * 1024

TM_IN = 512
TQ = 256
TK = 256
TM_POST = 512
TM_EXP = 512
ROW_ALIGN = SUBLANES
PERM_CHUNK = 256
POOL_HALO = 16
N_BISECT_BF16 = 10
N_BISECT_F32 = 8
BF16_ROWS = 2 * SUBLANES
BF16_STEP = 2.0 ** -7
TINY = 1e-30
PAIR = 2 * HEAD_DIM
IDX_PAD = LANES
ROUTER_PAD = LANES

F32 = jnp.float32
BF16 = jnp.bfloat16
NEG_INF = float("-inf")
M_INIT = -1e30
LOG2E = math.log2(math.e)


def _dot(a, b):
    return jnp.dot(a, b, preferred_element_type=F32)


def _dot_nt(a, b):
    return lax.dot_general(a, b, (((1,), (1,)), ((), ())), preferred_element_type=F32)


def _rmsnorm(x, g):
    ms = jnp.mean(x * x, axis=-1, keepdims=True)
    return x * lax.rsqrt(ms + RMS_EPS) * g


def _const_spec(shape):
    nd = len(shape)
    return pl.BlockSpec(shape, lambda *_: (0,) * nd)


def _params(n_axes):
    return pltpu.CompilerParams(
        dimension_semantics=("arbitrary",) * n_axes,
        vmem_limit_bytes=VMEM_LIMIT_BYTES)


_ROW_SECTIONS = (("pool", POOL_WIDTH), ("k", ATTN_WIDTH), ("ki", IDX_PAD),
                 ("g0", D_MODEL), ("g1", D_MODEL))
_COL_SECTIONS = (("q", ATTN_WIDTH), ("v", ATTN_WIDTH), ("qi", N_IDX_HEADS * IDX_PAD),
                 ("wi", 2 * SUBLANES))


def _section(sections, name):
    start = 0
    for key, width in sections:
        if key == name:
            return slice(start, start + width)
        start += width
    raise KeyError(name)


def _inproj_body(x_ref, g_ref, wrow_ref, wcol_ref, poolw_ref, pscale_ref, wbp_ref,
                 qT_ref, k_ref, vT_ref, qiT_ref, ki_ref, wiT_ref, pp_ref, g1_ref,
                 halo_ref, *, tiles_per_seq):
    i = pl.program_id(0)
    tm = x_ref.shape[0]
    h = _rmsnorm(x_ref[...], g_ref[...]).astype(BF16)
    row_w = lambda name: wrow_ref[:, _section(_ROW_SECTIONS, name)]
    col_w = lambda name: wcol_ref[_section(_COL_SECTIONS, name), :]

    qT = (_dot_nt(col_w("q"), h) * (ATTN_SCALE * LOG2E)).astype(BF16)
    vT = _dot_nt(col_w("v"), h).astype(BF16)
    qiT = _dot_nt(col_w("qi"), h).astype(BF16)
    wiT = _dot_nt(col_w("wi"), h) * IDX_SCALE
    for j in range(tm // TQ):
        qT_ref[j] = qT[:, j * TQ:(j + 1) * TQ]
        qiT_ref[j] = qiT[:, j * TQ:(j + 1) * TQ]
        wiT_ref[j] = wiT[:SUBLANES, j * TQ:(j + 1) * TQ]
    for j in range(tm // TK):
        vT_ref[j] = vT[:, j * TK:(j + 1) * TK]
    k_ref[...] = _dot(h, row_w("k")).astype(BF16)
    ki_ref[...] = _dot(h, row_w("ki")).astype(BF16)
    g1_ref[...] = jax.nn.sigmoid(_dot(h, row_w("g1"))).astype(BF16)

    zp = _dot(h, row_w("pool"))
    seq_tile = lax.rem(i, tiles_per_seq)

    @pl.when(seq_tile == 0)
    def _():
        halo_ref[...] = jnp.zeros_like(halo_ref)

    zext = jnp.concatenate([halo_ref[...], zp], axis=0)
    halo_ref[...] = zp[tm - POOL_HALO:, :]
    gw = POOL_GROUP_WIDTH
    s2 = zext + pltpu.roll(zext, 1, 0)
    s4 = s2[:, gw:] + pltpu.roll(s2[:, gw:], 2, 0)
    s8 = s4[:, gw:] + pltpu.roll(s4[:, gw:], 4, 0)
    s16 = s8[:, gw:] + pltpu.roll(s8[:, gw:], 8, 0)
    wsum = (s2[POOL_HALO:, :gw], s4[POOL_HALO:, :gw], s8[POOL_HALO:, :gw], s16[POOL_HALO:, :])
    t = seq_tile * tm + lax.broadcasted_iota(jnp.int32, (tm, 1), 0)
    mixed = []
    for g, w in enumerate(POOL_WINDOWS):
        cnt = jnp.minimum(t + 1, w).astype(F32)
        pooled = wsum[g] / cnt - zp[:, g * gw:(g + 1) * gw]
        mixed.append(_dot(pooled.astype(BF16), poolw_ref[g]) * pscale_ref[:, g * gw:(g + 1) * gw])
    mixed = jnp.concatenate(mixed, axis=1).astype(BF16)
    y_pool = _dot(mixed, wbp_ref[...])
    gate0 = jax.nn.sigmoid(_dot(h, row_w("g0")))
    pp_ref[...] = (gate0 * y_pool).astype(BF16)


def _inproj(xf, mix_norm, w_in, pool_w, pool_scale, w_branch_pool, seq):
    n, d = xf.shape
    tm = min(TM_IN, seq)
    assert seq % tm == 0 and tm % TK == 0 and tm % TQ == 0 and n % tm == 0
    offs = [0] + [int(o) for o in np.cumsum(SPLIT_SIZES)]
    z_pool, z_q, z_k, z_v, z_qi, z_ki, z_wi, z_gate = (
        w_in[:, offs[j]:offs[j + 1]] for j in range(len(SPLIT_SIZES)))
    pad_cols = lambda a, width: jnp.pad(a, ((0, 0), (0, width - a.shape[1])))
    qi_heads = jnp.pad(z_qi.reshape(d, N_IDX_HEADS, IDX_DIM),
                       ((0, 0), (0, 0), (0, IDX_PAD - IDX_DIM))).reshape(d, N_IDX_HEADS * IDX_PAD)
    parts = {"pool": z_pool, "k": z_k, "ki": pad_cols(z_ki, IDX_PAD),
             "g0": z_gate[:, :D_MODEL], "g1": z_gate[:, D_MODEL:],
             "q": z_q, "v": z_v, "qi": qi_heads, "wi": pad_cols(z_wi, 2 * SUBLANES)}
    w_row = jnp.concatenate([parts[k] for k, _ in _ROW_SECTIONS], axis=1).astype(BF16)
    w_col = jnp.concatenate([parts[k] for k, _ in _COL_SECTIONS], axis=1).astype(BF16).T
    consts = [mix_norm.reshape(1, d).astype(F32), w_row, w_col, pool_w.astype(BF16),
              pool_scale.reshape(1, POOL_WIDTH).astype(F32), w_branch_pool.astype(BF16)]
    grid = (n // tm,)
    row = lambda width: pl.BlockSpec((tm, width), lambda i: (i, 0))
    tiles = lambda t, height: pl.BlockSpec((tm // t, height, t), lambda i: (i, 0, 0))
    out_shape = [
        jax.ShapeDtypeStruct((n // TQ, ATTN_WIDTH, TQ), BF16),
        jax.ShapeDtypeStruct((n, ATTN_WIDTH), BF16),
        jax.ShapeDtypeStruct((n // TK, ATTN_WIDTH, TK), BF16),
        jax.ShapeDtypeStruct((n // TQ, N_IDX_HEADS * IDX_PAD, TQ), BF16),
        jax.ShapeDtypeStruct((n, IDX_PAD), BF16),
        jax.ShapeDtypeStruct((n // TQ, SUBLANES, TQ), F32),
        jax.ShapeDtypeStruct((n, D_MODEL), BF16),
        jax.ShapeDtypeStruct((n, D_MODEL), BF16),
    ]
    out_specs = [
        tiles(TQ, ATTN_WIDTH),
        row(ATTN_WIDTH),
        tiles(TK, ATTN_WIDTH),
        tiles(TQ, N_IDX_HEADS * IDX_PAD),
        row(IDX_PAD),
        tiles(TQ, SUBLANES),
        row(D_MODEL),
        row(D_MODEL),
    ]
    return pl.pallas_call(
        functools.partial(_inproj_body, tiles_per_seq=seq // tm),
        grid=grid,
        in_specs=[row(d)] + [_const_spec(c.shape) for c in consts],
        out_specs=out_specs,
        out_shape=out_shape,
        scratch_shapes=[pltpu.VMEM((POOL_HALO, POOL_WIDTH), F32)],
        compiler_params=_params(1),
        name="inproj",
    )(xf, *consts)


def _rel_thresholds():
    n = np.arange(0, 4 * REL_MAX_DIST)
    max_exact = REL_BUCKETS // 2
    nf = np.maximum(n, 1).astype(np.float32)
    large = max_exact + (np.log(nf / np.float32(max_exact))
                         / np.float32(math.log(REL_MAX_DIST / max_exact))
                         * np.float32(REL_BUCKETS - max_exact)).astype(np.int32)
    bucket = np.where(n < max_exact, n, np.minimum(large, REL_BUCKETS - 1))
    assert np.all(np.diff(bucket) >= 0) and np.all(np.diff(bucket) <= 1)
    assert bucket[-1] == REL_BUCKETS - 1
    return [int(np.argmax(bucket >= b)) for b in range(1, REL_BUCKETS)]


def _attn_body(table_ref, qT_ref, qiT_ref, wiT_ref, k_ref, ki_ref, vT_ref, o_ref,
               score_ref, sb_ref, band_ref, tri_ref, qm_ref, m_ref, l_ref, acc_ref, thr_ref,
               need_ref,
               *, topk):
    b = pl.program_id(0)
    qi = pl.program_id(1)
    tq = o_ref.shape[0]
    nk = qi + 1
    n_keys = score_ref.shape[0] * TK
    key_i = lax.broadcasted_iota(jnp.int32, (TK, tq), 0)
    qry_i = lax.broadcasted_iota(jnp.int32, (TK, tq), 1)

    @pl.when((b == 0) & (qi == 0))
    def _():
        r_i = lax.broadcasted_iota(jnp.int32, (TK, TK), 0)
        c_i = lax.broadcasted_iota(jnp.int32, (TK, TK), 1)
        tri_ref[...] = jnp.where(c_i < r_i, 1.0, 0.0).astype(BF16)
        thresholds = _rel_thresholds()
        assert thresholds[-1] <= TK
        for part in range(2):
            dist = qry_i - key_i + (1 - part) * TK
            for h in range(N_HEADS):
                bias = jnp.full((TK, tq), table_ref[h], F32)
                for bkt, thr in enumerate(thresholds, start=1):
                    bias = jnp.where(dist >= thr, table_ref[bkt * N_HEADS + h], bias)
                band_ref[h, part] = jnp.where(dist < 0, NEG_INF, bias * LOG2E)

    first_half = lax.broadcasted_iota(jnp.int32, (PAIR, tq), 0) < HEAD_DIM
    for hp in range(N_HEADS // 2):
        qp = qT_ref[0, hp * PAIR:(hp + 1) * PAIR, :]
        zero = jnp.zeros_like(qp)
        qm_ref[2 * hp] = jnp.where(first_half, qp, zero)
        qm_ref[2 * hp + 1] = jnp.where(first_half, zero, qp)

    wiT = wiT_ref[0]

    n_pairs = (nk + 1) // 2

    def score_pair(j, carry):
        mx, mn = carry
        tiles = (2 * j, 2 * j + 1)
        heads = [[_dot(ki_ref[pl.ds(pl.multiple_of(kj * TK, TK), TK), :],
                       qiT_ref[0, h * IDX_PAD:(h + 1) * IDX_PAD, :])
                  for h in range(N_IDX_HEADS)] for kj in tiles]
        for kj, dots in zip(tiles, heads):
            sc = jnp.maximum(dots[0], 0.0) * wiT[0:1, :]
            for h in range(1, N_IDX_HEADS):
                sc = sc + jnp.maximum(dots[h], 0.0) * wiT[h:h + 1, :]
            causal = (kj * TK + key_i) <= (qi * tq + qry_i)
            masked = jnp.where(causal, sc, NEG_INF)
            score_ref[kj] = masked
            sb_ref[kj] = masked.astype(BF16)
            mx = jnp.maximum(mx, jnp.max(sc, axis=0, keepdims=True))
            mn = jnp.minimum(mn, jnp.min(sc, axis=0, keepdims=True))
        return mx, mn

    row_max, row_min = lax.fori_loop(
        0, n_pairs, score_pair,
        (jnp.full((1, tq), NEG_INF, F32), jnp.full((1, tq), -NEG_INF, F32)))

    def fold(x, op):
        return op(x.reshape(TK // SUBLANES, SUBLANES, tq), axis=0)

    def count_where(pred):
        def body(j, acc):
            for kj in (2 * j, 2 * j + 1):
                acc = acc + fold(jnp.where(pred(score_ref[kj]), 1.0, 0.0), jnp.sum)
            return acc
        acc = lax.fori_loop(0, n_pairs, body, jnp.zeros((SUBLANES, tq), F32))
        return jnp.sum(acc, axis=0, keepdims=True)

    def max_where(pred):
        def body(j, acc):
            for kj in (2 * j, 2 * j + 1):
                s = score_ref[kj]
                acc = jnp.maximum(acc, fold(jnp.where(pred(s), s, NEG_INF), jnp.max))
            return acc
        acc = lax.fori_loop(0, n_pairs, body, jnp.full((SUBLANES, tq), NEG_INF, F32))
        return jnp.max(acc, axis=0, keepdims=True)

    kf = float(topk)
    thr_ref[...] = jnp.full((1, tq), NEG_INF, F32)
    need_ref[...] = jnp.full((1, tq), float(n_keys), F32)

    @pl.when(qi * tq + 1 > topk)
    def _():
        def count_above_bf16(mid_b):
            mid_t = jnp.broadcast_to(mid_b, (TK, tq))
            one, zero = jnp.ones((), BF16), jnp.zeros((), BF16)

            def body(j, acc):
                for kj in (2 * j, 2 * j + 1):
                    m = jnp.where(sb_ref[kj] > mid_t, one, zero)
                    parts = [m[r * BF16_ROWS:(r + 1) * BF16_ROWS]
                             for r in range(TK // BF16_ROWS)]
                    while len(parts) > 1:
                        parts = [a + b for a, b in zip(parts[::2], parts[1::2])]
                    acc = acc + parts[0].astype(F32)
                return acc
            acc = lax.fori_loop(0, n_pairs, body, jnp.zeros((BF16_ROWS, tq), F32))
            return jnp.sum(acc, axis=0, keepdims=True)

        def widen(v, sign):
            return v + sign * (jnp.abs(v) * BF16_STEP + TINY)

        def bisect_bf16(_, carry):
            lo, hi = carry
            mid_b = (0.5 * (lo + hi)).astype(BF16)
            above = count_above_bf16(mid_b) >= kf
            mid = mid_b.astype(F32)
            return jnp.where(above, mid, lo), jnp.where(above, hi, mid)

        lo, hi = lax.fori_loop(0, N_BISECT_BF16, bisect_bf16,
                               (widen(row_min, -1.0), widen(row_max, 1.0)))

        def bisect(_, carry):
            lo, hi = carry
            mid = 0.5 * (lo + hi)
            above = count_where(lambda s: s > mid) >= kf
            return jnp.where(above, mid, lo), jnp.where(above, hi, mid)

        _, hi = lax.fori_loop(0, N_BISECT_F32, bisect, (widen(lo, -1.0), widen(hi, 1.0)))
        cand = max_where(lambda s: s <= hi)
        n_ge = count_where(lambda s: s >= cand)

        def unresolved(state):
            it, _, n_ge = state
            return (jnp.min(n_ge) < kf) & (it < n_keys)

        def step(state):
            it, cand, n_ge = state
            nxt = max_where(lambda s: s < cand)
            n_nxt = count_where(lambda s: s >= nxt)
            open_ = n_ge < kf
            return it + 1, jnp.where(open_, nxt, cand), jnp.where(open_, n_nxt, n_ge)

        _, thr, n_ge = lax.while_loop(unresolved, step, (jnp.int32(0), cand, n_ge))
        thr_ref[...] = thr

        @pl.when(jnp.max(n_ge) > kf)
        def _():
            need_ref[...] = kf - count_where(lambda s: s > thr)

    m_ref[...] = jnp.full(m_ref.shape, M_INIT, F32)
    l_ref[...] = jnp.zeros(l_ref.shape, F32)
    acc_ref[...] = jnp.zeros(acc_ref.shape, F32)
    thr = thr_ref[...]
    need = need_ref[...]

    far_bias = [table_ref[(REL_BUCKETS - 1) * N_HEADS + h] * LOG2E for h in range(N_HEADS)]

    def attend(kjs, ties_before, near):
        logits = []
        for kj in kjs:
            k_t = k_ref[pl.ds(pl.multiple_of(kj * TK, TK), TK), :]
            logits.append([_dot(k_t[:, (h // 2) * PAIR:(h // 2 + 1) * PAIR], qm_ref[h])
                           for h in range(N_HEADS)])
        for kj, tile_logits in zip(kjs, logits):
            sc = score_ref[kj]
            tied = jnp.where(sc == thr, 1.0, 0.0)
            rank = _dot(tri_ref[...], tied.astype(BF16)) + ties_before
            ties_before = rank[TK - 1:, :] + tied[TK - 1:, :]
            sel_bias = jnp.where(
                sc > thr, 0.0,
                jnp.where(sc == thr, jnp.where(rank < need, 0.0, NEG_INF), NEG_INF))
            vT_t = vT_ref[kj]
            probs, alphas = [], []
            for h in range(N_HEADS):
                m_old = m_ref[h]
                if near:
                    s = tile_logits[h] + band_ref[h, kj - qi + 1] + sel_bias
                    m_new = jnp.maximum(m_old, jnp.max(s, axis=0, keepdims=True))
                    p = jnp.exp2(s - m_new)
                else:
                    s = tile_logits[h] + sel_bias
                    m_new = jnp.maximum(m_old, jnp.max(s, axis=0, keepdims=True) + far_bias[h])
                    p = jnp.exp2(s - (m_new - far_bias[h]))
                alpha = jnp.exp2(m_old - m_new)
                l_ref[h] = alpha * l_ref[h] + jnp.sum(p, axis=0, keepdims=True)
                m_ref[h] = m_new
                probs.append(p.astype(BF16))
                alphas.append(alpha)
            for h in range(N_HEADS):
                out_h = _dot(vT_t[h * HEAD_DIM:(h + 1) * HEAD_DIM, :], probs[h])
                acc_ref[h] = acc_ref[h] * alphas[h] + out_h
        return ties_before

    n_far = jnp.maximum(qi - 1, 0)
    far_pairs = n_far // 2
    ties = lax.fori_loop(0, far_pairs,
                         lambda j, t: attend((2 * j, 2 * j + 1), t, near=False),
                         jnp.zeros((1, tq), F32))
    ties = lax.fori_loop(2 * far_pairs, n_far, lambda kj, t: attend((kj,), t, near=False), ties)
    lax.fori_loop(n_far, nk, lambda kj, t: attend((kj,), t, near=True), ties)
    for hp in range(N_HEADS // 2):
        pair = jnp.concatenate([acc_ref[2 * hp + e] / l_ref[2 * hp + e] for e in range(2)], axis=0)
        o_ref[:, hp * PAIR:(hp + 1) * PAIR] = pair.T.astype(o_ref.dtype)


def _attention(qT, k, vT, qiT, ki, wiT, rel_bias, batch, seq):
    n = k.shape[0]
    tq = min(TQ, seq)
    assert tq == TQ == TK and seq % tq == 0
    topk = min(TOPK_MAX, seq // 4)
    assert topk == tq or seq == tq
    nq = seq // tq
    nkt = seq // TK
    assert nkt % 2 == 0
    table = rel_bias.astype(F32).reshape(REL_BUCKETS * N_HEADS)
    qtile = lambda height: pl.BlockSpec((1, height, tq), lambda b, i: (b * nq + i, 0, 0))
    return pl.pallas_call(
        functools.partial(_attn_body, topk=topk),
        grid=(batch, nq),
        in_specs=[
            pl.BlockSpec(memory_space=pltpu.SMEM),
            qtile(ATTN_WIDTH),
            qtile(N_IDX_HEADS * IDX_PAD),
            qtile(SUBLANES),
            pl.BlockSpec((seq, ATTN_WIDTH), lambda b, i: (b, 0)),
            pl.BlockSpec((seq, IDX_PAD), lambda b, i: (b, 0)),
            pl.BlockSpec((nkt, ATTN_WIDTH, TK), lambda b, i: (b, 0, 0)),
        ],
        out_specs=pl.BlockSpec((tq, ATTN_WIDTH), lambda b, i: (b * nq + i, 0)),
        out_shape=jax.ShapeDtypeStruct((n, ATTN_WIDTH), BF16),
        scratch_shapes=[
            pltpu.VMEM((nkt, TK, tq), F32),
            pltpu.VMEM((nkt, TK, tq), BF16),
            pltpu.VMEM((N_HEADS, 2, TK, tq), F32),
            pltpu.VMEM((TK, TK), BF16),
            pltpu.VMEM((N_HEADS, PAIR, tq), BF16),
            pltpu.VMEM((N_HEADS, 1, tq), F32),
            pltpu.VMEM((N_HEADS, 1, tq), F32),
            pltpu.VMEM((N_HEADS, HEAD_DIM, tq), F32),
            pltpu.VMEM((1, tq), F32),
            pltpu.VMEM((1, tq), F32),
        ],
        compiler_params=_params(2),
        name="attention",
    )(table, qT, qiT, wiT, k, ki, vT)


def _post_body(attn_ref, pp_ref, g1_ref, x_ref, wba_ref, wout_ref, fg_ref, wr_hi_ref, wr_lo_ref,
               br_ref, x1_ref, h2_ref, lpos_ref, gates_ref, runs_ref, counts_ref, carry_ref):
    i = pl.program_id(0)
    tm = x_ref.shape[0]
    y_attn = _dot(attn_ref[...], wba_ref[...])
    merged = pp_ref[...].astype(F32) + g1_ref[...].astype(F32) * y_attn
    x1 = x_ref[...] + _dot(merged.astype(BF16), wout_ref[...])
    x1_ref[...] = x1
    h2 = _rmsnorm(x1, fg_ref[...])
    h2_ref[...] = h2

    h_hi = h2.astype(BF16)
    h_lo = (h2 - h_hi.astype(F32)).astype(BF16)
    logits = (_dot(h_hi, wr_hi_ref[...]) + _dot(h_hi, wr_lo_ref[...])
              + _dot(h_lo, wr_hi_ref[...]) + br_ref[...])

    lane = lax.broadcasted_iota(jnp.int32, (tm, ROUTER_PAD), 1)
    work = logits
    vals, idxs = [], []
    for _ in range(TOP_K_EXPERTS):
        mx = jnp.max(work, axis=1, keepdims=True)
        ix = jnp.min(jnp.where(work == mx, lane, ROUTER_PAD), axis=1, keepdims=True)
        vals.append(mx)
        idxs.append(ix)
        work = jnp.where(lane == ix, NEG_INF, work)
    exps = [jnp.exp(v - vals[0]) for v in vals]
    denom = exps[0] + exps[1] + exps[2] + exps[3]

    member = jnp.zeros((tm, ROUTER_PAD), F32)
    for ix in idxs:
        member = member + jnp.where(lane == ix, 1.0, 0.0)

    @pl.when(i == 0)
    def _():
        carry_ref[...] = jnp.zeros_like(carry_ref)

    r_i = lax.broadcasted_iota(jnp.int32, (tm, tm), 0)
    c_i = lax.broadcasted_iota(jnp.int32, (tm, tm), 1)
    strict_lower = jnp.where(c_i < r_i, 1.0, 0.0).astype(BF16)
    local = _dot(strict_lower, member.astype(BF16))
    count = jnp.sum(member, axis=0, keepdims=True)
    run_len = jnp.floor((count + (ROW_ALIGN - 1)) * (1.0 / ROW_ALIGN)) * ROW_ALIGN
    e_r = lax.broadcasted_iota(jnp.int32, (ROUTER_PAD, ROUTER_PAD), 0)
    e_c = lax.broadcasted_iota(jnp.int32, (ROUTER_PAD, ROUTER_PAD), 1)
    strict_upper = jnp.where(e_r < e_c, 1.0, 0.0).astype(BF16)
    run_off = _dot(jnp.broadcast_to(run_len, (SUBLANES, ROUTER_PAD)).astype(BF16),
                   strict_upper)[:1, :]
    run_start = carry_ref[...]
    carry_new = run_start + run_len
    carry_ref[...] = carry_new
    counts_ref[...] = jnp.broadcast_to(carry_new, counts_ref.shape)
    sub = lax.broadcasted_iota(jnp.int32, (SUBLANES, ROUTER_PAD), 0)
    runs_ref[0] = jnp.where(sub == 0, run_start, jnp.where(sub == 1, run_len,
                                                          jnp.where(sub == 2, run_off, 0.0)))

    slot = run_off + local
    lpos = jnp.zeros((tm, ROUTER_PAD), jnp.int32)
    gates = jnp.zeros((tm, ROUTER_PAD), F32)
    for k in range(TOP_K_EXPERTS):
        pos = jnp.sum(jnp.where(lane == idxs[k], slot, 0.0), axis=1, keepdims=True)
        lpos = jnp.where(lane == k, pos.astype(jnp.int32), lpos)
        gates = jnp.where(lane == k, exps[k] / denom, gates)
    lpos_ref[...] = lpos
    gates_ref[...] = gates


def _post_attn(attn, pp, g1, xf, w_branch_attn, w_out, ffn_norm, w_router, b_router):
    n, d = xf.shape
    tm = min(TM_POST, n)
    assert n % tm == 0
    wr = jnp.pad(w_router.astype(F32), ((0, 0), (0, ROUTER_PAD - N_EXPERTS)))
    wr_hi = wr.astype(BF16)
    wr_lo = (wr - wr_hi.astype(F32)).astype(BF16)
    br = jnp.pad(b_router.astype(F32), (0, ROUTER_PAD - N_EXPERTS),
                 constant_values=NEG_INF).reshape(1, ROUTER_PAD)
    consts = [w_branch_attn.astype(BF16), w_out.astype(BF16),
              ffn_norm.reshape(1, d).astype(F32), wr_hi, wr_lo, br]
    row = lambda width: pl.BlockSpec((tm, width), lambda i: (i, 0))
    return pl.pallas_call(
        _post_body,
        grid=(n // tm,),
        in_specs=[row(ATTN_WIDTH), row(d), row(d), row(d)] + [_const_spec(c.shape) for c in consts],
        out_specs=[row(d), row(d), row(ROUTER_PAD), row(ROUTER_PAD),
                   pl.BlockSpec((1, SUBLANES, ROUTER_PAD), lambda i: (i, 0, 0)),
                   _const_spec((SUBLANES, ROUTER_PAD))],
        out_shape=[
            jax.ShapeDtypeStruct((n, d), F32),
            jax.ShapeDtypeStruct((n, d), F32),
            jax.ShapeDtypeStruct((n, ROUTER_PAD), jnp.int32),
            jax.ShapeDtypeStruct((n, ROUTER_PAD), F32),
            jax.ShapeDtypeStruct((n // tm, SUBLANES, ROUTER_PAD), F32),
            jax.ShapeDtypeStruct((SUBLANES, ROUTER_PAD), F32),
        ],
        scratch_shapes=[pltpu.VMEM((1, ROUTER_PAD), F32)],
        compiler_params=_params(1),
        name="post_attn",
    )(attn, pp, g1, xf, *consts)


def _block_copies(length, make_copy, max_block):
    block = max_block
    while block >= ROW_ALIGN:
        offset = pl.multiple_of(jnp.bitwise_and(length, -2 * block), ROW_ALIGN)
        yield jnp.bitwise_and(length, block) != 0, make_copy(offset, block)
        block //= 2


def _for_each_block(length, make_copy, max_block, action):
    for pred, cp in _block_copies(length, make_copy, max_block):
        @pl.when(pred)
        def _(cp=cp):
            action(cp)


def _staging_rows(tm):
    return TOP_K_EXPERTS * tm + N_EXPERTS * ROW_ALIGN


def _dispatch_body(run_dst_ref, run_len_ref, run_off_ref, tail_start_ref, tail_len_ref,
                   last_tile_ref, lpos_ref, h2_ref, xs_ref, buf_ref, zero_ref, sem, zsem,
                   *, first_tail_tile):
    i = pl.program_id(0)
    tm = h2_ref.shape[0]
    n_tiles = xs_ref.shape[0] // TM_EXP
    rows = buf_ref.shape[1]

    lpos_t = lpos_ref[...].T
    tokens = h2_ref[...].astype(BF16)
    half_now = lax.rem(i, 2)

    def permute_chunk(c, carry):
        r0 = pl.multiple_of(c * PERM_CHUNK, PERM_CHUNK)
        slot = r0 + lax.broadcasted_iota(jnp.int32, (PERM_CHUNK, tm), 0)
        onehot = jnp.where(slot == lpos_t[0:1, :], 1.0, 0.0)
        for k in range(1, TOP_K_EXPERTS):
            onehot = onehot + jnp.where(slot == lpos_t[k:k + 1, :], 1.0, 0.0)
        buf_ref[half_now, pl.ds(r0, PERM_CHUNK), :] = _dot(onehot.astype(BF16), tokens)
        return carry

    def run_copy(tile, e):
        base = tile * N_EXPERTS + e
        half = lax.rem(tile, 2)
        src0 = pl.multiple_of(run_off_ref[base], ROW_ALIGN)
        dst0 = pl.multiple_of(run_dst_ref[base], ROW_ALIGN)
        return run_len_ref[base], lambda off, blk: pltpu.make_async_copy(
            buf_ref.at[half, pl.ds(src0 + off, blk)], xs_ref.at[pl.ds(dst0 + off, blk)],
            sem.at[half])

    def for_runs(tile, action):
        def body(e, carry):
            length, make = run_copy(tile, e)
            _for_each_block(length, make, tm, action)
            return carry
        lax.fori_loop(0, N_EXPERTS, body, 0)

    @pl.when(i >= 2)
    def _():
        for_runs(i - 2, lambda cp: cp.wait())

    lax.fori_loop(0, rows // PERM_CHUNK, permute_chunk, 0)
    for_runs(i, lambda cp: cp.start())

    def zero_fill(action):
        def tail_copy(e):
            dst0 = pl.multiple_of(tail_start_ref[e], ROW_ALIGN)
            return tail_len_ref[e], lambda off, blk: pltpu.make_async_copy(
                zero_ref.at[pl.ds(0, blk)], xs_ref.at[pl.ds(dst0 + off, blk)], zsem)

        def fill_tail(e, carry):
            length, make = tail_copy(e)
            _for_each_block(length, make, TM_EXP // 2, action)
            return carry

        lax.fori_loop(0, N_EXPERTS, fill_tail, 0)

        def tail_tile(t, carry):
            @pl.when(t > last_tile_ref[0])
            def _():
                action(pltpu.make_async_copy(
                    zero_ref, xs_ref.at[pl.ds(pl.multiple_of(t * TM_EXP, TM_EXP), TM_EXP)], zsem))
            return carry

        lax.fori_loop(first_tail_tile, n_tiles, tail_tile, 0)

    @pl.when(i == 0)
    def _():
        zero_ref[...] = jnp.zeros_like(zero_ref)
        zero_fill(lambda cp: cp.start())

    @pl.when(i == pl.num_programs(0) - 1)
    def _():
        @pl.when(i >= 1)
        def _():
            for_runs(i - 1, lambda cp: cp.wait())
        for_runs(i, lambda cp: cp.wait())
        zero_fill(lambda cp: cp.wait())


def _dispatch(h2, lpos, plan, n_rows):
    n, d = h2.shape
    tm = min(TM_POST, n)
    assert n % tm == 0 and tm % ROW_ALIGN == 0 and _staging_rows(tm) % PERM_CHUNK == 0
    return pl.pallas_call(
        functools.partial(_dispatch_body, first_tail_tile=(n * TOP_K_EXPERTS) // TM_EXP),
        grid_spec=pltpu.PrefetchScalarGridSpec(
            num_scalar_prefetch=6,
            grid=(n // tm,),
            in_specs=[
                pl.BlockSpec((tm, ROUTER_PAD), lambda i, *_: (i, 0)),
                pl.BlockSpec((tm, d), lambda i, *_: (i, 0)),
            ],
            out_specs=pl.BlockSpec(memory_space=pl.ANY),
            scratch_shapes=[pltpu.VMEM((2, _staging_rows(tm), d), F32),
                            pltpu.VMEM((TM_EXP, d), F32),
                            pltpu.SemaphoreType.DMA((2,)), pltpu.SemaphoreType.DMA],
        ),
        out_shape=jax.ShapeDtypeStruct((n_rows, d), F32),
        compiler_params=_params(1),
        name="dispatch",
    )(plan["run_dst"], plan["run_len"], plan["run_off"], plan["tail_start"], plan["tail_len"],
      plan["last_tile"], lpos, h2)


def _experts_body(tile_expert_ref, tile_rows_ref, run_start_ref, next_expert_ref,
                  xs_ref, w1_hbm, b1_ref, w2_hbm, b2_ref, y_ref,
                  w1f_ref, w2f_ref, w1b_ref, w2b_ref, wsem):
    i = pl.program_id(0)

    def fetch(expert):
        return (pltpu.make_async_copy(w1_hbm.at[expert], w1f_ref, wsem.at[0]),
                pltpu.make_async_copy(w2_hbm.at[expert], w2f_ref, wsem.at[1]))

    @pl.when(i == 0)
    def _():
        for cp in fetch(tile_expert_ref[0]):
            cp.start()

    @pl.when(run_start_ref[i] == 1)
    def _():
        for cp in fetch(tile_expert_ref[i]):
            cp.wait()
        w1b_ref[...] = w1f_ref[...].astype(BF16)
        w2b_ref[...] = w2f_ref[...].astype(BF16)

        @pl.when(next_expert_ref[i] >= 0)
        def _():
            for cp in fetch(next_expert_ref[i]):
                cp.start()

    @pl.when(tile_rows_ref[i] > 0)
    def _():
        x = xs_ref[...].astype(BF16)
        gu = _dot(x, w1b_ref[...]) + b1_ref[0]
        g = jnp.minimum(gu[:, :D_FF], SWIGLU_LIMIT)
        u = jnp.clip(gu[:, D_FF:], -SWIGLU_LIMIT, SWIGLU_LIMIT)
        act = g * jax.nn.sigmoid(SWIGLU_ALPHA * g) * (u + 1.0)
        y_ref[...] = _dot(act.astype(BF16), w2b_ref[...]) + b2_ref[0]

    @pl.when(tile_rows_ref[i] == 0)
    def _():
        y_ref[...] = jnp.zeros_like(y_ref)


def _experts(xs, tile_expert, tile_rows, run_start, next_expert, w1, b1, w2, b2):
    n_rows, d = xs.shape
    n_tiles = n_rows // TM_EXP
    tile = lambda i, *_: (i, 0)
    per_expert = lambda i, te, *_: (te[i], 0, 0)
    return pl.pallas_call(
        _experts_body,
        grid_spec=pltpu.PrefetchScalarGridSpec(
            num_scalar_prefetch=4,
            grid=(n_tiles,),
            in_specs=[
                pl.BlockSpec((TM_EXP, d), tile),
                pl.BlockSpec(memory_space=pl.ANY),
                pl.BlockSpec((1, 1, 2 * D_FF), per_expert),
                pl.BlockSpec(memory_space=pl.ANY),
                pl.BlockSpec((1, 1, d), per_expert),
            ],
            out_specs=pl.BlockSpec((TM_EXP, d), tile),
            scratch_shapes=[
                pltpu.VMEM((d, 2 * D_FF), F32), pltpu.VMEM((D_FF, d), F32),
                pltpu.VMEM((d, 2 * D_FF), BF16), pltpu.VMEM((D_FF, d), BF16),
                pltpu.SemaphoreType.DMA((2,)),
            ],
        ),
        out_shape=jax.ShapeDtypeStruct((n_rows, d), F32),
        compiler_params=_params(1),
        name="experts",
    )(tile_expert, tile_rows, run_start, next_expert, xs, w1,
      b1.reshape(N_EXPERTS, 1, 2 * D_FF), w2, b2.reshape(N_EXPERTS, 1, d))


def _combine_body(run_dst_ref, run_len_ref, run_off_ref, lpos_ref, gates_ref, x1_ref, fn_ref,
                  y_ref, o_ref, buf_ref, sem):
    i = pl.program_id(0)
    tm = x1_ref.shape[0]
    rows = buf_ref.shape[1]

    @pl.when(i == 0)
    def _():
        buf_ref[...] = jnp.zeros_like(buf_ref)

    def run_copy(tile, e):
        base = tile * N_EXPERTS + e
        half = lax.rem(tile, 2)
        src0 = pl.multiple_of(run_dst_ref[base], ROW_ALIGN)
        dst0 = pl.multiple_of(run_off_ref[base], ROW_ALIGN)
        return run_len_ref[base], lambda off, blk: pltpu.make_async_copy(
            y_ref.at[pl.ds(src0 + off, blk)], buf_ref.at[half, pl.ds(dst0 + off, blk)],
            sem.at[half])

    def for_runs(tile, action):
        def body(e, carry):
            length, make = run_copy(tile, e)
            _for_each_block(length, make, tm, action)
            return carry
        lax.fori_loop(0, N_EXPERTS, body, 0)

    @pl.when(i == 0)
    def _():
        for_runs(i, lambda cp: cp.start())

    @pl.when(i + 1 < pl.num_programs(0))
    def _():
        for_runs(i + 1, lambda cp: cp.start())

    for_runs(i, lambda cp: cp.wait())

    lpos = lpos_ref[...]
    gates = gates_ref[...]
    slot = lax.broadcasted_iota(jnp.int32, (tm, rows), 1)
    weights = jnp.zeros((tm, rows), F32)
    for k in range(TOP_K_EXPERTS):
        weights = weights + jnp.where(slot == lpos[:, k:k + 1], gates[:, k:k + 1], 0.0)
    out = x1_ref[...] + _dot(weights.astype(BF16), buf_ref[lax.rem(i, 2)].astype(BF16))
    o_ref[...] = _rmsnorm(out, fn_ref[...])


def _combine(y, lpos, gates, x1, final_norm, plan):
    n, d = x1.shape
    tm = min(TM_POST, n)
    assert n % tm == 0
    row = lambda width: pl.BlockSpec((tm, width), lambda i, *_: (i, 0))
    return pl.pallas_call(
        _combine_body,
        grid_spec=pltpu.PrefetchScalarGridSpec(
            num_scalar_prefetch=3,
            grid=(n // tm,),
            in_specs=[
                row(ROUTER_PAD), row(ROUTER_PAD), row(d),
                pl.BlockSpec((1, d), lambda i, *_: (0, 0)),
                pl.BlockSpec(memory_space=pl.ANY),
            ],
            out_specs=row(d),
            scratch_shapes=[pltpu.VMEM((2, _staging_rows(tm), d), F32),
                            pltpu.SemaphoreType.DMA((2,))],
        ),
        out_shape=jax.ShapeDtypeStruct((n, d), F32),
        compiler_params=_params(1),
        name="combine",
    )(plan["run_dst"], plan["run_len"], plan["run_off"], lpos, gates, x1,
      final_norm.reshape(1, d).astype(F32), y)


def _routing_plan(runs, counts, n_tiles):
    counts = counts[0, :N_EXPERTS].astype(jnp.int32)
    padded = ((counts + TM_EXP - 1) // TM_EXP) * TM_EXP
    ends = jnp.cumsum(padded)
    starts = ends - padded
    runs = runs[:, :, :N_EXPERTS].astype(jnp.int32)
    flat = lambda a: a.reshape(-1).astype(jnp.int32)
    tile_row0 = jnp.arange(n_tiles, dtype=jnp.int32) * TM_EXP
    tile_expert = jnp.minimum(jnp.sum(tile_row0[:, None] >= ends[None, :], axis=1),
                              N_EXPERTS - 1).astype(jnp.int32)
    tile_rows = jnp.clip(counts[tile_expert] - (tile_row0 - starts[tile_expert]), 0, TM_EXP)
    used = tile_row0 < ends[-1]
    tile_rows = jnp.where(used, tile_rows, 0).astype(jnp.int32)
    changed = jnp.concatenate([jnp.ones((1,), bool), tile_expert[1:] != tile_expert[:-1]])
    ids = jnp.where(counts > 0, jnp.arange(N_EXPERTS, dtype=jnp.int32), N_EXPERTS)
    later = jnp.concatenate([lax.cummin(ids, reverse=True)[1:],
                             jnp.full((1,), N_EXPERTS, jnp.int32)])
    return {
        "tile_expert": tile_expert,
        "tile_rows": tile_rows,
        "first_of_expert": (used & changed).astype(jnp.int32),
        "next_expert": jnp.where(later < N_EXPERTS, later, -1)[tile_expert].astype(jnp.int32),
        "last_tile": jnp.maximum(ends[-1] // TM_EXP - 1, 0).astype(jnp.int32).reshape(1),
        "tail_start": (starts + counts).astype(jnp.int32),
        "tail_len": (padded - counts).astype(jnp.int32),
        "run_dst": flat(starts[None, :] + runs[:, 0, :]),
        "run_len": flat(runs[:, 1, :]),
        "run_off": flat(runs[:, 2, :]),
    }


def kernel(x, mix_norm, w_in, pool_w, pool_scale, w_branch_pool, w_branch_attn, rel_bias, w_out,
           ffn_norm, w_router, b_router, w1, b1, w2, b2, final_norm):
    batch, seq, d = x.shape
    n = batch * seq
    depth = mix_norm.shape[0]
    assert depth == 1, "the combine kernel fuses the final norm, so only one layer is supported"
    token_tiles = n // min(TM_POST, n)
    max_rows = (n * TOP_K_EXPERTS + N_EXPERTS * token_tiles * (ROW_ALIGN - 1)
                + N_EXPERTS * (TM_EXP - 1))
    n_tiles = (max_rows + TM_EXP - 1) // TM_EXP
    xf = x.reshape(n, d)
    for l in range(depth):
        qT, k, vT, qiT, ki, wiT, pp, g1 = _inproj(
            xf, mix_norm[l], w_in[l], pool_w[l], pool_scale[l], w_branch_pool[l], seq)
        attn = _attention(qT, k, vT, qiT, ki, wiT, rel_bias, batch, seq)
        x1, h2, lpos, gates, runs, counts = _post_attn(
            attn, pp, g1, xf, w_branch_attn[l], w_out[l], ffn_norm[l], w_router[l], b_router[l])
        plan = _routing_plan(runs, counts, n_tiles)
        xs = _dispatch(h2, lpos, plan, n_tiles * TM_EXP)
        y = _experts(xs, plan["tile_expert"], plan["tile_rows"], plan["first_of_expert"],
                     plan["next_expert"], w1[l], b1[l], w2[l], b2[l])
        xf = _combine(y, lpos, gates, x1, final_norm, plan)
    return xf.reshape(batch, seq, d)
```

```python
import functools
import math

import jax
import jax.numpy as jnp
import numpy as np
from jax import lax
from jax.experimental import pallas as pl
from jax.experimental.pallas import tpu as pltpu

D_MODEL = 1024
POOL_WIDTH = 512
POOL_WINDOWS = (2, 4, 8, 16)
POOL_GROUPS = len(POOL_WINDOWS)
POOL_GROUP_WIDTH = POOL_WIDTH // POOL_GROUPS
N_HEADS = 8
HEAD_DIM = 64
ATTN_WIDTH = N_HEADS * HEAD_DIM
N_IDX_HEADS = 4
IDX_DIM = 64
IDX_SCALE = (IDX_DIM ** -0.5) * (N_IDX_HEADS ** -0.5)
ATTN_SCALE = HEAD_DIM ** -0.5
TOPK_MAX = 256
REL_BUCKETS = 32
REL_MAX_DIST = 128
N_BRANCHES = 2
N_EXPERTS = 32
TOP_K_EXPERTS = 4
D_FF = D_MODEL
SWIGLU_LIMIT = 7.0
SWIGLU_ALPHA = 1.702
RMS_EPS = 1e-5
SPLIT_SIZES = (POOL_WIDTH, ATTN_WIDTH, ATTN_WIDTH, ATTN_WIDTH,
               N_IDX_HEADS * IDX_DIM, IDX_DIM, N_IDX_HEADS, N_BRANCHES * D_MODEL)

LANES = 128
SUBLANES = 8
VMEM_LIMIT_BYTES = 56 * 1024 * 1024

TM_IN = 512
TQ = 256
TK = 256
TM_POST = 512
TM_EXP = 512
ROW_ALIGN = SUBLANES
PERM_CHUNK = 1152
POOL_HALO = 16
N_BISECT_BF16 = 10
N_BISECT_F32 = 8
BF16_ROWS = 2 * SUBLANES
BF16_STEP = 2.0 ** -7
TINY = 1e-30
PAIR = 2 * HEAD_DIM
IDX_PAD = LANES
ROUTER_PAD = LANES

F32 = jnp.float32
BF16 = jnp.bfloat16
NEG_INF = float("-inf")
M_INIT = -1e30
LOG2E = math.log2(math.e)


def _dot(a, b):
    return jnp.dot(a, b, preferred_element_type=F32)


def _dot_nt(a, b):
    return lax.dot_general(a, b, (((1,), (1,)), ((), ())), preferred_element_type=F32)


def _rmsnorm(x, g):
    ms = jnp.mean(x * x, axis=-1, keepdims=True)
    return x * lax.rsqrt(ms + RMS_EPS) * g


def _const_spec(shape):
    nd = len(shape)
    return pl.BlockSpec(shape, lambda *_: (0,) * nd)


def _params(n_axes):
    return pltpu.CompilerParams(
        dimension_semantics=("arbitrary",) * n_axes,
        vmem_limit_bytes=VMEM_LIMIT_BYTES)


_ROW_SECTIONS = (("pool", POOL_WIDTH), ("k", ATTN_WIDTH), ("ki", IDX_PAD),
                 ("g0", D_MODEL), ("g1", D_MODEL))
_COL_SECTIONS = (("q", ATTN_WIDTH), ("v", ATTN_WIDTH), ("qi", N_IDX_HEADS * IDX_PAD),
                 ("wi", 2 * SUBLANES))


def _section(sections, name):
    start = 0
    for key, width in sections:
        if key == name:
            return slice(start, start + width)
        start += width
    raise KeyError(name)


def _inproj_body(x_ref, g_ref, wrow_ref, wcol_ref, poolw_ref, pscale_ref, wbp_ref,
                 qT_ref, k_ref, vT_ref, qiT_ref, ki_ref, wiT_ref, pp_ref, g1_ref,
                 halo_ref, *, tiles_per_seq):
    i = pl.program_id(0)
    tm = x_ref.shape[0]
    h = _rmsnorm(x_ref[...], g_ref[...]).astype(BF16)
    row_w = lambda name: wrow_ref[:, _section(_ROW_SECTIONS, name)]
    col_w = lambda name: wcol_ref[_section(_COL_SECTIONS, name), :]

    qT = (_dot_nt(col_w("q"), h) * (ATTN_SCALE * LOG2E)).astype(BF16)
    vT = _dot_nt(col_w("v"), h).astype(BF16)
    qiT = _dot_nt(col_w("qi"), h).astype(BF16)
    wiT = _dot_nt(col_w("wi"), h) * IDX_SCALE
    for j in range(tm // TQ):
        qT_ref[j] = qT[:, j * TQ:(j + 1) * TQ]
        qiT_ref[j] = qiT[:, j * TQ:(j + 1) * TQ]
        wiT_ref[j] = wiT[:SUBLANES, j * TQ:(j + 1) * TQ]
    for j in range(tm // TK):
        vT_ref[j] = vT[:, j * TK:(j + 1) * TK]
    k_ref[...] = _dot(h, row_w("k")).astype(BF16)
    ki_ref[...] = _dot(h, row_w("ki")).astype(BF16)
    g1_ref[...] = jax.nn.sigmoid(_dot(h, row_w("g1"))).astype(BF16)

    zp = _dot(h, row_w("pool"))
    seq_tile = lax.rem(i, tiles_per_seq)

    @pl.when(seq_tile == 0)
    def _():
        halo_ref[...] = jnp.zeros_like(halo_ref)

    zext = jnp.concatenate([halo_ref[...], zp], axis=0)
    halo_ref[...] = zp[tm - POOL_HALO:, :]
    gw = POOL_GROUP_WIDTH
    s2 = zext + pltpu.roll(zext, 1, 0)
    s4 = s2[:, gw:] + pltpu.roll(s2[:, gw:], 2, 0)
    s8 = s4[:, gw:] + pltpu.roll(s4[:, gw:], 4, 0)
    s16 = s8[:, gw:] + pltpu.roll(s8[:, gw:], 8, 0)
    wsum = (s2[POOL_HALO:, :gw], s4[POOL_HALO:, :gw], s8[POOL_HALO:, :gw], s16[POOL_HALO:, :])
    t = seq_tile * tm + lax.broadcasted_iota(jnp.int32, (tm, 1), 0)
    mixed = []
    for g, w in enumerate(POOL_WINDOWS):
        cnt = jnp.minimum(t + 1, w).astype(F32)
        pooled = wsum[g] / cnt - zp[:, g * gw:(g + 1) * gw]
        mixed.append(_dot(pooled.astype(BF16), poolw_ref[g]) * pscale_ref[:, g * gw:(g + 1) * gw])
    mixed = jnp.concatenate(mixed, axis=1).astype(BF16)
    y_pool = _dot(mixed, wbp_ref[...])
    gate0 = jax.nn.sigmoid(_dot(h, row_w("g0")))
    pp_ref[...] = (gate0 * y_pool).astype(BF16)


def _inproj(xf, mix_norm, w_in, pool_w, pool_scale, w_branch_pool, seq):
    n, d = xf.shape
    tm = min(TM_IN, seq)
    assert seq % tm == 0 and tm % TK == 0 and tm % TQ == 0 and n % tm == 0
    offs = [0] + [int(o) for o in np.cumsum(SPLIT_SIZES)]
    z_pool, z_q, z_k, z_v, z_qi, z_ki, z_wi, z_gate = (
        w_in[:, offs[j]:offs[j + 1]] for j in range(len(SPLIT_SIZES)))
    pad_cols = lambda a, width: jnp.pad(a, ((0, 0), (0, width - a.shape[1])))
    qi_heads = jnp.pad(z_qi.reshape(d, N_IDX_HEADS, IDX_DIM),
                       ((0, 0), (0, 0), (0, IDX_PAD - IDX_DIM))).reshape(d, N_IDX_HEADS * IDX_PAD)
    parts = {"pool": z_pool, "k": z_k, "ki": pad_cols(z_ki, IDX_PAD),
             "g0": z_gate[:, :D_MODEL], "g1": z_gate[:, D_MODEL:],
             "q": z_q, "v": z_v, "qi": qi_heads, "wi": pad_cols(z_wi, 2 * SUBLANES)}
    w_row = jnp.concatenate([parts[k] for k, _ in _ROW_SECTIONS], axis=1).astype(BF16)
    w_col = jnp.concatenate([parts[k] for k, _ in _COL_SECTIONS], axis=1).astype(BF16).T
    consts = [mix_norm.reshape(1, d).astype(F32), w_row, w_col, pool_w.astype(BF16),
              pool_scale.reshape(1, POOL_WIDTH).astype(F32), w_branch_pool.astype(BF16)]
    grid = (n // tm,)
    row = lambda width: pl.BlockSpec((tm, width), lambda i: (i, 0))
    tiles = lambda t, height: pl.BlockSpec((tm // t, height, t), lambda i: (i, 0, 0))
    out_shape = [
        jax.ShapeDtypeStruct((n // TQ, ATTN_WIDTH, TQ), BF16),
        jax.ShapeDtypeStruct((n, ATTN_WIDTH), BF16),
        jax.ShapeDtypeStruct((n // TK, ATTN_WIDTH, TK), BF16),
        jax.ShapeDtypeStruct((n // TQ, N_IDX_HEADS * IDX_PAD, TQ), BF16),
        jax.ShapeDtypeStruct((n, IDX_PAD), BF16),
        jax.ShapeDtypeStruct((n // TQ, SUBLANES, TQ), F32),
        jax.ShapeDtypeStruct((n, D_MODEL), BF16),
        jax.ShapeDtypeStruct((n, D_MODEL), BF16),
    ]
    out_specs = [
        tiles(TQ, ATTN_WIDTH),
        row(ATTN_WIDTH),
        tiles(TK, ATTN_WIDTH),
        tiles(TQ, N_IDX_HEADS * IDX_PAD),
        row(IDX_PAD),
        tiles(TQ, SUBLANES),
        row(D_MODEL),
        row(D_MODEL),
    ]
    return pl.pallas_call(
        functools.partial(_inproj_body, tiles_per_seq=seq // tm),
        grid=grid,
        in_specs=[row(d)] + [_const_spec(c.shape) for c in consts],
        out_specs=out_specs,
        out_shape=out_shape,
        scratch_shapes=[pltpu.VMEM((POOL_HALO, POOL_WIDTH), F32)],
        compiler_params=_params(1),
        name="inproj",
    )(xf, *consts)


def _rel_thresholds():
    n = np.arange(0, 4 * REL_MAX_DIST)
    max_exact = REL_BUCKETS // 2
    nf = np.maximum(n, 1).astype(np.float32)
    large = max_exact + (np.log(nf / np.float32(max_exact))
                         / np.float32(math.log(REL_MAX_DIST / max_exact))
                         * np.float32(REL_BUCKETS - max_exact)).astype(np.int32)
    bucket = np.where(n < max_exact, n, np.minimum(large, REL_BUCKETS - 1))
    assert np.all(np.diff(bucket) >= 0) and np.all(np.diff(bucket) <= 1)
    assert bucket[-1] == REL_BUCKETS - 1
    return [int(np.argmax(bucket >= b)) for b in range(1, REL_BUCKETS)]


def _attn_body(table_ref, qT_ref, qiT_ref, wiT_ref, k_ref, ki_ref, vT_ref, o_ref,
               score_ref, sb_ref, band_ref, tri_ref, qm_ref, m_ref, l_ref, acc_ref, thr_ref,
               need_ref,
               *, topk):
    b = pl.program_id(0)
    qi = pl.program_id(1)
    tq = o_ref.shape[0]
    nk = qi + 1
    n_keys = score_ref.shape[0] * TK
    key_i = lax.broadcasted_iota(jnp.int32, (TK, tq), 0)
    qry_i = lax.broadcasted_iota(jnp.int32, (TK, tq), 1)

    @pl.when((b == 0) & (qi == 0))
    def _():
        r_i = lax.broadcasted_iota(jnp.int32, (TK, TK), 0)
        c_i = lax.broadcasted_iota(jnp.int32, (TK, TK), 1)
        tri_ref[...] = jnp.where(c_i < r_i, 1.0, 0.0).astype(BF16)
        thresholds = _rel_thresholds()
        assert thresholds[-1] <= TK
        for part in range(2):
            dist = qry_i - key_i + (1 - part) * TK
            for h in range(N_HEADS):
                bias = jnp.full((TK, tq), table_ref[h], F32)
                for bkt, thr in enumerate(thresholds, start=1):
                    bias = jnp.where(dist >= thr, table_ref[bkt * N_HEADS + h], bias)
                band_ref[h, part] = jnp.where(dist < 0, NEG_INF, bias * LOG2E)

    first_half = lax.broadcasted_iota(jnp.int32, (PAIR, tq), 0) < HEAD_DIM
    for hp in range(N_HEADS // 2):
        qp = qT_ref[0, hp * PAIR:(hp + 1) * PAIR, :]
        zero = jnp.zeros_like(qp)
        qm_ref[2 * hp] = jnp.where(first_half, qp, zero)
        qm_ref[2 * hp + 1] = jnp.where(first_half, zero, qp)

    wiT = wiT_ref[0]

    n_pairs = (nk + 1) // 2

    def score_pair(j, carry):
        mx, mn = carry
        tiles = (2 * j, 2 * j + 1)
        heads = [[_dot(ki_ref[pl.ds(pl.multiple_of(kj * TK, TK), TK), :],
                       qiT_ref[0, h * IDX_PAD:(h + 1) * IDX_PAD, :])
                  for h in range(N_IDX_HEADS)] for kj in tiles]
        for kj, dots in zip(tiles, heads):
            sc = jnp.maximum(dots[0], 0.0) * wiT[0:1, :]
            for h in range(1, N_IDX_HEADS):
                sc = sc + jnp.maximum(dots[h], 0.0) * wiT[h:h + 1, :]
            causal = (kj * TK + key_i) <= (qi * tq + qry_i)
            masked = jnp.where(causal, sc, NEG_INF)
            score_ref[kj] = masked
            sb_ref[kj] = masked.astype(BF16)
            mx = jnp.maximum(mx, jnp.max(sc, axis=0, keepdims=True))
            mn = jnp.minimum(mn, jnp.min(sc, axis=0, keepdims=True))
        return mx, mn

    row_max, row_min = lax.fori_loop(
        0, n_pairs, score_pair,
        (jnp.full((1, tq), NEG_INF, F32), jnp.full((1, tq), -NEG_INF, F32)))

    def fold(x, op):
        return op(x.reshape(TK // SUBLANES, SUBLANES, tq), axis=0)

    def count_where(pred):
        def body(j, acc):
            for kj in (2 * j, 2 * j + 1):
                acc = acc + fold(jnp.where(pred(score_ref[kj]), 1.0, 0.0), jnp.sum)
            return acc
        acc = lax.fori_loop(0, n_pairs, body, jnp.zeros((SUBLANES, tq), F32))
        return jnp.sum(acc, axis=0, keepdims=True)

    def max_where(pred):
        def body(j, acc):
            for kj in (2 * j, 2 * j + 1):
                s = score_ref[kj]
                acc = jnp.maximum(acc, fold(jnp.where(pred(s), s, NEG_INF), jnp.max))
            return acc
        acc = lax.fori_loop(0, n_pairs, body, jnp.full((SUBLANES, tq), NEG_INF, F32))
        return jnp.max(acc, axis=0, keepdims=True)

    kf = float(topk)
    thr_ref[...] = jnp.full((1, tq), NEG_INF, F32)
    need_ref[...] = jnp.full((1, tq), float(n_keys), F32)

    @pl.when(qi * tq + 1 > topk)
    def _():
        def count_above_bf16(mid_b):
            mid_t = jnp.broadcast_to(mid_b, (TK, tq))
            one, zero = jnp.ones((), BF16), jnp.zeros((), BF16)

            def body(j, acc):
                for kj in (2 * j, 2 * j + 1):
                    m = jnp.where(sb_ref[kj] > mid_t, one, zero)
                    parts = [m[r * BF16_ROWS:(r + 1) * BF16_ROWS]
                             for r in range(TK // BF16_ROWS)]
                    while len(parts) > 1:
                        parts = [a + b for a, b in zip(parts[::2], parts[1::2])]
                    acc = acc + parts[0].astype(F32)
                return acc
            acc = lax.fori_loop(0, n_pairs, body, jnp.zeros((BF16_ROWS, tq), F32))
            return jnp.sum(acc, axis=0, keepdims=True)

        def widen(v, sign):
            return v + sign * (jnp.abs(v) * BF16_STEP + TINY)

        def bisect_bf16(_, carry):
            lo, hi = carry
            mid_b = (0.5 * (lo + hi)).astype(BF16)
            above = count_above_bf16(mid_b) >= kf
            mid = mid_b.astype(F32)
            return jnp.where(above, mid, lo), jnp.where(above, hi, mid)

        lo, hi = lax.fori_loop(0, N_BISECT_BF16, bisect_bf16,
                               (widen(row_min, -1.0), widen(row_max, 1.0)))

        def bisect(_, carry):
            lo, hi = carry
            mid = 0.5 * (lo + hi)
            above = count_where(lambda s: s > mid) >= kf
            return jnp.where(above, mid, lo), jnp.where(above, hi, mid)

        _, hi = lax.fori_loop(0, N_BISECT_F32, bisect, (widen(lo, -1.0), widen(hi, 1.0)))
        cand = max_where(lambda s: s <= hi)
        n_ge = count_where(lambda s: s >= cand)

        def unresolved(state):
            it, _, n_ge = state
            return (jnp.min(n_ge) < kf) & (it < n_keys)

        def step(state):
            it, cand, n_ge = state
            nxt = max_where(lambda s: s < cand)
            n_nxt = count_where(lambda s: s >= nxt)
            open_ = n_ge < kf
            return it + 1, jnp.where(open_, nxt, cand), jnp.where(open_, n_nxt, n_ge)

        _, thr, n_ge = lax.while_loop(unresolved, step, (jnp.int32(0), cand, n_ge))
        thr_ref[...] = thr

        @pl.when(jnp.max(n_ge) > kf)
        def _():
            need_ref[...] = kf - count_where(lambda s: s > thr)

    m_ref[...] = jnp.full(m_ref.shape, M_INIT, F32)
    l_ref[...] = jnp.zeros(l_ref.shape, F32)
    acc_ref[...] = jnp.zeros(acc_ref.shape, F32)
    thr = thr_ref[...]
    need = need_ref[...]

    far_bias = [table_ref[(REL_BUCKETS - 1) * N_HEADS + h] * LOG2E for h in range(N_HEADS)]

    def attend(kjs, ties_before, near):
        logits = []
        for kj in kjs:
            k_t = k_ref[pl.ds(pl.multiple_of(kj * TK, TK), TK), :]
            logits.append([_dot(k_t[:, (h // 2) * PAIR:(h // 2 + 1) * PAIR], qm_ref[h])
                           for h in range(N_HEADS)])
        for kj, tile_logits in zip(kjs, logits):
            sc = score_ref[kj]
            tied = jnp.where(sc == thr, 1.0, 0.0)
            rank = _dot(tri_ref[...], tied.astype(BF16)) + ties_before
            ties_before = rank[TK - 1:, :] + tied[TK - 1:, :]
            sel_bias = jnp.where(
                sc > thr, 0.0,
                jnp.where(sc == thr, jnp.where(rank < need, 0.0, NEG_INF), NEG_INF))
            vT_t = vT_ref[kj]
            probs, alphas = [], []
            for h in range(N_HEADS):
                m_old = m_ref[h]
                if near:
                    s = tile_logits[h] + band_ref[h, kj - qi + 1] + sel_bias
                    m_new = jnp.maximum(m_old, jnp.max(s, axis=0, keepdims=True))
                    p = jnp.exp2(s - m_new)
                else:
                    s = tile_logits[h] + sel_bias
                    m_new = jnp.maximum(m_old, jnp.max(s, axis=0, keepdims=True) + far_bias[h])
                    p = jnp.exp2(s - (m_new - far_bias[h]))
                alpha = jnp.exp2(m_old - m_new)
                l_ref[h] = alpha * l_ref[h] + jnp.sum(p, axis=0, keepdims=True)
                m_ref[h] = m_new
                probs.append(p.astype(BF16))
                alphas.append(alpha)
            for h in range(N_HEADS):
                out_h = _dot(vT_t[h * HEAD_DIM:(h + 1) * HEAD_DIM, :], probs[h])
                acc_ref[h] = acc_ref[h] * alphas[h] + out_h
        return ties_before

    n_far = jnp.maximum(qi - 1, 0)
    far_pairs = n_far // 2
    ties = lax.fori_loop(0, far_pairs,
                         lambda j, t: attend((2 * j, 2 * j + 1), t, near=False),
                         jnp.zeros((1, tq), F32))
    ties = lax.fori_loop(2 * far_pairs, n_far, lambda kj, t: attend((kj,), t, near=False), ties)

    @pl.when(qi == 0)
    def _():
        attend((qi,), ties, near=True)

    @pl.when(qi > 0)
    def _():
        attend((qi - 1, qi), ties, near=True)
    for hp in range(N_HEADS // 2):
        pair = jnp.concatenate([acc_ref[2 * hp + e] / l_ref[2 * hp + e] for e in range(2)], axis=0)
        o_ref[:, hp * PAIR:(hp + 1) * PAIR] = pair.T.astype(o_ref.dtype)


def _attention(qT, k, vT, qiT, ki, wiT, rel_bias, batch, seq):
    n = k.shape[0]
    tq = min(TQ, seq)
    assert tq == TQ == TK and seq % tq == 0
    topk = min(TOPK_MAX, seq // 4)
    assert topk == tq or seq == tq
    nq = seq // tq
    nkt = seq // TK
    assert nkt % 2 == 0
    table = rel_bias.astype(F32).reshape(REL_BUCKETS * N_HEADS)
    qtile = lambda height: pl.BlockSpec((1, height, tq), lambda b, i: (b * nq + i, 0, 0))
    return pl.pallas_call(
        functools.partial(_attn_body, topk=topk),
        grid=(batch, nq),
        in_specs=[
            pl.BlockSpec(memory_space=pltpu.SMEM),
            qtile(ATTN_WIDTH),
            qtile(N_IDX_HEADS * IDX_PAD),
            qtile(SUBLANES),
            pl.BlockSpec((seq, ATTN_WIDTH), lambda b, i: (b, 0)),
            pl.BlockSpec((seq, IDX_PAD), lambda b, i: (b, 0)),
            pl.BlockSpec((nkt, ATTN_WIDTH, TK), lambda b, i: (b, 0, 0)),
        ],
        out_specs=pl.BlockSpec((tq, ATTN_WIDTH), lambda b, i: (b * nq + i, 0)),
        out_shape=jax.ShapeDtypeStruct((n, ATTN_WIDTH), BF16),
        scratch_shapes=[
            pltpu.VMEM((nkt, TK, tq), F32),
            pltpu.VMEM((nkt, TK, tq), BF16),
            pltpu.VMEM((N_HEADS, 2, TK, tq), F32),
            pltpu.VMEM((TK, TK), BF16),
            pltpu.VMEM((N_HEADS, PAIR, tq), BF16),
            pltpu.VMEM((N_HEADS, 1, tq), F32),
            pltpu.VMEM((N_HEADS, 1, tq), F32),
            pltpu.VMEM((N_HEADS, HEAD_DIM, tq), F32),
            pltpu.VMEM((1, tq), F32),
            pltpu.VMEM((1, tq), F32),
        ],
        compiler_params=_params(2),
        name="attention",
    )(table, qT, qiT, wiT, k, ki, vT)


def _post_body(attn_ref, pp_ref, g1_ref, x_ref, wba_ref, wout_ref, fg_ref, wr_hi_ref, wr_lo_ref,
               br_ref, x1_ref, h2_ref, lpos_ref, gates_ref, runs_ref, counts_ref, carry_ref):
    i = pl.program_id(0)
    tm = x_ref.shape[0]
    y_attn = _dot(attn_ref[...], wba_ref[...])
    merged = pp_ref[...].astype(F32) + g1_ref[...].astype(F32) * y_attn
    x1 = x_ref[...] + _dot(merged.astype(BF16), wout_ref[...])
    x1_ref[...] = x1
    h2 = _rmsnorm(x1, fg_ref[...])
    h2_ref[...] = h2

    h_hi = h2.astype(BF16)
    h_lo = (h2 - h_hi.astype(F32)).astype(BF16)
    logits = (_dot(h_hi, wr_hi_ref[...]) + _dot(h_hi, wr_lo_ref[...])
              + _dot(h_lo, wr_hi_ref[...]) + br_ref[...])

    lane = lax.broadcasted_iota(jnp.int32, (tm, ROUTER_PAD), 1)
    work = logits
    vals, idxs = [], []
    for _ in range(TOP_K_EXPERTS):
        mx = jnp.max(work, axis=1, keepdims=True)
        ix = jnp.min(jnp.where(work == mx, lane, ROUTER_PAD), axis=1, keepdims=True)
        vals.append(mx)
        idxs.append(ix)
        work = jnp.where(lane == ix, NEG_INF, work)
    exps = [jnp.exp(v - vals[0]) for v in vals]
    denom = exps[0] + exps[1] + exps[2] + exps[3]

    member = jnp.zeros((tm, ROUTER_PAD), F32)
    for ix in idxs:
        member = member + jnp.where(lane == ix, 1.0, 0.0)

    @pl.when(i == 0)
    def _():
        carry_ref[...] = jnp.zeros_like(carry_ref)

    r_i = lax.broadcasted_iota(jnp.int32, (tm, tm), 0)
    c_i = lax.broadcasted_iota(jnp.int32, (tm, tm), 1)
    strict_lower = jnp.where(c_i < r_i, 1.0, 0.0).astype(BF16)
    local = _dot(strict_lower, member.astype(BF16))
    count = jnp.sum(member, axis=0, keepdims=True)
    run_len = jnp.floor((count + (ROW_ALIGN - 1)) * (1.0 / ROW_ALIGN)) * ROW_ALIGN
    e_r = lax.broadcasted_iota(jnp.int32, (ROUTER_PAD, ROUTER_PAD), 0)
    e_c = lax.broadcasted_iota(jnp.int32, (ROUTER_PAD, ROUTER_PAD), 1)
    strict_upper = jnp.where(e_r < e_c, 1.0, 0.0).astype(BF16)
    run_off = _dot(jnp.broadcast_to(run_len, (SUBLANES, ROUTER_PAD)).astype(BF16),
                   strict_upper)[:1, :]
    run_start = carry_ref[...]
    carry_new = run_start + run_len
    carry_ref[...] = carry_new
    counts_ref[...] = jnp.broadcast_to(carry_new, counts_ref.shape)
    sub = lax.broadcasted_iota(jnp.int32, (SUBLANES, ROUTER_PAD), 0)
    runs_ref[0] = jnp.where(sub == 0, run_start, jnp.where(sub == 1, run_len,
                                                          jnp.where(sub == 2, run_off, 0.0)))

    slot = run_off + local
    lpos = jnp.zeros((tm, ROUTER_PAD), jnp.int32)
    gates = jnp.zeros((tm, ROUTER_PAD), F32)
    for k in range(TOP_K_EXPERTS):
        pos = jnp.sum(jnp.where(lane == idxs[k], slot, 0.0), axis=1, keepdims=True)
        lpos = jnp.where(lane == k, pos.astype(jnp.int32), lpos)
        gates = jnp.where(lane == k, exps[k] / denom, gates)
    lpos_ref[...] = lpos
    gates_ref[...] = gates


def _post_attn(attn, pp, g1, xf, w_branch_attn, w_out, ffn_norm, w_router, b_router):
    n, d = xf.shape
    tm = min(TM_POST, n)
    assert n % tm == 0
    wr = jnp.pad(w_router.astype(F32), ((0, 0), (0, ROUTER_PAD - N_EXPERTS)))
    wr_hi = wr.astype(BF16)
    wr_lo = (wr - wr_hi.astype(F32)).astype(BF16)
    br = jnp.pad(b_router.astype(F32), (0, ROUTER_PAD - N_EXPERTS),
                 constant_values=NEG_INF).reshape(1, ROUTER_PAD)
    consts = [w_branch_attn.astype(BF16), w_out.astype(BF16),
              ffn_norm.reshape(1, d).astype(F32), wr_hi, wr_lo, br]
    row = lambda width: pl.BlockSpec((tm, width), lambda i: (i, 0))
    return pl.pallas_call(
        _post_body,
        grid=(n // tm,),
        in_specs=[row(ATTN_WIDTH), row(d), row(d), row(d)] + [_const_spec(c.shape) for c in consts],
        out_specs=[row(d), row(d), row(ROUTER_PAD), row(ROUTER_PAD),
                   pl.BlockSpec((1, SUBLANES, ROUTER_PAD), lambda i: (i, 0, 0)),
                   _const_spec((SUBLANES, ROUTER_PAD))],
        out_shape=[
            jax.ShapeDtypeStruct((n, d), F32),
            jax.ShapeDtypeStruct((n, d), F32),
            jax.ShapeDtypeStruct((n, ROUTER_PAD), jnp.int32),
            jax.ShapeDtypeStruct((n, ROUTER_PAD), F32),
            jax.ShapeDtypeStruct((n // tm, SUBLANES, ROUTER_PAD), F32),
            jax.ShapeDtypeStruct((SUBLANES, ROUTER_PAD), F32),
        ],
        scratch_shapes=[pltpu.VMEM((1, ROUTER_PAD), F32)],
        compiler_params=_params(1),
        name="post_attn",
    )(attn, pp, g1, xf, *consts)


def _block_copies(length, make_copy, max_block):
    block = max_block
    while block >= ROW_ALIGN:
        offset = pl.multiple_of(jnp.bitwise_and(length, -2 * block), ROW_ALIGN)
        yield jnp.bitwise_and(length, block) != 0, make_copy(offset, block)
        block //= 2


def _for_each_block(length, make_copy, max_block, action):
    for pred, cp in _block_copies(length, make_copy, max_block):
        @pl.when(pred)
        def _(cp=cp):
            action(cp)


def _staging_rows(tm):
    return TOP_K_EXPERTS * tm + N_EXPERTS * ROW_ALIGN


def _dispatch_body(run_dst_ref, run_len_ref, run_off_ref, tail_start_ref, tail_len_ref,
                   last_tile_ref, lpos_ref, h2_ref, xs_ref, buf_ref, zero_ref, sem, zsem,
                   *, first_tail_tile):
    i = pl.program_id(0)
    tm = h2_ref.shape[0]
    n_tiles = xs_ref.shape[0] // TM_EXP
    rows = buf_ref.shape[1]

    lpos_t = lpos_ref[...].T
    tokens = h2_ref[...].astype(BF16)
    half_now = lax.rem(i, 2)

    def permute_chunk(c, carry):
        r0 = pl.multiple_of(c * PERM_CHUNK, PERM_CHUNK)
        slot = r0 + lax.broadcasted_iota(jnp.int32, (PERM_CHUNK, tm), 0)
        onehot = jnp.where(slot == lpos_t[0:1, :], 1.0, 0.0)
        for k in range(1, TOP_K_EXPERTS):
            onehot = onehot + jnp.where(slot == lpos_t[k:k + 1, :], 1.0, 0.0)
        buf_ref[half_now, pl.ds(r0, PERM_CHUNK), :] = _dot(onehot.astype(BF16), tokens)
        return carry

    def run_copy(tile, e):
        base = tile * N_EXPERTS + e
        half = lax.rem(tile, 2)
        src0 = pl.multiple_of(run_off_ref[base], ROW_ALIGN)
        dst0 = pl.multiple_of(run_dst_ref[base], ROW_ALIGN)
        return run_len_ref[base], lambda off, blk: pltpu.make_async_copy(
            buf_ref.at[half, pl.ds(src0 + off, blk)], xs_ref.at[pl.ds(dst0 + off, blk)],
            sem.at[half])

    def for_runs(tile, action):
        def body(e, carry):
            length, make = run_copy(tile, e)
            _for_each_block(length, make, tm, action)
            return carry
        lax.fori_loop(0, N_EXPERTS, body, 0)

    @pl.when(i >= 2)
    def _():
        for_runs(i - 2, lambda cp: cp.wait())

    lax.fori_loop(0, rows // PERM_CHUNK, permute_chunk, 0)
    for_runs(i, lambda cp: cp.start())

    def zero_fill(action):
        def tail_copy(e):
            dst0 = pl.multiple_of(tail_start_ref[e], ROW_ALIGN)
            return tail_len_ref[e], lambda off, blk: pltpu.make_async_copy(
                zero_ref.at[pl.ds(0, blk)], xs_ref.at[pl.ds(dst0 + off, blk)], zsem)

        def fill_tail(e, carry):
            length, make = tail_copy(e)
            _for_each_block(length, make, TM_EXP // 2, action)
            return carry

        lax.fori_loop(0, N_EXPERTS, fill_tail, 0)

        def tail_tile(t, carry):
            @pl.when(t > last_tile_ref[0])
            def _():
                action(pltpu.make_async_copy(
                    zero_ref, xs_ref.at[pl.ds(pl.multiple_of(t * TM_EXP, TM_EXP), TM_EXP)], zsem))
            return carry

        lax.fori_loop(first_tail_tile, n_tiles, tail_tile, 0)

    @pl.when(i == 0)
    def _():
        zero_ref[...] = jnp.zeros_like(zero_ref)
        zero_fill(lambda cp: cp.start())

    @pl.when(i == pl.num_programs(0) - 1)
    def _():
        @pl.when(i >= 1)
        def _():
            for_runs(i - 1, lambda cp: cp.wait())
        for_runs(i, lambda cp: cp.wait())
        zero_fill(lambda cp: cp.wait())


def _dispatch(h2, lpos, plan, n_rows):
    n, d = h2.shape
    tm = min(TM_POST, n)
    assert n % tm == 0 and tm % ROW_ALIGN == 0 and _staging_rows(tm) % PERM_CHUNK == 0
    return pl.pallas_call(
        functools.partial(_dispatch_body, first_tail_tile=(n * TOP_K_EXPERTS) // TM_EXP),
        grid_spec=pltpu.PrefetchScalarGridSpec(
            num_scalar_prefetch=6,
            grid=(n // tm,),
            in_specs=[
                pl.BlockSpec((tm, ROUTER_PAD), lambda i, *_: (i, 0)),
                pl.BlockSpec((tm, d), lambda i, *_: (i, 0)),
            ],
            out_specs=pl.BlockSpec(memory_space=pl.ANY),
            scratch_shapes=[pltpu.VMEM((2, _staging_rows(tm), d), F32),
                            pltpu.VMEM((TM_EXP, d), F32),
                            pltpu.SemaphoreType.DMA((2,)), pltpu.SemaphoreType.DMA],
        ),
        out_shape=jax.ShapeDtypeStruct((n_rows, d), F32),
        compiler_params=_params(1),
        name="dispatch",
    )(plan["run_dst"], plan["run_len"], plan["run_off"], plan["tail_start"], plan["tail_len"],
      plan["last_tile"], lpos, h2)


def _experts_body(tile_expert_ref, tile_rows_ref, run_start_ref, next_expert_ref, last_tile_ref,
                  xs_ref, w1_hbm, b1_ref, w2_hbm, b2_ref, y_ref,
                  w1f_ref, w2f_ref, w1b_ref, w2b_ref, wsem):
    i = pl.program_id(0)

    def fetch(expert):
        return (pltpu.make_async_copy(w1_hbm.at[expert], w1f_ref, wsem.at[0]),
                pltpu.make_async_copy(w2_hbm.at[expert], w2f_ref, wsem.at[1]))

    @pl.when(i == 0)
    def _():
        for cp in fetch(tile_expert_ref[0]):
            cp.start()

    @pl.when(run_start_ref[i] == 1)
    def _():
        for cp in fetch(tile_expert_ref[i]):
            cp.wait()
        w1b_ref[...] = w1f_ref[...].astype(BF16)
        w2b_ref[...] = w2f_ref[...].astype(BF16)

        @pl.when(next_expert_ref[i] >= 0)
        def _():
            for cp in fetch(next_expert_ref[i]):
                cp.start()

    @pl.when(tile_rows_ref[i] > 0)
    def _():
        x = xs_ref[...].astype(BF16)
        gu = _dot(x, w1b_ref[...]) + b1_ref[0]
        g = jnp.minimum(gu[:, :D_FF], SWIGLU_LIMIT)
        u = jnp.clip(gu[:, D_FF:], -SWIGLU_LIMIT, SWIGLU_LIMIT)
        act = g * jax.nn.sigmoid(SWIGLU_ALPHA * g) * (u + 1.0)
        y_ref[...] = _dot(act.astype(BF16), w2b_ref[...]) + b2_ref[0]

    @pl.when(tile_rows_ref[i] == 0)
    def _():
        y_ref[...] = jnp.zeros_like(y_ref)


def _experts(xs, tile_expert, tile_rows, run_start, next_expert, last_tile, w1, b1, w2, b2):
    n_rows, d = xs.shape
    n_tiles = n_rows // TM_EXP
    tile = lambda i, *_: (i, 0)
    tile_in = lambda i, te, tr, rs, ne, lt: (jnp.minimum(i, lt[0]), 0)
    per_expert = lambda i, te, *_: (te[i], 0, 0)
    return pl.pallas_call(
        _experts_body,
        grid_spec=pltpu.PrefetchScalarGridSpec(
            num_scalar_prefetch=5,
            grid=(n_tiles,),
            in_specs=[
                pl.BlockSpec((TM_EXP, d), tile_in),
                pl.BlockSpec(memory_space=pl.ANY),
                pl.BlockSpec((1, 1, 2 * D_FF), per_expert),
                pl.BlockSpec(memory_space=pl.ANY),
                pl.BlockSpec((1, 1, d), per_expert),
            ],
            out_specs=pl.BlockSpec((TM_EXP, d), tile),
            scratch_shapes=[
                pltpu.VMEM((d, 2 * D_FF), F32), pltpu.VMEM((D_FF, d), F32),
                pltpu.VMEM((d, 2 * D_FF), BF16), pltpu.VMEM((D_FF, d), BF16),
                pltpu.SemaphoreType.DMA((2,)),
            ],
        ),
        out_shape=jax.ShapeDtypeStruct((n_rows, d), F32),
        compiler_params=_params(1),
        name="experts",
    )(tile_expert, tile_rows, run_start, next_expert, last_tile, xs, w1,
      b1.reshape(N_EXPERTS, 1, 2 * D_FF), w2, b2.reshape(N_EXPERTS, 1, d))


def _combine_body(run_dst_ref, run_len_ref, run_off_ref, lpos_ref, gates_ref, x1_ref, fn_ref,
                  y_ref, o_ref, buf_ref, sem):
    i = pl.program_id(0)
    tm = x1_ref.shape[0]
    rows = buf_ref.shape[1]

    @pl.when(i == 0)
    def _():
        buf_ref[...] = jnp.zeros_like(buf_ref)

    def run_copy(tile, e):
        base = tile * N_EXPERTS + e
        half = lax.rem(tile, 2)
        src0 = pl.multiple_of(run_dst_ref[base], ROW_ALIGN)
        dst0 = pl.multiple_of(run_off_ref[base], ROW_ALIGN)
        return run_len_ref[base], lambda off, blk: pltpu.make_async_copy(
            y_ref.at[pl.ds(src0 + off, blk)], buf_ref.at[half, pl.ds(dst0 + off, blk)],
            sem.at[half])

    def for_runs(tile, action):
        def body(e, carry):
            length, make = run_copy(tile, e)
            _for_each_block(length, make, tm, action)
            return carry
        lax.fori_loop(0, N_EXPERTS, body, 0)

    @pl.when(i == 0)
    def _():
        for_runs(i, lambda cp: cp.start())

    @pl.when(i + 1 < pl.num_programs(0))
    def _():
        for_runs(i + 1, lambda cp: cp.start())

    for_runs(i, lambda cp: cp.wait())

    lpos = lpos_ref[...]
    gates = gates_ref[...]
    slot = lax.broadcasted_iota(jnp.int32, (tm, rows), 1)
    weights = jnp.zeros((tm, rows), F32)
    for k in range(TOP_K_EXPERTS):
        weights = weights + jnp.where(slot == lpos[:, k:k + 1], gates[:, k:k + 1], 0.0)
    out = x1_ref[...] + _dot(weights.astype(BF16), buf_ref[lax.rem(i, 2)].astype(BF16))
    o_ref[...] = _rmsnorm(out, fn_ref[...])


def _combine(y, lpos, gates, x1, final_norm, plan):
    n, d = x1.shape
    tm = min(TM_POST, n)
    assert n % tm == 0
    row = lambda width: pl.BlockSpec((tm, width), lambda i, *_: (i, 0))
    return pl.pallas_call(
        _combine_body,
        grid_spec=pltpu.PrefetchScalarGridSpec(
            num_scalar_prefetch=3,
            grid=(n // tm,),
            in_specs=[
                row(ROUTER_PAD), row(ROUTER_PAD), row(d),
                pl.BlockSpec((1, d), lambda i, *_: (0, 0)),
                pl.BlockSpec(memory_space=pl.ANY),
            ],
            out_specs=row(d),
            scratch_shapes=[pltpu.VMEM((2, _staging_rows(tm), d), F32),
                            pltpu.SemaphoreType.DMA((2,))],
        ),
        out_shape=jax.ShapeDtypeStruct((n, d), F32),
        compiler_params=_params(1),
        name="combine",
    )(plan["run_dst"], plan["run_len"], plan["run_off"], lpos, gates, x1,
      final_norm.reshape(1, d).astype(F32), y)


def _routing_plan(runs, counts, n_tiles):
    counts = counts[0, :N_EXPERTS].astype(jnp.int32)
    padded = ((counts + TM_EXP - 1) // TM_EXP) * TM_EXP
    ends = jnp.cumsum(padded)
    starts = ends - padded
    runs = runs[:, :, :N_EXPERTS].astype(jnp.int32)
    flat = lambda a: a.reshape(-1).astype(jnp.int32)
    tile_row0 = jnp.arange(n_tiles, dtype=jnp.int32) * TM_EXP
    tile_expert = jnp.minimum(jnp.sum(tile_row0[:, None] >= ends[None, :], axis=1),
                              N_EXPERTS - 1).astype(jnp.int32)
    tile_rows = jnp.clip(counts[tile_expert] - (tile_row0 - starts[tile_expert]), 0, TM_EXP)
    used = tile_row0 < ends[-1]
    tile_rows = jnp.where(used, tile_rows, 0).astype(jnp.int32)
    changed = jnp.concatenate([jnp.ones((1,), bool), tile_expert[1:] != tile_expert[:-1]])
    ids = jnp.where(counts > 0, jnp.arange(N_EXPERTS, dtype=jnp.int32), N_EXPERTS)
    later = jnp.concatenate([lax.cummin(ids, reverse=True)[1:],
                             jnp.full((1,), N_EXPERTS, jnp.int32)])
    return {
        "tile_expert": tile_expert,
        "tile_rows": tile_rows,
        "first_of_expert": (used & changed).astype(jnp.int32),
        "next_expert": jnp.where(later < N_EXPERTS, later, -1)[tile_expert].astype(jnp.int32),
        "last_tile": jnp.maximum(ends[-1] // TM_EXP - 1, 0).astype(jnp.int32).reshape(1),
        "tail_start": (starts + counts).astype(jnp.int32),
        "tail_len": (padded - counts).astype(jnp.int32),
        "run_dst": flat(starts[None, :] + runs[:, 0, :]),
        "run_len": flat(runs[:, 1, :]),
        "run_off": flat(runs[:, 2, :]),
    }


def kernel(x, mix_norm, w_in, pool_w, pool_scale, w_branch_pool, w_branch_attn, rel_bias, w_out,
           ffn_norm, w_router, b_router, w1, b1, w2, b2, final_norm):
    batch, seq, d = x.shape
    n = batch * seq
    depth = mix_norm.shape[0]
    assert depth == 1, "the combine kernel fuses the final norm, so only one layer is supported"
    token_tiles = n // min(TM_POST, n)
    max_rows = (n * TOP_K_EXPERTS + N_EXPERTS * token_tiles * (ROW_ALIGN - 1)
                + N_EXPERTS * (TM_EXP - 1))
    n_tiles = (max_rows + TM_EXP - 1) // TM_EXP
    xf = x.reshape(n, d)
    for l in range(depth):
        qT, k, vT, qiT, ki, wiT, pp, g1 = _inproj(
            xf, mix_norm[l], w_in[l], pool_w[l], pool_scale[l], w_branch_pool[l], seq)
        attn = _attention(qT, k, vT, qiT, ki, wiT, rel_bias, batch, seq)
        x1, h2, lpos, gates, runs, counts = _post_attn(
            attn, pp, g1, xf, w_branch_attn[l], w_out[l], ffn_norm[l], w_router[l], b_router[l])
        plan = _routing_plan(runs, counts, n_tiles)
        xs = _dispatch(h2, lpos, plan, n_tiles * TM_EXP)
        y = _experts(xs, plan["tile_expert"], plan["tile_rows"], plan["first_of_expert"],
                     plan["next_expert"], plan["last_tile"], w1[l], b1[l], w2[l], b2[l])
        xf = _combine(y, lpos, gates, x1, final_norm, plan)
    return xf.reshape(batch, seq, d)
```

```python
import functools
import math

import jax
import jax.numpy as jnp
import numpy as np
from jax import lax
from jax.experimental import pallas as pl
from jax.experimental.pallas import tpu as pltpu

D_MODEL = 1024
POOL_WIDTH = 512
POOL_WINDOWS = (2, 4, 8, 16)
POOL_GROUPS = len(POOL_WINDOWS)
POOL_GROUP_WIDTH = POOL_WIDTH // POOL_GROUPS
N_HEADS = 8
HEAD_DIM = 64
ATTN_WIDTH = N_HEADS * HEAD_DIM
N_IDX_HEADS = 4
IDX_DIM = 64
IDX_SCALE = (IDX_DIM ** -0.5) * (N_IDX_HEADS ** -0.5)
ATTN_SCALE = HEAD_DIM ** -0.5
TOPK_MAX = 256
REL_BUCKETS = 32
REL_MAX_DIST = 128
N_BRANCHES = 2
N_EXPERTS = 32
TOP_K_EXPERTS = 4
D_FF = D_MODEL
SWIGLU_LIMIT = 7.0
SWIGLU_ALPHA = 1.702
RMS_EPS = 1e-5
SPLIT_SIZES = (POOL_WIDTH, ATTN_WIDTH, ATTN_WIDTH, ATTN_WIDTH,
               N_IDX_HEADS * IDX_DIM, IDX_DIM, N_IDX_HEADS, N_BRANCHES * D_MODEL)

LANES = 128
SUBLANES = 8
VMEM_LIMIT_BYTES = 56 * 1024 * 1024

TM_IN = 1024
TQ = 256
TK = 256
TM_POST = 512
TM_EXP = 512
ROW_ALIGN = SUBLANES
PERM_CHUNK = 1152
POOL_HALO = 16
N_BISECT_BF16 = 10
N_BISECT_F32 = 8
BF16_ROWS = 2 * SUBLANES
BF16_STEP = 2.0 ** -7
TINY = 1e-30
PAIR = 2 * HEAD_DIM
IDX_PAD = LANES
ROUTER_PAD = LANES

F32 = jnp.float32
BF16 = jnp.bfloat16
NEG_INF = float("-inf")
M_INIT = -1e30
LOG2E = math.log2(math.e)


def _dot(a, b):
    return jnp.dot(a, b, preferred_element_type=F32)


def _dot_nt(a, b):
    return lax.dot_general(a, b, (((1,), (1,)), ((), ())), preferred_element_type=F32)


def _rmsnorm(x, g):
    ms = jnp.mean(x * x, axis=-1, keepdims=True)
    return x * lax.rsqrt(ms + RMS_EPS) * g


def _const_spec(shape):
    nd = len(shape)
    return pl.BlockSpec(shape, lambda *_: (0,) * nd)


def _params(n_axes):
    return pltpu.CompilerParams(
        dimension_semantics=("arbitrary",) * n_axes,
        vmem_limit_bytes=VMEM_LIMIT_BYTES)


_ROW_SECTIONS = (("pool", POOL_WIDTH), ("k", ATTN_WIDTH), ("ki", IDX_PAD),
                 ("g0", D_MODEL), ("g1", D_MODEL))
_COL_SECTIONS = (("q", ATTN_WIDTH), ("v", ATTN_WIDTH), ("qi", N_IDX_HEADS * IDX_PAD),
                 ("wi", 2 * SUBLANES))


def _section(sections, name):
    start = 0
    for key, width in sections:
        if key == name:
            return slice(start, start + width)
        start += width
    raise KeyError(name)


def _inproj_body(x_ref, g_ref, wrow_ref, wcol_ref, poolw_ref, pscale_ref, wbp_ref,
                 qT_ref, k_ref, vT_ref, qiT_ref, ki_ref, wiT_ref, pp_ref, g1_ref,
                 halo_ref, *, tiles_per_seq):
    i = pl.program_id(0)
    tm = x_ref.shape[0]
    h = _rmsnorm(x_ref[...], g_ref[...]).astype(BF16)
    row_w = lambda name: wrow_ref[:, _section(_ROW_SECTIONS, name)]
    col_w = lambda name: wcol_ref[_section(_COL_SECTIONS, name), :]

    qT = (_dot_nt(col_w("q"), h) * (ATTN_SCALE * LOG2E)).astype(BF16)
    vT = _dot_nt(col_w("v"), h).astype(BF16)
    qiT = _dot_nt(col_w("qi"), h).astype(BF16)
    wiT = _dot_nt(col_w("wi"), h) * IDX_SCALE
    for j in range(tm // TQ):
        qT_ref[j] = qT[:, j * TQ:(j + 1) * TQ]
        qiT_ref[j] = qiT[:, j * TQ:(j + 1) * TQ]
        wiT_ref[j] = wiT[:SUBLANES, j * TQ:(j + 1) * TQ]
    for j in range(tm // TK):
        vT_ref[j] = vT[:, j * TK:(j + 1) * TK]
    k_ref[...] = _dot(h, row_w("k")).astype(BF16)
    ki_ref[...] = _dot(h, row_w("ki")).astype(BF16)
    g1_ref[...] = jax.nn.sigmoid(_dot(h, row_w("g1"))).astype(BF16)

    zp = _dot(h, row_w("pool"))
    seq_tile = lax.rem(i, tiles_per_seq)

    @pl.when(seq_tile == 0)
    def _():
        halo_ref[...] = jnp.zeros_like(halo_ref)

    zext = jnp.concatenate([halo_ref[...], zp], axis=0)
    halo_ref[...] = zp[tm - POOL_HALO:, :]
    gw = POOL_GROUP_WIDTH
    s2 = zext + pltpu.roll(zext, 1, 0)
    s4 = s2[:, gw:] + pltpu.roll(s2[:, gw:], 2, 0)
    s8 = s4[:, gw:] + pltpu.roll(s4[:, gw:], 4, 0)
    s16 = s8[:, gw:] + pltpu.roll(s8[:, gw:], 8, 0)
    wsum = (s2[POOL_HALO:, :gw], s4[POOL_HALO:, :gw], s8[POOL_HALO:, :gw], s16[POOL_HALO:, :])
    t = seq_tile * tm + lax.broadcasted_iota(jnp.int32, (tm, 1), 0)
    mixed = []
    for g, w in enumerate(POOL_WINDOWS):
        cnt = jnp.minimum(t + 1, w).astype(F32)
        pooled = wsum[g] / cnt - zp[:, g * gw:(g + 1) * gw]
        mixed.append(_dot(pooled.astype(BF16), poolw_ref[g]) * pscale_ref[:, g * gw:(g + 1) * gw])
    mixed = jnp.concatenate(mixed, axis=1).astype(BF16)
    y_pool = _dot(mixed, wbp_ref[...])
    gate0 = jax.nn.sigmoid(_dot(h, row_w("g0")))
    pp_ref[...] = (gate0 * y_pool).astype(BF16)


def _inproj(xf, mix_norm, w_in, pool_w, pool_scale, w_branch_pool, seq):
    n, d = xf.shape
    tm = min(TM_IN, seq)
    assert seq % tm == 0 and tm % TK == 0 and tm % TQ == 0 and n % tm == 0
    offs = [0] + [int(o) for o in np.cumsum(SPLIT_SIZES)]
    z_pool, z_q, z_k, z_v, z_qi, z_ki, z_wi, z_gate = (
        w_in[:, offs[j]:offs[j + 1]] for j in range(len(SPLIT_SIZES)))
    pad_cols = lambda a, width: jnp.pad(a, ((0, 0), (0, width - a.shape[1])))
    qi_heads = jnp.pad(z_qi.reshape(d, N_IDX_HEADS, IDX_DIM),
                       ((0, 0), (0, 0), (0, IDX_PAD - IDX_DIM))).reshape(d, N_IDX_HEADS * IDX_PAD)
    parts = {"pool": z_pool, "k": z_k, "ki": pad_cols(z_ki, IDX_PAD),
             "g0": z_gate[:, :D_MODEL], "g1": z_gate[:, D_MODEL:],
             "q": z_q, "v": z_v, "qi": qi_heads, "wi": pad_cols(z_wi, 2 * SUBLANES)}
    w_row = jnp.concatenate([parts[k] for k, _ in _ROW_SECTIONS], axis=1).astype(BF16)
    w_col = jnp.concatenate([parts[k] for k, _ in _COL_SECTIONS], axis=1).astype(BF16).T
    consts = [mix_norm.reshape(1, d).astype(F32), w_row, w_col, pool_w.astype(BF16),
              pool_scale.reshape(1, POOL_WIDTH).astype(F32), w_branch_pool.astype(BF16)]
    grid = (n // tm,)
    row = lambda width: pl.BlockSpec((tm, width), lambda i: (i, 0))
    tiles = lambda t, height: pl.BlockSpec((tm // t, height, t), lambda i: (i, 0, 0))
    out_shape = [
        jax.ShapeDtypeStruct((n // TQ, ATTN_WIDTH, TQ), BF16),
        jax.ShapeDtypeStruct((n, ATTN_WIDTH), BF16),
        jax.ShapeDtypeStruct((n // TK, ATTN_WIDTH, TK), BF16),
        jax.ShapeDtypeStruct((n // TQ, N_IDX_HEADS * IDX_PAD, TQ), BF16),
        jax.ShapeDtypeStruct((n, IDX_PAD), BF16),
        jax.ShapeDtypeStruct((n // TQ, SUBLANES, TQ), F32),
        jax.ShapeDtypeStruct((n, D_MODEL), BF16),
        jax.ShapeDtypeStruct((n, D_MODEL), BF16),
    ]
    out_specs = [
        tiles(TQ, ATTN_WIDTH),
        row(ATTN_WIDTH),
        tiles(TK, ATTN_WIDTH),
        tiles(TQ, N_IDX_HEADS * IDX_PAD),
        row(IDX_PAD),
        tiles(TQ, SUBLANES),
        row(D_MODEL),
        row(D_MODEL),
    ]
    return pl.pallas_call(
        functools.partial(_inproj_body, tiles_per_seq=seq // tm),
        grid=grid,
        in_specs=[row(d)] + [_const_spec(c.shape) for c in consts],
        out_specs=out_specs,
        out_shape=out_shape,
        scratch_shapes=[pltpu.VMEM((POOL_HALO, POOL_WIDTH), F32)],
        compiler_params=_params(1),
        name="inproj",
    )(xf, *consts)


def _rel_thresholds():
    n = np.arange(0, 4 * REL_MAX_DIST)
    max_exact = REL_BUCKETS // 2
    nf = np.maximum(n, 1).astype(np.float32)
    large = max_exact + (np.log(nf / np.float32(max_exact))
                         / np.float32(math.log(REL_MAX_DIST / max_exact))
                         * np.float32(REL_BUCKETS - max_exact)).astype(np.int32)
    bucket = np.where(n < max_exact, n, np.minimum(large, REL_BUCKETS - 1))
    assert np.all(np.diff(bucket) >= 0) and np.all(np.diff(bucket) <= 1)
    assert bucket[-1] == REL_BUCKETS - 1
    return [int(np.argmax(bucket >= b)) for b in range(1, REL_BUCKETS)]


def _attn_body(table_ref, qT_ref, qiT_ref, wiT_ref, k_ref, ki_ref, vT_ref, o_ref,
               score_ref, sb_ref, band_ref, tri_ref, qm_ref, m_ref, l_ref, acc_ref, thr_ref,
               need_ref,
               *, topk):
    b = pl.program_id(0)
    qi = pl.program_id(1)
    tq = o_ref.shape[0]
    nk = qi + 1
    n_keys = score_ref.shape[0] * TK
    key_i = lax.broadcasted_iota(jnp.int32, (TK, tq), 0)
    qry_i = lax.broadcasted_iota(jnp.int32, (TK, tq), 1)

    @pl.when((b == 0) & (qi == 0))
    def _():
        r_i = lax.broadcasted_iota(jnp.int32, (TK, TK), 0)
        c_i = lax.broadcasted_iota(jnp.int32, (TK, TK), 1)
        tri_ref[...] = jnp.where(c_i < r_i, 1.0, 0.0).astype(BF16)
        thresholds = _rel_thresholds()
        assert thresholds[-1] <= TK
        for part in range(2):
            dist = qry_i - key_i + (1 - part) * TK
            for h in range(N_HEADS):
                bias = jnp.full((TK, tq), table_ref[h], F32)
                for bkt, thr in enumerate(thresholds, start=1):
                    bias = jnp.where(dist >= thr, table_ref[bkt * N_HEADS + h], bias)
                band_ref[h, part] = jnp.where(dist < 0, NEG_INF, bias * LOG2E)

    first_half = lax.broadcasted_iota(jnp.int32, (PAIR, tq), 0) < HEAD_DIM
    for hp in range(N_HEADS // 2):
        qp = qT_ref[0, hp * PAIR:(hp + 1) * PAIR, :]
        zero = jnp.zeros_like(qp)
        qm_ref[2 * hp] = jnp.where(first_half, qp, zero)
        qm_ref[2 * hp + 1] = jnp.where(first_half, zero, qp)

    wiT = wiT_ref[0]

    n_pairs = (nk + 1) // 2

    def score_pair(j, carry):
        mx, mn = carry
        tiles = (2 * j, 2 * j + 1)
        heads = [[_dot(ki_ref[pl.ds(pl.multiple_of(kj * TK, TK), TK), :],
                       qiT_ref[0, h * IDX_PAD:(h + 1) * IDX_PAD, :])
                  for h in range(N_IDX_HEADS)] for kj in tiles]
        for kj, dots in zip(tiles, heads):
            sc = jnp.maximum(dots[0], 0.0) * wiT[0:1, :]
            for h in range(1, N_IDX_HEADS):
                sc = sc + jnp.maximum(dots[h], 0.0) * wiT[h:h + 1, :]
            causal = (kj * TK + key_i) <= (qi * tq + qry_i)
            masked = jnp.where(causal, sc, NEG_INF)
            score_ref[kj] = masked
            sb_ref[kj] = masked.astype(BF16)
            mx = jnp.maximum(mx, jnp.max(sc, axis=0, keepdims=True))
            mn = jnp.minimum(mn, jnp.min(sc, axis=0, keepdims=True))
        return mx, mn

    row_max, row_min = lax.fori_loop(
        0, n_pairs, score_pair,
        (jnp.full((1, tq), NEG_INF, F32), jnp.full((1, tq), -NEG_INF, F32)))

    def fold(x, op):
        return op(x.reshape(TK // SUBLANES, SUBLANES, tq), axis=0)

    def count_where(pred):
        def body(j, acc):
            for kj in (2 * j, 2 * j + 1):
                acc = acc + fold(jnp.where(pred(score_ref[kj]), 1.0, 0.0), jnp.sum)
            return acc
        acc = lax.fori_loop(0, n_pairs, body, jnp.zeros((SUBLANES, tq), F32))
        return jnp.sum(acc, axis=0, keepdims=True)

    def max_where(pred):
        def body(j, acc):
            for kj in (2 * j, 2 * j + 1):
                s = score_ref[kj]
                acc = jnp.maximum(acc, fold(jnp.where(pred(s), s, NEG_INF), jnp.max))
            return acc
        acc = lax.fori_loop(0, n_pairs, body, jnp.full((SUBLANES, tq), NEG_INF, F32))
        return jnp.max(acc, axis=0, keepdims=True)

    kf = float(topk)
    thr_ref[...] = jnp.full((1, tq), NEG_INF, F32)
    need_ref[...] = jnp.full((1, tq), float(n_keys), F32)

    @pl.when(qi * tq + 1 > topk)
    def _():
        def count_above_bf16(mid_b):
            mid_t = jnp.broadcast_to(mid_b, (TK, tq))
            one, zero = jnp.ones((), BF16), jnp.zeros((), BF16)

            def body(j, acc):
                for kj in (2 * j, 2 * j + 1):
                    m = jnp.where(sb_ref[kj] > mid_t, one, zero)
                    parts = [m[r * BF16_ROWS:(r + 1) * BF16_ROWS]
                             for r in range(TK // BF16_ROWS)]
                    while len(parts) > 1:
                        parts = [a + b for a, b in zip(parts[::2], parts[1::2])]
                    acc = acc + parts[0].astype(F32)
                return acc
            acc = lax.fori_loop(0, n_pairs, body, jnp.zeros((BF16_ROWS, tq), F32))
            return jnp.sum(acc, axis=0, keepdims=True)

        def widen(v, sign):
            return v + sign * (jnp.abs(v) * BF16_STEP + TINY)

        def bisect_bf16(_, carry):
            lo, hi = carry
            mid_b = (0.5 * (lo + hi)).astype(BF16)
            above = count_above_bf16(mid_b) >= kf
            mid = mid_b.astype(F32)
            return jnp.where(above, mid, lo), jnp.where(above, hi, mid)

        lo, hi = lax.fori_loop(0, N_BISECT_BF16, bisect_bf16,
                               (widen(row_min, -1.0), widen(row_max, 1.0)))

        def bisect(_, carry):
            lo, hi = carry
            mid = 0.5 * (lo + hi)
            above = count_where(lambda s: s > mid) >= kf
            return jnp.where(above, mid, lo), jnp.where(above, hi, mid)

        _, hi = lax.fori_loop(0, N_BISECT_F32, bisect, (widen(lo, -1.0), widen(hi, 1.0)))
        cand = max_where(lambda s: s <= hi)
        n_ge = count_where(lambda s: s >= cand)

        def unresolved(state):
            it, _, n_ge = state
            return (jnp.min(n_ge) < kf) & (it < n_keys)

        def step(state):
            it, cand, n_ge = state
            nxt = max_where(lambda s: s < cand)
            n_nxt = count_where(lambda s: s >= nxt)
            open_ = n_ge < kf
            return it + 1, jnp.where(open_, nxt, cand), jnp.where(open_, n_nxt, n_ge)

        _, thr, n_ge = lax.while_loop(unresolved, step, (jnp.int32(0), cand, n_ge))
        thr_ref[...] = thr

        @pl.when(jnp.max(n_ge) > kf)
        def _():
            need_ref[...] = kf - count_where(lambda s: s > thr)

    m_ref[...] = jnp.full(m_ref.shape, M_INIT, F32)
    l_ref[...] = jnp.zeros(l_ref.shape, F32)
    acc_ref[...] = jnp.zeros(acc_ref.shape, F32)
    thr = thr_ref[...]
    need = need_ref[...]

    far_bias = [table_ref[(REL_BUCKETS - 1) * N_HEADS + h] * LOG2E for h in range(N_HEADS)]

    def attend(kjs, ties_before, near):
        logits = []
        for kj in kjs:
            k_t = k_ref[pl.ds(pl.multiple_of(kj * TK, TK), TK), :]
            logits.append([_dot(k_t[:, (h // 2) * PAIR:(h // 2 + 1) * PAIR], qm_ref[h])
                           for h in range(N_HEADS)])
        for kj, tile_logits in zip(kjs, logits):
            sc = score_ref[kj]
            tied = jnp.where(sc == thr, 1.0, 0.0)
            rank = _dot(tri_ref[...], tied.astype(BF16)) + ties_before
            ties_before = rank[TK - 1:, :] + tied[TK - 1:, :]
            sel_bias = jnp.where(
                sc > thr, 0.0,
                jnp.where(sc == thr, jnp.where(rank < need, 0.0, NEG_INF), NEG_INF))
            vT_t = vT_ref[kj]
            probs, alphas = [], []
            for h in range(N_HEADS):
                m_old = m_ref[h]
                if near:
                    s = tile_logits[h] + band_ref[h, kj - qi + 1] + sel_bias
                    m_new = jnp.maximum(m_old, jnp.max(s, axis=0, keepdims=True))
                    p = jnp.exp2(s - m_new)
                else:
                    s = tile_logits[h] + sel_bias
                    m_new = jnp.maximum(m_old, jnp.max(s, axis=0, keepdims=True) + far_bias[h])
                    p = jnp.exp2(s - (m_new - far_bias[h]))
                alpha = jnp.exp2(m_old - m_new)
                l_ref[h] = alpha * l_ref[h] + jnp.sum(p, axis=0, keepdims=True)
                m_ref[h] = m_new
                probs.append(p.astype(BF16))
                alphas.append(alpha)
            for h in range(N_HEADS):
                out_h = _dot(vT_t[h * HEAD_DIM:(h + 1) * HEAD_DIM, :], probs[h])
                acc_ref[h] = acc_ref[h] * alphas[h] + out_h
        return ties_before

    n_far = jnp.maximum(qi - 1, 0)
    far_pairs = n_far // 2
    ties = lax.fori_loop(0, far_pairs,
                         lambda j, t: attend((2 * j, 2 * j + 1), t, near=False),
                         jnp.zeros((1, tq), F32))
    ties = lax.fori_loop(2 * far_pairs, n_far, lambda kj, t: attend((kj,), t, near=False), ties)

    @pl.when(qi == 0)
    def _():
        attend((qi,), ties, near=True)

    @pl.when(qi > 0)
    def _():
        attend((qi - 1, qi), ties, near=True)
    for hp in range(N_HEADS // 2):
        pair = jnp.concatenate([acc_ref[2 * hp + e] / l_ref[2 * hp + e] for e in range(2)], axis=0)
        o_ref[:, hp * PAIR:(hp + 1) * PAIR] = pair.T.astype(o_ref.dtype)


def _attention(qT, k, vT, qiT, ki, wiT, rel_bias, batch, seq):
    n = k.shape[0]
    tq = min(TQ, seq)
    assert tq == TQ == TK and seq % tq == 0
    topk = min(TOPK_MAX, seq // 4)
    assert topk == tq or seq == tq
    nq = seq // tq
    nkt = seq // TK
    assert nkt % 2 == 0
    table = rel_bias.astype(F32).reshape(REL_BUCKETS * N_HEADS)
    qtile = lambda height: pl.BlockSpec((1, height, tq), lambda b, i: (b * nq + i, 0, 0))
    return pl.pallas_call(
        functools.partial(_attn_body, topk=topk),
        grid=(batch, nq),
        in_specs=[
            pl.BlockSpec(memory_space=pltpu.SMEM),
            qtile(ATTN_WIDTH),
            qtile(N_IDX_HEADS * IDX_PAD),
            qtile(SUBLANES),
            pl.BlockSpec((seq, ATTN_WIDTH), lambda b, i: (b, 0)),
            pl.BlockSpec((seq, IDX_PAD), lambda b, i: (b, 0)),
            pl.BlockSpec((nkt, ATTN_WIDTH, TK), lambda b, i: (b, 0, 0)),
        ],
        out_specs=pl.BlockSpec((tq, ATTN_WIDTH), lambda b, i: (b * nq + i, 0)),
        out_shape=jax.ShapeDtypeStruct((n, ATTN_WIDTH), BF16),
        scratch_shapes=[
            pltpu.VMEM((nkt, TK, tq), F32),
            pltpu.VMEM((nkt, TK, tq), BF16),
            pltpu.VMEM((N_HEADS, 2, TK, tq), F32),
            pltpu.VMEM((TK, TK), BF16),
            pltpu.VMEM((N_HEADS, PAIR, tq), BF16),
            pltpu.VMEM((N_HEADS, 1, tq), F32),
            pltpu.VMEM((N_HEADS, 1, tq), F32),
            pltpu.VMEM((N_HEADS, HEAD_DIM, tq), F32),
            pltpu.VMEM((1, tq), F32),
            pltpu.VMEM((1, tq), F32),
        ],
        compiler_params=_params(2),
        name="attention",
    )(table, qT, qiT, wiT, k, ki, vT)


def _post_body(attn_ref, pp_ref, g1_ref, x_ref, wba_ref, wout_ref, fg_ref, wr_hi_ref, wr_lo_ref,
               br_ref, x1_ref, h2_ref, lpos_ref, gates_ref, runs_ref, counts_ref, carry_ref):
    i = pl.program_id(0)
    tm = x_ref.shape[0]
    y_attn = _dot(attn_ref[...], wba_ref[...])
    merged = pp_ref[...].astype(F32) + g1_ref[...].astype(F32) * y_attn
    x1 = x_ref[...] + _dot(merged.astype(BF16), wout_ref[...])
    x1_ref[...] = x1
    h2 = _rmsnorm(x1, fg_ref[...])
    h2_ref[...] = h2

    h_hi = h2.astype(BF16)
    h_lo = (h2 - h_hi.astype(F32)).astype(BF16)
    logits = (_dot(h_hi, wr_hi_ref[...]) + _dot(h_hi, wr_lo_ref[...])
              + _dot(h_lo, wr_hi_ref[...]) + br_ref[...])

    lane = lax.broadcasted_iota(jnp.int32, (tm, ROUTER_PAD), 1)
    work = logits
    vals, idxs = [], []
    for _ in range(TOP_K_EXPERTS):
        mx = jnp.max(work, axis=1, keepdims=True)
        ix = jnp.min(jnp.where(work == mx, lane, ROUTER_PAD), axis=1, keepdims=True)
        vals.append(mx)
        idxs.append(ix)
        work = jnp.where(lane == ix, NEG_INF, work)
    exps = [jnp.exp(v - vals[0]) for v in vals]
    denom = exps[0] + exps[1] + exps[2] + exps[3]

    member = jnp.zeros((tm, ROUTER_PAD), F32)
    for ix in idxs:
        member = member + jnp.where(lane == ix, 1.0, 0.0)

    @pl.when(i == 0)
    def _():
        carry_ref[...] = jnp.zeros_like(carry_ref)

    r_i = lax.broadcasted_iota(jnp.int32, (tm, tm), 0)
    c_i = lax.broadcasted_iota(jnp.int32, (tm, tm), 1)
    strict_lower = jnp.where(c_i < r_i, 1.0, 0.0).astype(BF16)
    local = _dot(strict_lower, member.astype(BF16))
    count = jnp.sum(member, axis=0, keepdims=True)
    run_len = jnp.floor((count + (ROW_ALIGN - 1)) * (1.0 / ROW_ALIGN)) * ROW_ALIGN
    e_r = lax.broadcasted_iota(jnp.int32, (ROUTER_PAD, ROUTER_PAD), 0)
    e_c = lax.broadcasted_iota(jnp.int32, (ROUTER_PAD, ROUTER_PAD), 1)
    strict_upper = jnp.where(e_r < e_c, 1.0, 0.0).astype(BF16)
    run_off = _dot(jnp.broadcast_to(run_len, (SUBLANES, ROUTER_PAD)).astype(BF16),
                   strict_upper)[:1, :]
    run_start = carry_ref[...]
    carry_new = run_start + run_len
    carry_ref[...] = carry_new
    counts_ref[...] = jnp.broadcast_to(carry_new, counts_ref.shape)
    sub = lax.broadcasted_iota(jnp.int32, (SUBLANES, ROUTER_PAD), 0)
    runs_ref[0] = jnp.where(sub == 0, run_start, jnp.where(sub == 1, run_len,
                                                          jnp.where(sub == 2, run_off, 0.0)))

    slot = run_off + local
    lpos = jnp.zeros((tm, ROUTER_PAD), jnp.int32)
    gates = jnp.zeros((tm, ROUTER_PAD), F32)
    for k in range(TOP_K_EXPERTS):
        pos = jnp.sum(jnp.where(lane == idxs[k], slot, 0.0), axis=1, keepdims=True)
        lpos = jnp.where(lane == k, pos.astype(jnp.int32), lpos)
        gates = jnp.where(lane == k, exps[k] / denom, gates)
    lpos_ref[...] = lpos
    gates_ref[...] = gates


def _post_attn(attn, pp, g1, xf, w_branch_attn, w_out, ffn_norm, w_router, b_router):
    n, d = xf.shape
    tm = min(TM_POST, n)
    assert n % tm == 0
    wr = jnp.pad(w_router.astype(F32), ((0, 0), (0, ROUTER_PAD - N_EXPERTS)))
    wr_hi = wr.astype(BF16)
    wr_lo = (wr - wr_hi.astype(F32)).astype(BF16)
    br = jnp.pad(b_router.astype(F32), (0, ROUTER_PAD - N_EXPERTS),
                 constant_values=NEG_INF).reshape(1, ROUTER_PAD)
    consts = [w_branch_attn.astype(BF16), w_out.astype(BF16),
              ffn_norm.reshape(1, d).astype(F32), wr_hi, wr_lo, br]
    row = lambda width: pl.BlockSpec((tm, width), lambda i: (i, 0))
    return pl.pallas_call(
        _post_body,
        grid=(n // tm,),
        in_specs=[row(ATTN_WIDTH), row(d), row(d), row(d)] + [_const_spec(c.shape) for c in consts],
        out_specs=[row(d), row(d), row(ROUTER_PAD), row(ROUTER_PAD),
                   pl.BlockSpec((1, SUBLANES, ROUTER_PAD), lambda i: (i, 0, 0)),
                   _const_spec((SUBLANES, ROUTER_PAD))],
        out_shape=[
            jax.ShapeDtypeStruct((n, d), F32),
            jax.ShapeDtypeStruct((n, d), F32),
            jax.ShapeDtypeStruct((n, ROUTER_PAD), jnp.int32),
            jax.ShapeDtypeStruct((n, ROUTER_PAD), F32),
            jax.ShapeDtypeStruct((n // tm, SUBLANES, ROUTER_PAD), F32),
            jax.ShapeDtypeStruct((SUBLANES, ROUTER_PAD), F32),
        ],
        scratch_shapes=[pltpu.VMEM((1, ROUTER_PAD), F32)],
        compiler_params=_params(1),
        name="post_attn",
    )(attn, pp, g1, xf, *consts)


def _block_copies(length, make_copy, max_block):
    block = max_block
    while block >= ROW_ALIGN:
        offset = pl.multiple_of(jnp.bitwise_and(length, -2 * block), ROW_ALIGN)
        yield jnp.bitwise_and(length, block) != 0, make_copy(offset, block)
        block //= 2


def _for_each_block(length, make_copy, max_block, action):
    for pred, cp in _block_copies(length, make_copy, max_block):
        @pl.when(pred)
        def _(cp=cp):
            action(cp)


def _staging_rows(tm):
    return TOP_K_EXPERTS * tm + N_EXPERTS * ROW_ALIGN


def _dispatch_body(run_dst_ref, run_len_ref, run_off_ref, tail_start_ref, tail_len_ref,
                   last_tile_ref, lpos_ref, h2_ref, xs_ref, buf_ref, zero_ref, sem, zsem,
                   *, first_tail_tile):
    i = pl.program_id(0)
    tm = h2_ref.shape[0]
    n_tiles = xs_ref.shape[0] // TM_EXP
    rows = buf_ref.shape[1]

    lpos_t = lpos_ref[...].T
    tokens = h2_ref[...].astype(BF16)
    half_now = lax.rem(i, 2)

    def permute_chunk(c, carry):
        r0 = pl.multiple_of(c * PERM_CHUNK, PERM_CHUNK)
        slot = r0 + lax.broadcasted_iota(jnp.int32, (PERM_CHUNK, tm), 0)
        onehot = jnp.where(slot == lpos_t[0:1, :], 1.0, 0.0)
        for k in range(1, TOP_K_EXPERTS):
            onehot = onehot + jnp.where(slot == lpos_t[k:k + 1, :], 1.0, 0.0)
        buf_ref[half_now, pl.ds(r0, PERM_CHUNK), :] = _dot(onehot.astype(BF16), tokens)
        return carry

    def run_copy(tile, e):
        base = tile * N_EXPERTS + e
        half = lax.rem(tile, 2)
        src0 = pl.multiple_of(run_off_ref[base], ROW_ALIGN)
        dst0 = pl.multiple_of(run_dst_ref[base], ROW_ALIGN)
        return run_len_ref[base], lambda off, blk: pltpu.make_async_copy(
            buf_ref.at[half, pl.ds(src0 + off, blk)], xs_ref.at[pl.ds(dst0 + off, blk)],
            sem.at[half])

    def for_runs(tile, action):
        def body(e, carry):
            length, make = run_copy(tile, e)
            _for_each_block(length, make, tm, action)
            return carry
        lax.fori_loop(0, N_EXPERTS, body, 0)

    @pl.when(i >= 2)
    def _():
        for_runs(i - 2, lambda cp: cp.wait())

    lax.fori_loop(0, rows // PERM_CHUNK, permute_chunk, 0)
    for_runs(i, lambda cp: cp.start())

    def zero_fill(action):
        def tail_copy(e):
            dst0 = pl.multiple_of(tail_start_ref[e], ROW_ALIGN)
            return tail_len_ref[e], lambda off, blk: pltpu.make_async_copy(
                zero_ref.at[pl.ds(0, blk)], xs_ref.at[pl.ds(dst0 + off, blk)], zsem)

        def fill_tail(e, carry):
            length, make = tail_copy(e)
            _for_each_block(length, make, TM_EXP // 2, action)
            return carry

        lax.fori_loop(0, N_EXPERTS, fill_tail, 0)

        def tail_tile(t, carry):
            @pl.when(t > last_tile_ref[0])
            def _():
                action(pltpu.make_async_copy(
                    zero_ref, xs_ref.at[pl.ds(pl.multiple_of(t * TM_EXP, TM_EXP), TM_EXP)], zsem))
            return carry

        lax.fori_loop(first_tail_tile, n_tiles, tail_tile, 0)

    @pl.when(i == 0)
    def _():
        zero_ref[...] = jnp.zeros_like(zero_ref)
        zero_fill(lambda cp: cp.start())

    @pl.when(i == pl.num_programs(0) - 1)
    def _():
        @pl.when(i >= 1)
        def _():
            for_runs(i - 1, lambda cp: cp.wait())
        for_runs(i, lambda cp: cp.wait())
        zero_fill(lambda cp: cp.wait())


def _dispatch(h2, lpos, plan, n_rows):
    n, d = h2.shape
    tm = min(TM_POST, n)
    assert n % tm == 0 and tm % ROW_ALIGN == 0 and _staging_rows(tm) % PERM_CHUNK == 0
    return pl.pallas_call(
        functools.partial(_dispatch_body, first_tail_tile=(n * TOP_K_EXPERTS) // TM_EXP),
        grid_spec=pltpu.PrefetchScalarGridSpec(
            num_scalar_prefetch=6,
            grid=(n // tm,),
            in_specs=[
                pl.BlockSpec((tm, ROUTER_PAD), lambda i, *_: (i, 0)),
                pl.BlockSpec((tm, d), lambda i, *_: (i, 0)),
            ],
            out_specs=pl.BlockSpec(memory_space=pl.ANY),
            scratch_shapes=[pltpu.VMEM((2, _staging_rows(tm), d), F32),
                            pltpu.VMEM((TM_EXP, d), F32),
                            pltpu.SemaphoreType.DMA((2,)), pltpu.SemaphoreType.DMA],
        ),
        out_shape=jax.ShapeDtypeStruct((n_rows, d), F32),
        compiler_params=_params(1),
        name="dispatch",
    )(plan["run_dst"], plan["run_len"], plan["run_off"], plan["tail_start"], plan["tail_len"],
      plan["last_tile"], lpos, h2)


def _experts_body(tile_expert_ref, tile_rows_ref, run_start_ref, next_expert_ref, last_tile_ref,
                  xs_ref, w1_hbm, b1_ref, w2_hbm, b2_ref, y_ref,
                  w1f_ref, w2f_ref, w1b_ref, w2b_ref, wsem):
    i = pl.program_id(0)

    def fetch(expert):
        return (pltpu.make_async_copy(w1_hbm.at[expert], w1f_ref, wsem.at[0]),
                pltpu.make_async_copy(w2_hbm.at[expert], w2f_ref, wsem.at[1]))

    @pl.when(i == 0)
    def _():
        for cp in fetch(tile_expert_ref[0]):
            cp.start()

    @pl.when(run_start_ref[i] == 1)
    def _():
        for cp in fetch(tile_expert_ref[i]):
            cp.wait()
        w1b_ref[...] = w1f_ref[...].astype(BF16)
        w2b_ref[...] = w2f_ref[...].astype(BF16)

        @pl.when(next_expert_ref[i] >= 0)
        def _():
            for cp in fetch(next_expert_ref[i]):
                cp.start()

    @pl.when(tile_rows_ref[i] > 0)
    def _():
        x = xs_ref[...].astype(BF16)
        gu = _dot(x, w1b_ref[...]) + b1_ref[0]
        g = jnp.minimum(gu[:, :D_FF], SWIGLU_LIMIT)
        u = jnp.clip(gu[:, D_FF:], -SWIGLU_LIMIT, SWIGLU_LIMIT)
        act = g * jax.nn.sigmoid(SWIGLU_ALPHA * g) * (u + 1.0)
        y_ref[...] = _dot(act.astype(BF16), w2b_ref[...]) + b2_ref[0]

    @pl.when(tile_rows_ref[i] == 0)
    def _():
        y_ref[...] = jnp.zeros_like(y_ref)


def _experts(xs, tile_expert, tile_rows, run_start, next_expert, last_tile, w1, b1, w2, b2):
    n_rows, d = xs.shape
    n_tiles = n_rows // TM_EXP
    tile = lambda i, *_: (i, 0)
    tile_in = lambda i, te, tr, rs, ne, lt: (jnp.minimum(i, lt[0]), 0)
    per_expert = lambda i, te, *_: (te[i], 0, 0)
    return pl.pallas_call(
        _experts_body,
        grid_spec=pltpu.PrefetchScalarGridSpec(
            num_scalar_prefetch=5,
            grid=(n_tiles,),
            in_specs=[
                pl.BlockSpec((TM_EXP, d), tile_in),
                pl.BlockSpec(memory_space=pl.ANY),
                pl.BlockSpec((1, 1, 2 * D_FF), per_expert),
                pl.BlockSpec(memory_space=pl.ANY),
                pl.BlockSpec((1, 1, d), per_expert),
            ],
            out_specs=pl.BlockSpec((TM_EXP, d), tile),
            scratch_shapes=[
                pltpu.VMEM((d, 2 * D_FF), F32), pltpu.VMEM((D_FF, d), F32),
                pltpu.VMEM((d, 2 * D_FF), BF16), pltpu.VMEM((D_FF, d), BF16),
                pltpu.SemaphoreType.DMA((2,)),
            ],
        ),
        out_shape=jax.ShapeDtypeStruct((n_rows, d), F32),
        compiler_params=_params(1),
        name="experts",
    )(tile_expert, tile_rows, run_start, next_expert, last_tile, xs, w1,
      b1.reshape(N_EXPERTS, 1, 2 * D_FF), w2, b2.reshape(N_EXPERTS, 1, d))


def _combine_body(run_dst_ref, run_len_ref, run_off_ref, lpos_ref, gates_ref, x1_ref, fn_ref,
                  y_ref, o_ref, buf_ref, sem):
    i = pl.program_id(0)
    tm = x1_ref.shape[0]
    rows = buf_ref.shape[1]

    @pl.when(i == 0)
    def _():
        buf_ref[...] = jnp.zeros_like(buf_ref)

    def run_copy(tile, e):
        base = tile * N_EXPERTS + e
        half = lax.rem(tile, 2)
        src0 = pl.multiple_of(run_dst_ref[base], ROW_ALIGN)
        dst0 = pl.multiple_of(run_off_ref[base], ROW_ALIGN)
        return run_len_ref[base], lambda off, blk: pltpu.make_async_copy(
            y_ref.at[pl.ds(src0 + off, blk)], buf_ref.at[half, pl.ds(dst0 + off, blk)],
            sem.at[half])

    def for_runs(tile, action):
        def body(e, carry):
            length, make = run_copy(tile, e)
            _for_each_block(length, make, tm, action)
            return carry
        lax.fori_loop(0, N_EXPERTS, body, 0)

    @pl.when(i == 0)
    def _():
        for_runs(i, lambda cp: cp.start())

    @pl.when(i + 1 < pl.num_programs(0))
    def _():
        for_runs(i + 1, lambda cp: cp.start())

    for_runs(i, lambda cp: cp.wait())

    lpos = lpos_ref[...]
    gates = gates_ref[...]
    slot = lax.broadcasted_iota(jnp.int32, (tm, rows), 1)
    weights = jnp.zeros((tm, rows), F32)
    for k in range(TOP_K_EXPERTS):
        weights = weights + jnp.where(slot == lpos[:, k:k + 1], gates[:, k:k + 1], 0.0)
    out = x1_ref[...] + _dot(weights.astype(BF16), buf_ref[lax.rem(i, 2)].astype(BF16))
    o_ref[...] = _rmsnorm(out, fn_ref[...])


def _combine(y, lpos, gates, x1, final_norm, plan):
    n, d = x1.shape
    tm = min(TM_POST, n)
    assert n % tm == 0
    row = lambda width: pl.BlockSpec((tm, width), lambda i, *_: (i, 0))
    return pl.pallas_call(
        _combine_body,
        grid_spec=pltpu.PrefetchScalarGridSpec(
            num_scalar_prefetch=3,
            grid=(n // tm,),
            in_specs=[
                row(ROUTER_PAD), row(ROUTER_PAD), row(d),
                pl.BlockSpec((1, d), lambda i, *_: (0, 0)),
                pl.BlockSpec(memory_space=pl.ANY),
            ],
            out_specs=row(d),
            scratch_shapes=[pltpu.VMEM((2, _staging_rows(tm), d), F32),
                            pltpu.SemaphoreType.DMA((2,))],
        ),
        out_shape=jax.ShapeDtypeStruct((n, d), F32),
        compiler_params=_params(1),
        name="combine",
    )(plan["run_dst"], plan["run_len"], plan["run_off"], lpos, gates, x1,
      final_norm.reshape(1, d).astype(F32), y)


def _routing_plan(runs, counts, n_tiles):
    counts = counts[0, :N_EXPERTS].astype(jnp.int32)
    padded = ((counts + TM_EXP - 1) // TM_EXP) * TM_EXP
    ends = jnp.cumsum(padded)
    starts = ends - padded
    runs = runs[:, :, :N_EXPERTS].astype(jnp.int32)
    flat = lambda a: a.reshape(-1).astype(jnp.int32)
    tile_row0 = jnp.arange(n_tiles, dtype=jnp.int32) * TM_EXP
    tile_expert = jnp.minimum(jnp.sum(tile_row0[:, None] >= ends[None, :], axis=1),
                              N_EXPERTS - 1).astype(jnp.int32)
    tile_rows = jnp.clip(counts[tile_expert] - (tile_row0 - starts[tile_expert]), 0, TM_EXP)
    used = tile_row0 < ends[-1]
    tile_rows = jnp.where(used, tile_rows, 0).astype(jnp.int32)
    changed = jnp.concatenate([jnp.ones((1,), bool), tile_expert[1:] != tile_expert[:-1]])
    ids = jnp.where(counts > 0, jnp.arange(N_EXPERTS, dtype=jnp.int32), N_EXPERTS)
    later = jnp.concatenate([lax.cummin(ids, reverse=True)[1:],
                             jnp.full((1,), N_EXPERTS, jnp.int32)])
    return {
        "tile_expert": tile_expert,
        "tile_rows": tile_rows,
        "first_of_expert": (used & changed).astype(jnp.int32),
        "next_expert": jnp.where(later < N_EXPERTS, later, -1)[tile_expert].astype(jnp.int32),
        "last_tile": jnp.maximum(ends[-1] // TM_EXP - 1, 0).astype(jnp.int32).reshape(1),
        "tail_start": (starts + counts).astype(jnp.int32),
        "tail_len": (padded - counts).astype(jnp.int32),
        "run_dst": flat(starts[None, :] + runs[:, 0, :]),
        "run_len": flat(runs[:, 1, :]),
        "run_off": flat(runs[:, 2, :]),
    }


def kernel(x, mix_norm, w_in, pool_w, pool_scale, w_branch_pool, w_branch_attn, rel_bias, w_out,
           ffn_norm, w_router, b_router, w1, b1, w2, b2, final_norm):
    batch, seq, d = x.shape
    n = batch * seq
    depth = mix_norm.shape[0]
    assert depth == 1, "the combine kernel fuses the final norm, so only one layer is supported"
    token_tiles = n // min(TM_POST, n)
    max_rows = (n * TOP_K_EXPERTS + N_EXPERTS * token_tiles * (ROW_ALIGN - 1)
                + N_EXPERTS * (TM_EXP - 1))
    n_tiles = (max_rows + TM_EXP - 1) // TM_EXP
    xf = x.reshape(n, d)
    for l in range(depth):
        qT, k, vT, qiT, ki, wiT, pp, g1 = _inproj(
            xf, mix_norm[l], w_in[l], pool_w[l], pool_scale[l], w_branch_pool[l], seq)
        attn = _attention(qT, k, vT, qiT, ki, wiT, rel_bias, batch, seq)
        x1, h2, lpos, gates, runs, counts = _post_attn(
            attn, pp, g1, xf, w_branch_attn[l], w_out[l], ffn_norm[l], w_router[l], b_router[l])
        plan = _routing_plan(runs, counts, n_tiles)
        xs = _dispatch(h2, lpos, plan, n_tiles * TM_EXP)
        y = _experts(xs, plan["tile_expert"], plan["tile_rows"], plan["first_of_expert"],
                     plan["next_expert"], plan["last_tile"], w1[l], b1[l], w2[l], b2[l])
        xf = _combine(y, lpos, gates, x1, final_norm, plan)
    return xf.reshape(batch, seq, d)
```
